```python
import jax, jax.numpy as jnp
from jax import lax
import numpy as np

D_MODEL = 4096
BATCH = 16
SEQ = 256
DEPTH = 2
DEC_BATCH = 4
DEC_SEQ = 1024
PAST_LEN = 512

GRID_W = 64
N_EVEN = (DEPTH + 1) // 2
N_ODD = DEPTH // 2
N_SUB = 3
D_FF = 11008
MIX_W = D_MODEL
POOL_GROUPS = 4
POOL_WINDOWS = (2, 4, 8, 16)
POOL_W = MIX_W // 2
POOL_GW = POOL_W // POOL_GROUPS
FNET_GROUPS = 4
FNET_W = MIX_W - POOL_W
FNET_GW = FNET_W // FNET_GROUPS
RWKV_W = MIX_W // 2
RWKV_HEAD = 64
RWKV_H = RWKV_W // RWKV_HEAD
DECAY_LORA = 128
ICLR_LORA = 128
GATE_LORA = 480
RWKV_COLS = 3 * RWKV_W + GATE_LORA + 2 * DECAY_LORA + ICLR_LORA
ATT_W = MIX_W - RWKV_W
HEAD_DIM = 128
N_Q_HEADS = ATT_W // HEAD_DIM
N_KV_HEADS = 4
GQA_GROUP = N_Q_HEADS // N_KV_HEADS
KV_W = N_KV_HEADS * HEAD_DIM
ODD_IN_COLS = RWKV_COLS + ATT_W + 2 * KV_W
Q_BLOCK = 128
ROPE_THETA = 10000.0
ALPHA = (2 * DEPTH) ** 0.25
BETA = (8 * DEPTH) ** -0.25
LN_EPS = 1e-5
GN_EPS = 64e-5

kernel_name = 'hybrid_diffusion_prefix_step'


def layer_norm(x, g, b):
    xf = x.astype(jnp.float32)
    mu = jnp.mean(xf, -1, keepdims=True)
    var = jnp.mean(jnp.square(xf - mu), -1, keepdims=True)
    return ((xf - mu) * lax.rsqrt(var + LN_EPS)).astype(x.dtype) * g + b


def rms_norm(x, g):
    xf = x.astype(jnp.float32)
    return (xf * lax.rsqrt(jnp.mean(jnp.square(xf), -1, keepdims=True) + 1e-6)).astype(x.dtype) * g


def modulate(x, shift, scale):
    return x * (1 + scale) + shift


def deepnorm_residual(x, y, gate, g, b):
    return layer_norm(ALPHA * x + gate * y, g, b)


def swiglu(u, w_in, w_out):
    gate, up = jnp.split(u @ w_in, 2, axis=-1)
    return (jax.nn.silu(gate) * up) @ w_out


def centred_window_mean(u, w):
    n = u.shape[1]
    cs = jnp.cumsum(u.astype(jnp.float32), axis=1)
    cs = jnp.concatenate([jnp.zeros_like(cs[:, :1]), cs], axis=1)
    t = jnp.arange(n)
    lo = jnp.clip(t - w // 2, 0, n)
    hi = jnp.clip(t + w // 2, 0, n)
    cnt = (hi - lo).astype(jnp.float32)[None, :, None]
    return ((cs[:, hi] - cs[:, lo]) / cnt).astype(u.dtype)


def pool_mixer(u, pool_w, pool_scale):
    b, n, _ = u.shape
    ug = u.reshape(b, n, POOL_GROUPS, POOL_GW)
    pooled = jnp.stack([centred_window_mean(ug[:, :, gi], w) for gi, w in enumerate(POOL_WINDOWS)], axis=2) - ug
    y = jnp.einsum('bngc,gcd->bngd', pooled, pool_w)
    return y.reshape(b, n, POOL_W) * pool_scale


def fourier_mixer(u):
    b, n, _ = u.shape
    ug = u.reshape(b, n, FNET_GROUPS, FNET_GW).astype(jnp.float32)
    f = jnp.fft.fft2(ug, axes=(1, 3)).real * (n * FNET_GW) ** -0.5
    return f.reshape(b, n, FNET_W).astype(u.dtype)


def even_mixer(u, w_in, pool_w, pool_scale, w_out):
    p = u @ w_in
    y = jnp.concatenate([pool_mixer(p[..., :POOL_W], pool_w, pool_scale), fourier_mixer(p[..., POOL_W:])], axis=-1)
    return y @ w_out


def token_shift(p, mu):
    zero = jnp.zeros_like(p[:, :1])
    prev = jnp.concatenate([zero, p[:, :-1]], axis=1)
    nxt = jnp.concatenate([p[:, 1:], zero], axis=1)
    return p + (0.5 * (prev + nxt) - p) * mu


def rwkv_step(S, inp):
    w_t, r_t, k_t, v_t, kk_t, kka_t = inp
    sa = jnp.einsum('bhij,bhj->bhi', S, -kk_t)
    S = S * w_t[:, :, None, :] + sa[..., None] * kka_t[:, :, None, :] + v_t[..., None] * k_t[:, :, None, :]
    return S, jnp.einsum('bhij,bhj->bhi', S, r_t)


def rwkv7_mixer(p, shift_mu, decay_w0, decay_w2, iclr_a0, iclr_w2, gate_w2, k_k, k_a, r_k, gn_g, gn_b, init_state):
    b, n, _ = p.shape
    p = token_shift(p, shift_mu)
    splits = [RWKV_W, 2 * RWKV_W, 3 * RWKV_W, 3 * RWKV_W + GATE_LORA, 3 * RWKV_W + GATE_LORA + 2 * DECAY_LORA]
    r, k, v, g_lo, w_lo, a_lo = jnp.split(p, splits, axis=-1)
    w_log = decay_w0 + jnp.einsum('bndr,drc->bndc', jnp.tanh(w_lo.reshape(b, n, 2, DECAY_LORA)), decay_w2)
    w_log = -jax.nn.softplus(-w_log.astype(jnp.float32)) - 0.5
    decay = jnp.exp(-jnp.exp(w_log))
    a = jax.nn.sigmoid(iclr_a0 + a_lo @ iclr_w2)
    g = jax.nn.sigmoid(g_lo) @ gate_w2
    heads = lambda t: t.reshape(b, n, RWKV_H, RWKV_HEAD).astype(jnp.float32)
    kk = heads(k * k_k)
    kk = kk * lax.rsqrt(jnp.sum(jnp.square(kk), -1, keepdims=True) + 1e-12)
    rh, kh, vh, ah = heads(r), heads(k * (1 + (a - 1) * k_a)), heads(v), heads(a)
    tmaj = lambda t: jnp.moveaxis(t, 1, 0)
    shared = (tmaj(rh), tmaj(kh), tmaj(vh), tmaj(kk), tmaj(kk * ah))
    dh = decay.reshape(b, n, 2, RWKV_H, RWKV_HEAD)
    if init_state is None:
        init_state = jnp.zeros((b, 2, RWKV_H, RWKV_HEAD, RWKV_HEAD), jnp.float32)
    init_state = init_state.astype(jnp.float32)
    s_f, y_f = lax.scan(rwkv_step, init_state[:, 0], (tmaj(dh[:, :, 0]),) + shared)
    s_b, y_b = lax.scan(rwkv_step, init_state[:, 1], (tmaj(dh[:, :, 1]),) + shared, reverse=True)
    y = jnp.moveaxis(y_f + y_b, 0, 1)
    mu = jnp.mean(y, -1, keepdims=True)
    var = jnp.mean(jnp.square(y - mu), -1, keepdims=True)
    yn = ((y - mu) * lax.rsqrt(var + GN_EPS)).reshape(b, n, RWKV_W) * gn_g + gn_b
    bonus = (jnp.sum(rh * kh * r_k, -1, keepdims=True) * vh).reshape(b, n, RWKV_W)
    out = (yn + bonus).astype(p.dtype) * g
    return out, jnp.stack([s_f, s_b], axis=1)


def axial_rope(x):
    n = x.shape[1]
    rows = n // GRID_W
    row = jnp.repeat(jnp.arange(rows, dtype=jnp.float32), GRID_W)
    col = jnp.tile(jnp.arange(GRID_W, dtype=jnp.float32), rows)
    half = HEAD_DIM // 2
    quarter = half // 2
    inv_freq = ROPE_THETA ** (-jnp.arange(quarter, dtype=jnp.float32) / quarter)

    def rotate(xh, pos):
        ang = pos[:, None] * inv_freq[None, :]
        cos = jnp.cos(ang)[None, :, None, :]
        sin = jnp.sin(ang)[None, :, None, :]
        x1, x2 = xh[..., :quarter], xh[..., quarter:]
        return jnp.concatenate([x1 * cos - x2 * sin, x2 * cos + x1 * sin], axis=-1)

    xf = x.astype(jnp.float32)
    return jnp.concatenate([rotate(xf[..., :half], row), rotate(xf[..., half:], col)], axis=-1).astype(x.dtype)


def block_attention(q, k, v):
    b, n = q.shape[:2]
    nb = n // Q_BLOCK
    qb = jnp.moveaxis(q.reshape(b, nb, Q_BLOCK, N_KV_HEADS, GQA_GROUP, HEAD_DIM), 1, 0)
    scale = HEAD_DIM ** -0.5

    def one_block(qblk):
        s = jnp.einsum('bqhgd,bkhd->bhgqk', qblk, k).astype(jnp.float32) * scale
        pr = jax.nn.softmax(s, axis=-1).astype(v.dtype)
        return jnp.einsum('bhgqk,bkhd->bqhgd', pr, v)

    return jnp.moveaxis(lax.map(one_block, qb), 0, 1).reshape(b, n, ATT_W)


def odd_mixer(u, w_in, shift_mu, decay_w0, decay_w2, iclr_a0, iclr_w2, gate_w2, k_k, k_a, r_k, gn_g, gn_b,
              q_norm, k_norm, w_out, ctx_k, ctx_v, ctx_state):
    b, n, _ = u.shape
    p = u @ w_in
    p_rwkv, p_q, p_k, p_v = jnp.split(p, [RWKV_COLS, RWKV_COLS + ATT_W, RWKV_COLS + ATT_W + KV_W], axis=-1)
    y_rwkv, s_final = rwkv7_mixer(p_rwkv, shift_mu, decay_w0, decay_w2, iclr_a0, iclr_w2, gate_w2,
                                  k_k, k_a, r_k, gn_g, gn_b, ctx_state)
    q = rms_norm(p_q.reshape(b, n, N_Q_HEADS, HEAD_DIM), q_norm)
    k = rms_norm(p_k.reshape(b, n, N_KV_HEADS, HEAD_DIM), k_norm)
    v = p_v.reshape(b, n, N_KV_HEADS, HEAD_DIM)
    if ctx_k is None:
        o = block_attention(q.reshape(b, n, N_KV_HEADS, GQA_GROUP, HEAD_DIM), k, v)
        ctx_out = (k, v, s_final)
    else:
        q = axial_rope(q)
        keys = jnp.concatenate([ctx_k.astype(k.dtype), axial_rope(k)], axis=1)
        vals = jnp.concatenate([ctx_v.astype(v.dtype), v], axis=1)
        o = block_attention(q.reshape(b, n, N_KV_HEADS, GQA_GROUP, HEAD_DIM), keys, vals)
        ctx_out = None
    return jnp.concatenate([y_rwkv, o], axis=-1) @ w_out, ctx_out


def run_trunk(x, cond, ctx_cache, weights):
    (w_ada, b_ada, ln_g, ln_b, ffn_w_in, ffn_w_out, even_w_in, pool_w, pool_scale, even_w_out,
     odd_w_in, shift_mu, decay_w0, decay_w2, iclr_a0, iclr_w2, gate_w2, k_k, k_a, r_k, gn_g, gn_b,
     q_norm, k_norm, odd_w_out) = weights
    silu_c = jax.nn.silu(cond)
    new_k, new_v, new_s = [], [], []
    for l in range(DEPTH):
        i = l // 2
        mod = (silu_c @ w_ada[l] + b_ada[l]).reshape(-1, 1, N_SUB, 3, D_MODEL)
        shift, scale, gate = mod[:, :, :, 0], mod[:, :, :, 1], mod[:, :, :, 2]
        u = modulate(x, shift[:, :, 0], scale[:, :, 0])
        x = deepnorm_residual(x, 0.5 * swiglu(u, ffn_w_in[l, 0], ffn_w_out[l, 0]), gate[:, :, 0], ln_g[l, 0], ln_b[l, 0])
        u = modulate(x, shift[:, :, 1], scale[:, :, 1])
        if l % 2 == 0:
            y = even_mixer(u, even_w_in[i], pool_w[i], pool_scale[i], even_w_out[i])
        else:
            if ctx_cache is None:
                ctx = (None, None, None)
            else:
                ctx = (ctx_cache[0][:, i], ctx_cache[1][:, i], ctx_cache[2][:, i])
            y, ctx_out = odd_mixer(u, odd_w_in[i], shift_mu[i], decay_w0[i], decay_w2[i], iclr_a0[i], iclr_w2[i],
                                   gate_w2[i], k_k[i], k_a[i], r_k[i], gn_g[i], gn_b[i], q_norm[i], k_norm[i],
                                   odd_w_out[i], ctx[0], ctx[1], ctx[2])
            if ctx_out is not None:
                new_k.append(ctx_out[0])
                new_v.append(ctx_out[1])
                new_s.append(ctx_out[2])
        x = deepnorm_residual(x, y, gate[:, :, 1], ln_g[l, 1], ln_b[l, 1])
        u = modulate(x, shift[:, :, 2], scale[:, :, 2])
        x = deepnorm_residual(x, 0.5 * swiglu(u, ffn_w_in[l, 1], ffn_w_out[l, 1]), gate[:, :, 2], ln_g[l, 2], ln_b[l, 2])
    return x, new_k, new_v, new_s


def setup_inputs(seed: int = 0) -> dict:
    key = jax.random.key(seed)
    ks = iter(jax.random.split(key, 40))

    def nrm(shape, s):
        return jax.random.normal(next(ks), shape, jnp.float32) * s

    def uni(shape, lo, hi):
        return jax.random.uniform(next(ks), shape, jnp.float32, lo, hi)

    D = D_MODEL
    return {
        'x_prompt': nrm((BATCH, SEQ, D), 1.0),
        'x_sample': nrm((DEC_BATCH, DEC_SEQ, D), 1.0),
        'cache_k': nrm((DEC_BATCH, N_ODD, PAST_LEN, N_KV_HEADS, HEAD_DIM), 1.0),
        'cache_v': nrm((DEC_BATCH, N_ODD, PAST_LEN, N_KV_HEADS, HEAD_DIM), 1.0),
        'state_rwkv': nrm((DEC_BATCH, N_ODD, 2, RWKV_H, RWKV_HEAD, RWKV_HEAD), 0.3),
        'c': nrm((DEC_BATCH, D), 1.0),
        'c_ctx': nrm((D,), 1.0),
        'w_ada': nrm((DEPTH, D, N_SUB * 3 * D), 0.5 * D ** -0.5),
        'b_ada': nrm((DEPTH, N_SUB * 3 * D), 0.02),
        'ln_g': 1.0 + nrm((DEPTH, N_SUB, D), 0.02),
        'ln_b': nrm((DEPTH, N_SUB, D), 0.02),
        'ffn_w_in': nrm((DEPTH, 2, D, 2 * D_FF), D ** -0.5),
        'ffn_w_out': nrm((DEPTH, 2, D_FF, D), BETA * D_FF ** -0.5),
        'even_w_in': nrm((N_EVEN, D, MIX_W), D ** -0.5),
        'pool_w': nrm((N_EVEN, POOL_GROUPS, POOL_GW, POOL_GW), POOL_GW ** -0.5),
        'pool_scale': 1.0 + nrm((N_EVEN, POOL_W), 0.1),
        'even_w_out': nrm((N_EVEN, MIX_W, D), BETA * MIX_W ** -0.5),
        'odd_w_in': nrm((N_ODD, D, ODD_IN_COLS), D ** -0.5),
        'shift_mu': uni((N_ODD, RWKV_COLS), 0.0, 1.0),
        'decay_w0': uni((N_ODD, 2, RWKV_W), -3.0, 0.0),
        'decay_w2': nrm((N_ODD, 2, DECAY_LORA, RWKV_W), 0.1 * DECAY_LORA ** -0.5),
        'iclr_a0': nrm((N_ODD, RWKV_W), 0.1),
        'iclr_w2': nrm((N_ODD, ICLR_LORA, RWKV_W), 0.1 * ICLR_LORA ** -0.5),
        'gate_w2': nrm((N_ODD, GATE_LORA, RWKV_W), GATE_LORA ** -0.5),
        'k_k': 0.85 + nrm((N_ODD, RWKV_W), 0.05),
        'k_a': 1.0 + nrm((N_ODD, RWKV_W), 0.05),
        'r_k': nrm((N_ODD, RWKV_H, RWKV_HEAD), 0.1),
        'gn_g': 1.0 + nrm((N_ODD, RWKV_W), 0.02),
        'gn_b': nrm((N_ODD, RWKV_W), 0.02),
        'q_norm': 1.0 + nrm((N_ODD, HEAD_DIM), 0.02),
        'k_norm': 1.0 + nrm((N_ODD, HEAD_DIM), 0.02),
        'odd_w_out': nrm((N_ODD, MIX_W, D), BETA * MIX_W ** -0.5),
    }


def reference(x_prompt, x_sample, cache_k, cache_v, state_rwkv, c, c_ctx, w_ada, b_ada, ln_g, ln_b,
              ffn_w_in, ffn_w_out, even_w_in, pool_w, pool_scale, even_w_out, odd_w_in, shift_mu,
              decay_w0, decay_w2, iclr_a0, iclr_w2, gate_w2, k_k, k_a, r_k, gn_g, gn_b, q_norm, k_norm,
              odd_w_out):
    weights = (w_ada, b_ada, ln_g, ln_b, ffn_w_in, ffn_w_out, even_w_in, pool_w, pool_scale, even_w_out,
               odd_w_in, shift_mu, decay_w0, decay_w2, iclr_a0, iclr_w2, gate_w2, k_k, k_a, r_k, gn_g, gn_b,
               q_norm, k_norm, odd_w_out)
    y_prompt, new_k, new_v, new_s = run_trunk(x_prompt, c_ctx[None, :], None, weights)
    y_sample, _, _, _ = run_trunk(x_sample, c, (cache_k, cache_v, state_rwkv), weights)
    new_cache_k = jnp.stack(new_k, axis=1)
    new_cache_v = jnp.stack(new_v, axis=1)
    new_state_rwkv = jnp.stack(new_s, axis=1)
    return (y_prompt, y_sample, new_cache_k, new_cache_v, new_state_rwkv)
```

```python
import functools

import numpy as np
import jax
import jax.numpy as jnp
from jax import lax
from jax.experimental import pallas as pl
from jax.experimental.pallas import tpu as pltpu

F32 = jnp.float32
BF16 = jnp.bfloat16
HI = lax.Precision.HIGHEST

D = 4096
DEPTH = 2
N_SUB = 3
D_FF = 11008
N_PROMPT_SEQ = 16
PROMPT_LEN = 256
N_SAMPLE_SEQ = 4
SAMPLE_LEN = 1024
PAST_LEN = 512
GRID_W = 64
PROMPT_ROWS = N_PROMPT_SEQ * PROMPT_LEN
ROWS = PROMPT_ROWS + N_SAMPLE_SEQ * SAMPLE_LEN
RB = 1024
N_GROUPS = 8
POOL_WINDOWS = (2, 4, 8, 16)
GW = 512
RWKV_W = 2048
HEAD = 64
N_HEADS = 32
GATE_LORA = 480
GATE_PAD = 512
DECAY_LORA = 128
ICLR_LORA = 128
ATT_W = 2048
HEAD_DIM = 128
N_KV = 4
KV_W = N_KV * HEAD_DIM
ROPE_THETA = 10000.0
ALPHA = (2 * DEPTH) ** 0.25
LN_EPS = 1e-5
GN_EPS = 64e-5
CHUNK = 64

C_R, C_K, C_V, C_Q = 0, 2048, 4096, 6144
C_AK, C_AV, C_G, C_W, C_A = 8192, 8704, 9216, 9728, 9984
ODD_COLS = 10240

VMEM_LIMIT = 56 * 1024 * 1024


def _cparams(sem):
    return pltpu.CompilerParams(dimension_semantics=sem, vmem_limit_bytes=VMEM_LIMIT)


def _group_of_block(i, rows_per_block):
    r0 = i * rows_per_block
    return jnp.where(r0 < PROMPT_ROWS, 0, 1 + (r0 - PROMPT_ROWS) // SAMPLE_LEN)


def _ada_kernel(c_ref, w_ref, b_ref, o_ref):
    c = c_ref[...]
    s = (c * jax.nn.sigmoid(c)).astype(BF16)
    o_ref[...] = jnp.dot(s, w_ref[...].astype(BF16), preferred_element_type=F32) + b_ref[...]


def _ada(cond, w_ada, b_ada):
    tn = 512
    n = w_ada.shape[-1]
    return pl.pallas_call(
        _ada_kernel,
        grid=(DEPTH, n // tn),
        in_specs=[pl.BlockSpec((N_GROUPS, D), lambda l, j: (0, 0)),
                  pl.BlockSpec((None, D, tn), lambda l, j: (l, 0, j)),
                  pl.BlockSpec((None, 1, tn), lambda l, j: (l, 0, j))],
        out_specs=pl.BlockSpec((None, N_GROUPS, tn), lambda l, j: (l, 0, j)),
        out_shape=jax.ShapeDtypeStruct((DEPTH, N_GROUPS, n), F32),
        compiler_params=_cparams(("parallel", "parallel")),
        name="ada",
    )(cond, w_ada, b_ada)


def _mod_spec(tr, l, sub, kind):
    return pl.BlockSpec((None, None, None, 1, D),
                        lambda i: (l, sub * 3 + kind, _group_of_block(i, tr), 0, 0))


def _modulate_kernel(x_ref, shift_ref, scale_ref, u_ref):
    u_ref[...] = (x_ref[...] * (1.0 + scale_ref[...]) + shift_ref[...]).astype(u_ref.dtype)


def _modulate(x, mods, l, sub):
    tr = 256
    return pl.pallas_call(
        _modulate_kernel,
        grid=(ROWS // tr,),
        in_specs=[pl.BlockSpec((tr, D), lambda i: (i, 0)),
                  _mod_spec(tr, l, sub, 0), _mod_spec(tr, l, sub, 1)],
        out_specs=pl.BlockSpec((tr, D), lambda i: (i, 0)),
        out_shape=jax.ShapeDtypeStruct((ROWS, D), BF16),
        compiler_params=_cparams(("parallel",)),
        name="modulate",
    )(x, mods, mods)


def _ln_kernel(x_ref, y_ref, gate_ref, g_ref, b_ref, *rest, y_scale, emit_u):
    z = ALPHA * x_ref[...] + (y_scale * gate_ref[...]) * y_ref[...]
    mu = jnp.mean(z, axis=-1, keepdims=True)
    zc = z - mu
    var = jnp.mean(zc * zc, axis=-1, keepdims=True)
    xn = zc * lax.rsqrt(var + LN_EPS) * g_ref[...] + b_ref[...]
    if emit_u:
        shift_ref, scale_ref, xo_ref, u_ref = rest
        xo_ref[...] = xn
        u_ref[...] = (xn * (1.0 + scale_ref[...]) + shift_ref[...]).astype(u_ref.dtype)
    else:
        (xo_ref,) = rest
        xo_ref[...] = xn


def _ln_residual(x, y, mods, ln_g, ln_b, l, sub, y_scale, nxt):
    tr = 256
    row = pl.BlockSpec((tr, D), lambda i: (i, 0))
    vec = pl.BlockSpec((None, None, 1, D), lambda i: (l, sub, 0, 0))
    in_specs = [row, row, _mod_spec(tr, l, sub, 2), vec, vec]
    args = [x, y, mods, ln_g, ln_b]
    emit_u = nxt is not None
    if emit_u:
        in_specs += [_mod_spec(tr, nxt[0], nxt[1], 0), _mod_spec(tr, nxt[0], nxt[1], 1)]
        args += [mods, mods]
        out_specs = [row, row]
        out_shape = [jax.ShapeDtypeStruct((ROWS, D), F32), jax.ShapeDtypeStruct((ROWS, D), BF16)]
    else:
        out_specs = row
        out_shape = jax.ShapeDtypeStruct((ROWS, D), F32)
    return pl.pallas_call(
        functools.partial(_ln_kernel, y_scale=y_scale, emit_u=emit_u),
        grid=(ROWS // tr,),
        in_specs=in_specs, out_specs=out_specs, out_shape=out_shape,
        compiler_params=_cparams(("parallel",)),
        name="ln_residual",
    )(*args)


def _mm_kernel(a_ref, w_ref, o_ref):
    o_ref[...] = jnp.dot(a_ref[...], w_ref[...], preferred_element_type=F32).astype(o_ref.dtype)


def _mm(a, w, tm, tn, out_dtype, name):
    m, k = a.shape
    n = w.shape[1]
    return pl.pallas_call(
        _mm_kernel,
        grid=(m // tm, n // tn),
        in_specs=[pl.BlockSpec((tm, k), lambda i, j: (i, 0)),
                  pl.BlockSpec((k, tn), lambda i, j: (0, j))],
        out_specs=pl.BlockSpec((tm, tn), lambda i, j: (i, j)),
        out_shape=jax.ShapeDtypeStruct((m, n), out_dtype),
        compiler_params=_cparams(("parallel", "parallel")),
        name=name,
    )(a, w)


def _mm2_kernel(a1_ref, a2_ref, w1_ref, w2_ref, o_ref):
    acc = jnp.dot(a1_ref[...], w1_ref[...], preferred_element_type=F32)
    acc += jnp.dot(a2_ref[...], w2_ref[...], preferred_element_type=F32)
    o_ref[...] = acc.astype(o_ref.dtype)


def _mm2(a1, a2, w, tm, tn, out_dtype, name):
    m, k1 = a1.shape
    k2 = a2.shape[1]
    assert k1 == k2
    n = w.shape[1]
    return pl.pallas_call(
        _mm2_kernel,
        grid=(m // tm, n // tn),
        in_specs=[pl.BlockSpec((tm, k1), lambda i, j: (i, 0)),
                  pl.BlockSpec((tm, k2), lambda i, j: (i, 0)),
                  pl.BlockSpec((k1, tn), lambda i, j: (0, j)),
                  pl.BlockSpec((k2, tn), lambda i, j: (1, j))],
        out_specs=pl.BlockSpec((tm, tn), lambda i, j: (i, j)),
        out_shape=jax.ShapeDtypeStruct((m, n), out_dtype),
        compiler_params=_cparams(("parallel", "parallel")),
        name=name,
    )(a1, a2, w, w)


def _swiglu_kernel(a_ref, wg_ref, wu_ref, o_ref):
    a = a_ref[...]
    g = jnp.dot(a, wg_ref[...], preferred_element_type=F32)
    u = jnp.dot(a, wu_ref[...], preferred_element_type=F32)
    o_ref[...] = (g * jax.nn.sigmoid(g) * u).astype(o_ref.dtype)


def _swiglu_in(u, w_in):
    tm, tn = 1024, 256
    nb = D_FF // tn
    return pl.pallas_call(
        _swiglu_kernel,
        grid=(ROWS // tm, nb),
        in_specs=[pl.BlockSpec((tm, D), lambda i, j: (i, 0)),
                  pl.BlockSpec((D, tn), lambda i, j: (0, j)),
                  pl.BlockSpec((D, tn), lambda i, j: (0, nb + j))],
        out_specs=pl.BlockSpec((tm, tn), lambda i, j: (i, j)),
        out_shape=jax.ShapeDtypeStruct((ROWS, D_FF), BF16),
        compiler_params=_cparams(("parallel", "parallel")),
        name="swiglu_in",
    )(u, w_in, w_in)


def _even_tables():
    t = np.arange(RB)
    band = np.zeros((2, 4, RB, RB), np.float32)
    icnt = np.zeros((2, 4, RB, 1), np.float32)
    cn = np.zeros((2, RB, RB), np.float64)
    sn = np.zeros((2, RB, RB), np.float64)
    for kind, n in enumerate((PROMPT_LEN, SAMPLE_LEN)):
        seq = t // n
        pos = t % n
        for gi, w in enumerate(POOL_WINDOWS):
            lo = np.clip(pos - w // 2, 0, n)
            hi = np.clip(pos + w // 2, 0, n)
            same = seq[:, None] == seq[None, :]
            band[kind, gi] = (same & (pos[None, :] >= lo[:, None]) & (pos[None, :] < hi[:, None]))
            icnt[kind, gi, :, 0] = 1.0 / (hi - lo)
        same = (seq[:, None] == seq[None, :]).astype(np.float64)
        ang = 2.0 * np.pi * ((pos[:, None] * pos[None, :]) % n) / n
        scale = (n * GW) ** -0.5
        cn[kind] = np.cos(ang) * same * scale
        sn[kind] = np.sin(ang) * same * scale
    c = np.arange(GW)
    angc = 2.0 * np.pi * ((c[:, None] * c[None, :]) % GW) / GW
    return (band, icnt, cn.astype(np.float32), sn.astype(np.float32),
            np.cos(angc).astype(np.float32), np.sin(angc).astype(np.float32))


def _even_mid_kernel(p_ref, band_ref, icnt_ref, pw_ref, ps_ref, cn_ref, sn_ref, cc_ref, sc_ref, o_ref):
    j = pl.program_id(1)

    @pl.when(j < 4)
    def _():
        x = p_ref[...]
        xh = x.astype(BF16)
        xl = (x - xh.astype(F32)).astype(BF16)
        band = band_ref[...]
        wsum = (jnp.dot(band, xh, preferred_element_type=F32)
                + jnp.dot(band, xl, preferred_element_type=F32))
        pooled = wsum * icnt_ref[...] - x
        y = jnp.dot(pooled.astype(BF16), pw_ref[...], preferred_element_type=F32)
        o_ref[...] = (y * ps_ref[...]).astype(o_ref.dtype)

    @pl.when(j >= 4)
    def _():
        x = p_ref[...].astype(BF16)
        xc = jnp.dot(x, cc_ref[...], preferred_element_type=F32).astype(BF16)
        xs = jnp.dot(x, sc_ref[...], preferred_element_type=F32).astype(BF16)
        y = (jnp.dot(cn_ref[...], xc, preferred_element_type=F32)
             - jnp.dot(sn_ref[...], xs, preferred_element_type=F32))
        o_ref[...] = y.astype(o_ref.dtype)


def _even_mid(p, pool_w, pool_scale):
    band, icnt, cn, sn, cc, sc = _even_tables()
    band = jnp.asarray(band).astype(BF16)
    cn, sn, cc, sc = (jnp.asarray(t).astype(BF16) for t in (cn, sn, cc, sc))
    kind = lambda i: jnp.where(i * RB < PROMPT_ROWS, 0, 1)
    grp = lambda j: jnp.minimum(j, 3)
    return pl.pallas_call(
        _even_mid_kernel,
        grid=(ROWS // RB, 8),
        in_specs=[pl.BlockSpec((RB, GW), lambda i, j: (i, j)),
                  pl.BlockSpec((None, None, RB, RB), lambda i, j: (kind(i), grp(j), 0, 0)),
                  pl.BlockSpec((None, None, RB, 1), lambda i, j: (kind(i), grp(j), 0, 0)),
                  pl.BlockSpec((None, GW, GW), lambda i, j: (grp(j), 0, 0)),
                  pl.BlockSpec((None, 1, GW), lambda i, j: (grp(j), 0, 0)),
                  pl.BlockSpec((None, RB, RB), lambda i, j: (kind(i), 0, 0)),
                  pl.BlockSpec((None, RB, RB), lambda i, j: (kind(i), 0, 0)),
                  pl.BlockSpec((GW, GW), lambda i, j: (0, 0)),
                  pl.BlockSpec((GW, GW), lambda i, j: (0, 0))],
        out_specs=pl.BlockSpec((RB, GW), lambda i, j: (i, j)),
        out_shape=jax.ShapeDtypeStruct((ROWS, D), BF16),
        compiler_params=_cparams(("parallel", "arbitrary")),
        name="even_mid",
    )(p, band, jnp.asarray(icnt), pool_w, pool_scale, cn, sn, cc, sc)


def _head_sum_matrix(width, value):
    h = np.arange(width) // HEAD
    return (h[:, None] == h[None, :]).astype(np.float32) * value


def _token_shift(x, mu, seq_len):
    n = x.shape[0]
    pos = lax.broadcasted_iota(jnp.int32, x.shape, 0) % seq_len
    prev = jnp.where(pos == 0, 0.0, pltpu.roll(x, 1, 0))
    nxt = jnp.where(pos == seq_len - 1, 0.0, pltpu.roll(x, n - 1, 0))
    return x + (0.5 * (prev + nxt) - x) * mu


def _rwkv_prep_kernel(pr_ref, pk_ref, pv_ref, pg_ref, pw_ref, pa_ref,
                      mr_ref, mk_ref, mv_ref, mg_ref, mw_ref, ma_ref,
                      w0_ref, w2_ref, a0_ref, aw2_ref, gw2_ref, kk_ref, ka_ref, e_ref,
                      r_o, k_o, v_o, kk_o, kka_o, lw_o, g_o):
    i = pl.program_id(0)

    def body(seq_len):
        r = _token_shift(pr_ref[...], mr_ref[...], seq_len)
        k = _token_shift(pk_ref[...], mk_ref[...], seq_len)
        v = _token_shift(pv_ref[...], mv_ref[...], seq_len)
        g_lo = _token_shift(pg_ref[...], mg_ref[...], seq_len)
        w_lo = _token_shift(pw_ref[...], mw_ref[...], seq_len)
        a_lo = _token_shift(pa_ref[...], ma_ref[...], seq_len)
        tw = jnp.tanh(w_lo).astype(BF16)
        for d in range(2):
            x = w0_ref[d] + jnp.dot(tw[:, d * DECAY_LORA:(d + 1) * DECAY_LORA], w2_ref[d],
                                    preferred_element_type=F32)
            z = -x
            softplus = jnp.maximum(z, 0.0) + jnp.log(1.0 + jnp.exp(-jnp.abs(z)))
            lw_o[d] = -jnp.exp(-softplus - 0.5)
        a = jax.nn.sigmoid(a0_ref[...] + jnp.dot(a_lo.astype(BF16), aw2_ref[...], preferred_element_type=F32))
        g_o[...] = jnp.dot(jax.nn.sigmoid(g_lo).astype(BF16), gw2_ref[...], preferred_element_type=F32)
        kk = k * kk_ref[...]
        ss = jnp.dot(kk * kk, e_ref[...], preferred_element_type=F32, precision=HI)
        kk = kk * lax.rsqrt(ss + 1e-12)
        r_o[...] = r
        k_o[...] = k * (1.0 + (a - 1.0) * ka_ref[...])
        v_o[...] = v
        kk_o[...] = kk
        kka_o[...] = kk * a

    @pl.when(i * RB < PROMPT_ROWS)
    def _():
        body(PROMPT_LEN)

    @pl.when(i * RB >= PROMPT_ROWS)
    def _():
        body(SAMPLE_LEN)


def _rwkv_prep(p, mu, w0, w2, a0, aw2, gw2, k_k, k_a):
    cw = 256
    ncb = RWKV_W // cw
    e = jnp.asarray(_head_sum_matrix(cw, 1.0))
    pcol = lambda base, width: pl.BlockSpec((RB, width), lambda i, c: (i, base // width))
    pch = lambda base: pl.BlockSpec((RB, cw), lambda i, c: (i, base // cw + c))
    mcol = lambda base, width: pl.BlockSpec((1, width), lambda i, c: (0, base // width))
    mch = lambda base: pl.BlockSpec((1, cw), lambda i, c: (0, base // cw + c))
    vch = pl.BlockSpec((1, cw), lambda i, c: (0, c))
    out = pl.BlockSpec((RB, cw), lambda i, c: (i, c))
    sds = jax.ShapeDtypeStruct((ROWS, RWKV_W), F32)
    return pl.pallas_call(
        _rwkv_prep_kernel,
        grid=(ROWS // RB, ncb),
        in_specs=[pch(C_R), pch(C_K), pch(C_V), pcol(C_G, GATE_PAD), pcol(C_W, 2 * DECAY_LORA), pcol(C_A, ICLR_LORA),
                  mch(C_R), mch(C_K), mch(C_V), mcol(C_G, GATE_PAD), mcol(C_W, 2 * DECAY_LORA), mcol(C_A, ICLR_LORA),
                  pl.BlockSpec((2, 1, cw), lambda i, c: (0, 0, c)),
                  pl.BlockSpec((2, DECAY_LORA, cw), lambda i, c: (0, 0, c)),
                  vch,
                  pl.BlockSpec((ICLR_LORA, cw), lambda i, c: (0, c)),
                  pl.BlockSpec((GATE_PAD, cw), lambda i, c: (0, c)),
                  vch, vch,
                  pl.BlockSpec((cw, cw), lambda i, c: (0, 0))],
        out_specs=[out, out, out, out, out,
                   pl.BlockSpec((2, RB, cw), lambda i, c: (0, i, c)), out],
        out_shape=[sds, sds, sds, sds, sds, jax.ShapeDtypeStruct((2, ROWS, RWKV_W), F32), sds],
        compiler_params=_cparams(("parallel", "arbitrary")),
        name="rwkv_prep",
    )(p, p, p, p, p, p, mu, mu, mu, mu, mu, mu, w0, w2, a0, aw2, gw2, k_k, k_a, e)


def _dot(a, b):
    return jnp.dot(a, b, preferred_element_type=F32, precision=HI)


def _dot_nt(a, b):
    return lax.dot_general(a, b, (((1,), (1,)), ((), ())), preferred_element_type=F32, precision=HI)


def _rwkv_scan_kernel(*refs, reverse, has_init, heads):
    if has_init:
        lw_ref, r_ref, k_ref, v_ref, a_ref, b_ref, s0_ref, y_ref, s_ref = refs
    else:
        lw_ref, r_ref, k_ref, v_ref, a_ref, b_ref, y_ref, s_ref = refs
    c_idx = pl.program_id(2)
    L = CHUNK

    @pl.when(c_idx == 0)
    def _():
        if has_init:
            s_ref[...] = s0_ref[...]
        else:
            s_ref[...] = jnp.zeros_like(s_ref)

    row = lax.broadcasted_iota(jnp.int32, (L, L), 0)
    col = lax.broadcasted_iota(jnp.int32, (L, L), 1)
    if reverse:
        incl = col >= row
        strict = col > row
    else:
        incl = col <= row
        strict = col < row
    eye = (row == col).astype(F32)
    tri = incl.astype(F32)
    blk8 = (row // 8) == (col // 8)
    off = [((row // (2 * b)) == (col // (2 * b))) & ((row // b) != (col // b)) for b in (8, 16, 32)]
    row2 = lax.broadcasted_iota(jnp.int32, (L, 2 * L), 0)
    col2 = lax.broadcasted_iota(jnp.int32, (L, 2 * L), 1) % L
    incl2 = (col2 >= row2) if reverse else (col2 <= row2)

    for h in range(heads):
        hs = slice(h * HEAD, (h + 1) * HEAD)
        lw = lw_ref[:, hs]
        c = _dot(tri, lw)
        g_in = jnp.exp(c)
        g_ex = jnp.exp(c - lw)
        g_inv = jnp.exp(-c)
        v = v_ref[:, hs]
        at = a_ref[:, hs] * g_ex
        rt = r_ref[:, hs] * g_in
        bt = b_ref[:, hs] * g_inv
        kt = k_ref[:, hs] * g_inv
        ar = jnp.concatenate([at, rt], axis=0)
        bk = jnp.concatenate([bt, kt], axis=0)
        s0 = s_ref[h]
        sc = _dot_nt(ar, bk)
        ws = _dot_nt(ar, s0)
        n_mat = jnp.where(strict, sc[:L, :L], 0.0)
        m_mat = jnp.where(strict, sc[:L, L:], 0.0)
        pq = jnp.where(incl2, sc[L:, :], 0.0)
        w = ws[:L] + _dot(m_mat, v)
        n8 = jnp.where(blk8, n_mat, 0.0)
        n8_2 = _dot(n8, n8)
        n8_4 = _dot(n8_2, n8_2)
        t = _dot(_dot(eye - n8, eye + n8_2), eye + n8_4)
        for m in off:
            t = t - _dot(t, _dot(jnp.where(m, n_mat, 0.0), t))
        u = -_dot(t, w)
        uv = jnp.concatenate([u, v], axis=0)
        y_ref[:, hs] = ws[L:] + _dot(pq, uv)
        g_last = g_in[0:1, :] if reverse else g_in[L - 1:L, :]
        s_ref[h] = (s0 + _dot(uv.T, bk)) * g_last


def _rwkv_scan(lw, r, k, v, a, b, s0, *, direction, n_seq, seq_len, row0):
    heads = 8
    hb = N_HEADS // heads
    nc = seq_len // CHUNK
    cw = heads * HEAD
    reverse = direction == 1

    def rows(s, h, c):
        ce = nc - 1 - c if reverse else c
        return row0 // CHUNK + s * nc + ce

    blk = pl.BlockSpec((CHUNK, cw), lambda s, h, c: (rows(s, h, c), h))
    in_specs = [pl.BlockSpec((None, CHUNK, cw), lambda s, h, c: (direction, rows(s, h, c), h)),
                blk, blk, blk, blk, blk]
    args = [lw, r, k, v, a, b]
    if s0 is not None:
        in_specs.append(pl.BlockSpec((None, None, heads, HEAD, HEAD), lambda s, h, c: (s, direction, h, 0, 0)))
        args.append(s0)
    n_rows = n_seq * seq_len
    y, s = pl.pallas_call(
        functools.partial(_rwkv_scan_kernel, reverse=reverse, has_init=s0 is not None, heads=heads),
        grid=(n_seq, hb, nc),
        in_specs=in_specs,
        out_specs=[pl.BlockSpec((CHUNK, cw), lambda s, h, c: (s * nc + (nc - 1 - c if reverse else c), h)),
                   pl.BlockSpec((None, heads, HEAD, HEAD), lambda s, h, c: (s, h, 0, 0))],
        out_shape=[jax.ShapeDtypeStruct((n_rows, RWKV_W), F32),
                   jax.ShapeDtypeStruct((n_seq, N_HEADS, HEAD, HEAD), F32)],
        compiler_params=_cparams(("parallel", "parallel", "arbitrary")),
        name="rwkv_scan",
    )(*args)
    return y, s


def _rwkv_post_kernel(yf_ref, yb_ref, r_ref, k_ref, v_ref, g_ref, gg_ref, gb_ref, rk_ref, e_ref, o_ref):
    y = yf_ref[...] + yb_ref[...]
    e = e_ref[...]
    mu = _dot(y, e)
    yc = y - mu
    var = _dot(yc * yc, e)
    yn = yc * lax.rsqrt(var + GN_EPS) * gg_ref[...] + gb_ref[...]
    bonus = _dot(r_ref[...] * k_ref[...] * rk_ref[...], e) * float(HEAD) * v_ref[...]
    o_ref[...] = ((yn + bonus) * g_ref[...]).astype(o_ref.dtype)


def _rwkv_post(yf, yb, r, k, v, g, gn_g, gn_b, r_k):
    tr, cw = 512, 512
    e = jnp.asarray(_head_sum_matrix(cw, 1.0 / HEAD))
    blk = pl.BlockSpec((tr, cw), lambda i, c: (i, c))
    vec = pl.BlockSpec((1, cw), lambda i, c: (0, c))
    return pl.pallas_call(
        _rwkv_post_kernel,
        grid=(ROWS // tr, RWKV_W // cw),
        in_specs=[blk, blk, blk, blk, blk, blk, vec, vec, vec, pl.BlockSpec((cw, cw), lambda i, c: (0, 0))],
        out_specs=blk,
        out_shape=jax.ShapeDtypeStruct((ROWS, RWKV_W), BF16),
        compiler_params=_cparams(("parallel", "parallel")),
        name="rwkv_post",
    )(yf, yb, r, k, v, g, gn_g, gn_b, r_k, e)


def _rms(x, gain):
    return x * lax.rsqrt(jnp.mean(x * x, axis=-1, keepdims=True) + 1e-6) * gain


def _rope(x, cos, sin_signed):
    lane = lax.broadcasted_iota(jnp.int32, x.shape, 1)
    quarter = HEAD_DIM // 4
    partner = jnp.where(lane % (2 * quarter) < quarter,
                        pltpu.roll(x, HEAD_DIM - quarter, 1), pltpu.roll(x, quarter, 1))
    return x * cos + partner * sin_signed


def _rope_tables():
    n = SAMPLE_LEN
    quarter = HEAD_DIM // 4
    row = (np.arange(n) // GRID_W).astype(np.float64)
    col = (np.arange(n) % GRID_W).astype(np.float64)
    inv_freq = ROPE_THETA ** (-np.arange(quarter, dtype=np.float64) / quarter)
    ar = row[:, None] * inv_freq[None, :]
    ac = col[:, None] * inv_freq[None, :]
    cos = np.concatenate([np.cos(ar), np.cos(ar), np.cos(ac), np.cos(ac)], axis=1)
    sin = np.concatenate([-np.sin(ar), np.sin(ar), -np.sin(ac), np.sin(ac)], axis=1)
    return cos.astype(np.float32), sin.astype(np.float32)


def _attn_prompt_kernel(q_ref, k_ref, v_ref, qn_ref, kn_ref, o_ref, ko_ref, vo_ref):
    scale = HEAD_DIM ** -0.5
    kn = _rms(k_ref[...], kn_ref[...])
    v = v_ref[...]
    ko_ref[...] = kn
    vo_ref[...] = v
    kb = kn.astype(BF16)
    vb = v.astype(BF16)
    for j in range(ATT_W // KV_W):
        hs = slice(j * HEAD_DIM, (j + 1) * HEAD_DIM)
        q = _rms(q_ref[:, hs], qn_ref[...]).astype(BF16)
        s = lax.dot_general(q, kb, (((1,), (1,)), ((), ())), preferred_element_type=F32) * scale
        e = jnp.exp(s - jnp.max(s, axis=-1, keepdims=True))
        pr = e / jnp.sum(e, axis=-1, keepdims=True)
        o_ref[:, hs] = jnp.dot(pr.astype(BF16), vb, preferred_element_type=F32).astype(o_ref.dtype)


def _attn_prompt(p, q_norm, k_norm):
    n = PROMPT_LEN
    qw = ATT_W // N_KV
    return pl.pallas_call(
        _attn_prompt_kernel,
        grid=(N_PROMPT_SEQ, N_KV),
        in_specs=[pl.BlockSpec((n, qw), lambda s, g: (s, C_Q // qw + g)),
                  pl.BlockSpec((n, HEAD_DIM), lambda s, g: (s, C_AK // HEAD_DIM + g)),
                  pl.BlockSpec((n, HEAD_DIM), lambda s, g: (s, C_AV // HEAD_DIM + g)),
                  pl.BlockSpec((1, HEAD_DIM), lambda s, g: (0, 0)),
                  pl.BlockSpec((1, HEAD_DIM), lambda s, g: (0, 0))],
        out_specs=[pl.BlockSpec((n, qw), lambda s, g: (s, g)),
                   pl.BlockSpec((n, HEAD_DIM), lambda s, g: (s, g)),
                   pl.BlockSpec((n, HEAD_DIM), lambda s, g: (s, g))],
        out_shape=[jax.ShapeDtypeStruct((PROMPT_ROWS, ATT_W), BF16),
                   jax.ShapeDtypeStruct((PROMPT_ROWS, KV_W), F32),
                   jax.ShapeDtypeStruct((PROMPT_ROWS, KV_W), F32)],
        compiler_params=_cparams(("parallel", "parallel")),
        name="attn_prompt",
    )(p, p, p, q_norm, k_norm)


def _attn_sample_kernel(q_ref, k_ref, v_ref, ck_ref, cv_ref, qn_ref, kn_ref, cos_ref, sin_ref, o_ref):
    scale = HEAD_DIM ** -0.5
    tq = 256
    cos = cos_ref[...]
    sin = sin_ref[...]
    ks = _rope(_rms(k_ref[...], kn_ref[...]), cos, sin).astype(BF16)
    vs = v_ref[...].astype(BF16)
    kc = ck_ref[...].astype(BF16)
    vc = cv_ref[...].astype(BF16)
    nt = (((1,), (1,)), ((), ()))
    for j in range(ATT_W // KV_W):
        hs = slice(j * HEAD_DIM, (j + 1) * HEAD_DIM)
        for qb in range(SAMPLE_LEN // tq):
            rs = slice(qb * tq, (qb + 1) * tq)
            q = _rope(_rms(q_ref[rs, hs], qn_ref[...]), cos[rs], sin[rs]).astype(BF16)
            s1 = lax.dot_general(q, kc, nt, preferred_element_type=F32) * scale
            s2 = lax.dot_general(q, ks, nt, preferred_element_type=F32) * scale
            m = jnp.maximum(jnp.max(s1, axis=-1, keepdims=True), jnp.max(s2, axis=-1, keepdims=True))
            e1 = jnp.exp(s1 - m)
            e2 = jnp.exp(s2 - m)
            inv = 1.0 / (jnp.sum(e1, axis=-1, keepdims=True) + jnp.sum(e2, axis=-1, keepdims=True))
            o = (jnp.dot((e1 * inv).astype(BF16), vc, preferred_element_type=F32)
                 + jnp.dot((e2 * inv).astype(BF16), vs, preferred_element_type=F32))
            o_ref[rs, hs] = o.astype(o_ref.dtype)


def _attn_sample(p, cache_k, cache_v, q_norm, k_norm):
    n = SAMPLE_LEN
    qw = ATT_W // N_KV
    r0 = PROMPT_ROWS // n
    cos, sin = _rope_tables()
    return pl.pallas_call(
        _attn_sample_kernel,
        grid=(N_SAMPLE_SEQ, N_KV),
        in_specs=[pl.BlockSpec((n, qw), lambda s, g: (r0 + s, C_Q // qw + g)),
                  pl.BlockSpec((n, HEAD_DIM), lambda s, g: (r0 + s, C_AK // HEAD_DIM + g)),
                  pl.BlockSpec((n, HEAD_DIM), lambda s, g: (r0 + s, C_AV // HEAD_DIM + g)),
                  pl.BlockSpec((PAST_LEN, HEAD_DIM), lambda s, g: (s, g)),
                  pl.BlockSpec((PAST_LEN, HEAD_DIM), lambda s, g: (s, g)),
                  pl.BlockSpec((1, HEAD_DIM), lambda s, g: (0, 0)),
                  pl.BlockSpec((1, HEAD_DIM), lambda s, g: (0, 0)),
                  pl.BlockSpec((n, HEAD_DIM), lambda s, g: (0, 0)),
                  pl.BlockSpec((n, HEAD_DIM), lambda s, g: (0, 0))],
        out_specs=pl.BlockSpec((n, qw), lambda s, g: (s, g)),
        out_shape=jax.ShapeDtypeStruct((N_SAMPLE_SEQ * n, ATT_W), BF16),
        compiler_params=_cparams(("parallel", "parallel")),
        name="attn_sample",
    )(p, p, p, cache_k, cache_v, q_norm, k_norm, jnp.asarray(cos), jnp.asarray(sin))


def _pack_odd_w_in(w):
    r, k, v = w[:, 0:2048], w[:, 2048:4096], w[:, 4096:6144]
    g = w[:, 6144:6144 + GATE_LORA]
    wl = w[:, 6624:6880]
    al = w[:, 6880:7008]
    q, ak, av = w[:, 7008:9056], w[:, 9056:9568], w[:, 9568:10080]
    z = lambda n: jnp.zeros((w.shape[0], n), w.dtype)
    return jnp.concatenate([r, k, v, q, ak, av, g, z(GATE_PAD - GATE_LORA), wl, al, z(ODD_COLS - C_A - ICLR_LORA)], axis=1)


def _pack_shift_mu(mu):
    z = lambda n: jnp.zeros((n,), mu.dtype)
    out = jnp.concatenate([mu[0:6144], z(C_G - C_Q), mu[6144:6144 + GATE_LORA], z(GATE_PAD - GATE_LORA),
                           mu[6624:6880], mu[6880:7008], z(ODD_COLS - C_A - ICLR_LORA)])
    return out[None, :]


def kernel(x_prompt, x_sample, cache_k, cache_v, state_rwkv, c, c_ctx, w_ada, b_ada, ln_g, ln_b, ffn_w_in,
           ffn_w_out, even_w_in, pool_w, pool_scale, even_w_out, odd_w_in, shift_mu, decay_w0, decay_w2,
           iclr_a0, iclr_w2, gate_w2, k_k, k_a, r_k, gn_g, gn_b, q_norm, k_norm, odd_w_out):
    x = jnp.concatenate([x_prompt.reshape(PROMPT_ROWS, D), x_sample.reshape(ROWS - PROMPT_ROWS, D)], axis=0)
    cond = jnp.concatenate([c_ctx[None, :], c, jnp.zeros((N_GROUPS - 1 - c.shape[0], D), F32)], axis=0)
    mods = _ada(cond, w_ada, b_ada[:, None, :])
    mods = mods.reshape(DEPTH, N_GROUPS, 3 * N_SUB, D).transpose(0, 2, 1, 3)[:, :, :, None, :]
    ln_g4 = ln_g[:, :, None, :]
    ln_b4 = ln_b[:, :, None, :]

    u = _modulate(x, mods, 0, 0)
    new_k = new_v = new_s = None
    for l in range(DEPTH):
        i = l // 2
        h = _swiglu_in(u, ffn_w_in[l, 0].astype(BF16))
        y = _mm(h, ffn_w_out[l, 0].astype(BF16), 512, 256, F32, "ffn_out")
        x, u = _ln_residual(x, y, mods, ln_g4, ln_b4, l, 0, 0.5, (l, 1))
        if l % 2 == 0:
            p = _mm(u, even_w_in[i].astype(BF16), 1024, 512, F32, "even_in")
            mid = _even_mid(p, pool_w[i].astype(BF16), pool_scale[i].reshape(4, 1, GW))
            y = _mm(mid, even_w_out[i].astype(BF16), 1024, 512, F32, "even_out")
        else:
            p = _mm(u, _pack_odd_w_in(odd_w_in[i]).astype(BF16), 1024, 512, F32, "odd_in")
            gw2 = jnp.concatenate([gate_w2[i], jnp.zeros((GATE_PAD - GATE_LORA, RWKV_W), F32)], axis=0)
            r, k, v, kk, kka, lw, g = _rwkv_prep(
                p, _pack_shift_mu(shift_mu[i]), decay_w0[i][:, None, :], decay_w2[i].astype(BF16),
                iclr_a0[i][None, :], iclr_w2[i].astype(BF16), gw2.astype(BF16), k_k[i][None, :], k_a[i][None, :])
            ys, ss = [], []
            for d in range(2):
                yp, sp = _rwkv_scan(lw, r, k, v, kk, kka, None, direction=d,
                                    n_seq=N_PROMPT_SEQ, seq_len=PROMPT_LEN, row0=0)
                yl, _ = _rwkv_scan(lw, r, k, v, kk, kka, state_rwkv[:, i], direction=d,
                                   n_seq=N_SAMPLE_SEQ, seq_len=SAMPLE_LEN, row0=PROMPT_ROWS)
                ys.append(jnp.concatenate([yp, yl], axis=0))
                ss.append(sp)
            y_rwkv = _rwkv_post(ys[0], ys[1], r, k, v, g, gn_g[i][None, :], gn_b[i][None, :],
                                r_k[i].reshape(1, RWKV_W))
            o_p, nk, nv = _attn_prompt(p, q_norm[i][None, :], k_norm[i][None, :])
            o_s = _attn_sample(p, cache_k[:, i].reshape(N_SAMPLE_SEQ * PAST_LEN, KV_W),
                               cache_v[:, i].reshape(N_SAMPLE_SEQ * PAST_LEN, KV_W),
                               q_norm[i][None, :], k_norm[i][None, :])
            o = jnp.concatenate([o_p, o_s], axis=0)
            y = _mm2(y_rwkv, o, odd_w_out[i].astype(BF16), 1024, 512, F32, "odd_out")
            new_k = nk.reshape(N_PROMPT_SEQ, 1, PROMPT_LEN, N_KV, HEAD_DIM)
            new_v = nv.reshape(N_PROMPT_SEQ, 1, PROMPT_LEN, N_KV, HEAD_DIM)
            new_s = jnp.stack(ss, axis=1)[:, None]
        x, u = _ln_residual(x, y, mods, ln_g4, ln_b4, l, 1, 1.0, (l, 2))
        h = _swiglu_in(u, ffn_w_in[l, 1].astype(BF16))
        y = _mm(h, ffn_w_out[l, 1].astype(BF16), 512, 256, F32, "ffn_out")
        if l + 1 < DEPTH:
            x, u = _ln_residual(x, y, mods, ln_g4, ln_b4, l, 2, 0.5, (l + 1, 0))
        else:
            x = _ln_residual(x, y, mods, ln_g4, ln_b4, l, 2, 0.5, None)

    y_prompt = x[:PROMPT_ROWS].reshape(N_PROMPT_SEQ, PROMPT_LEN, D)
    y_sample = x[PROMPT_ROWS:].reshape(N_SAMPLE_SEQ, SAMPLE_LEN, D)
    return (y_prompt, y_sample, new_k, new_v, new_s)
```

```python
import functools

import numpy as np
import jax
import jax.numpy as jnp
from jax import lax
from jax.experimental import pallas as pl
from jax.experimental.pallas import tpu as pltpu

F32 = jnp.float32
BF16 = jnp.bfloat16

D = 4096
DEPTH = 2
N_SUB = 3
D_FF = 11008
N_PROMPT_SEQ = 16
PROMPT_LEN = 256
N_SAMPLE_SEQ = 4
SAMPLE_LEN = 1024
PAST_LEN = 512
GRID_W = 64
PROMPT_ROWS = N_PROMPT_SEQ * PROMPT_LEN
ROWS = PROMPT_ROWS + N_SAMPLE_SEQ * SAMPLE_LEN
RB = 1024
N_GROUPS = 8
POOL_WINDOWS = (2, 4, 8, 16)
GW = 512
RWKV_W = 2048
HEAD = 64
N_HEADS = 32
GATE_LORA = 480
GATE_PAD = 512
DECAY_LORA = 128
ICLR_LORA = 128
ATT_W = 2048
HEAD_DIM = 128
N_KV = 4
KV_W = N_KV * HEAD_DIM
ROPE_THETA = 10000.0
ALPHA = (2 * DEPTH) ** 0.25
LN_EPS = 1e-5
GN_EPS = 64e-5
CHUNK = 64
SCAN_PAIRS = 16

C_R, C_K, C_V, C_Q = 0, 2048, 4096, 6144
C_AK, C_AV, C_G, C_W, C_A = 8192, 8704, 9216, 9728, 9984
ODD_COLS = 10240

VMEM_LIMIT = 56 * 1024 * 1024


def _cparams(sem):
    return pltpu.CompilerParams(dimension_semantics=sem, vmem_limit_bytes=VMEM_LIMIT)


def _group_of_block(i, rows_per_block):
    r0 = i * rows_per_block
    return jnp.where(r0 < PROMPT_ROWS, 0, 1 + (r0 - PROMPT_ROWS) // SAMPLE_LEN)


def _ada_kernel(c_ref, w_ref, b_ref, o_ref):
    c = c_ref[...]
    s = (c * jax.nn.sigmoid(c)).astype(BF16)
    o_ref[...] = jnp.dot(s, w_ref[...].astype(BF16), preferred_element_type=F32) + b_ref[...]


def _ada(cond, w_ada, b_ada):
    tn = 512
    n = w_ada.shape[-1]
    return pl.pallas_call(
        _ada_kernel,
        grid=(DEPTH, n // tn),
        in_specs=[pl.BlockSpec((N_GROUPS, D), lambda l, j: (0, 0)),
                  pl.BlockSpec((None, D, tn), lambda l, j: (l, 0, j)),
                  pl.BlockSpec((None, 1, tn), lambda l, j: (l, 0, j))],
        out_specs=pl.BlockSpec((None, N_GROUPS, tn), lambda l, j: (l, 0, j)),
        out_shape=jax.ShapeDtypeStruct((DEPTH, N_GROUPS, n), F32),
        compiler_params=_cparams(("parallel", "parallel")),
        name="ada",
    )(cond, w_ada, b_ada)


def _mod_spec(tr, l, sub, kind):
    return pl.BlockSpec((None, None, None, 1, D),
                        lambda i: (l, sub * 3 + kind, _group_of_block(i, tr), 0, 0))


def _modulate_kernel(x_ref, shift_ref, scale_ref, u_ref):
    u_ref[...] = (x_ref[...] * (1.0 + scale_ref[...]) + shift_ref[...]).astype(u_ref.dtype)


def _modulate(x, mods, l, sub):
    tr = 256
    return pl.pallas_call(
        _modulate_kernel,
        grid=(ROWS // tr,),
        in_specs=[pl.BlockSpec((tr, D), lambda i: (i, 0)),
                  _mod_spec(tr, l, sub, 0), _mod_spec(tr, l, sub, 1)],
        out_specs=pl.BlockSpec((tr, D), lambda i: (i, 0)),
        out_shape=jax.ShapeDtypeStruct((ROWS, D), BF16),
        compiler_params=_cparams(("parallel",)),
        name="modulate",
    )(x, mods, mods)


def _ln_kernel(x_ref, y_ref, gate_ref, g_ref, b_ref, *rest, y_scale, emit_u):
    z = ALPHA * x_ref[...] + (y_scale * gate_ref[...]) * y_ref[...]
    mu = jnp.mean(z, axis=-1, keepdims=True)
    zc = z - mu
    var = jnp.mean(zc * zc, axis=-1, keepdims=True)
    xn = zc * lax.rsqrt(var + LN_EPS) * g_ref[...] + b_ref[...]
    if emit_u:
        shift_ref, scale_ref, xo_ref, u_ref = rest
        xo_ref[...] = xn
        u_ref[...] = (xn * (1.0 + scale_ref[...]) + shift_ref[...]).astype(u_ref.dtype)
    else:
        (xo_ref,) = rest
        xo_ref[...] = xn


def _ln_residual(x, y, mods, ln_g, ln_b, l, sub, y_scale, nxt):
    tr = 256
    row = pl.BlockSpec((tr, D), lambda i: (i, 0))
    vec = pl.BlockSpec((None, None, 1, D), lambda i: (l, sub, 0, 0))
    in_specs = [row, row, _mod_spec(tr, l, sub, 2), vec, vec]
    args = [x, y, mods, ln_g, ln_b]
    emit_u = nxt is not None
    if emit_u:
        in_specs += [_mod_spec(tr, nxt[0], nxt[1], 0), _mod_spec(tr, nxt[0], nxt[1], 1)]
        args += [mods, mods]
        out_specs = [row, row]
        out_shape = [jax.ShapeDtypeStruct((ROWS, D), F32), jax.ShapeDtypeStruct((ROWS, D), BF16)]
    else:
        out_specs = row
        out_shape = jax.ShapeDtypeStruct((ROWS, D), F32)
    return pl.pallas_call(
        functools.partial(_ln_kernel, y_scale=y_scale, emit_u=emit_u),
        grid=(ROWS // tr,),
        in_specs=in_specs, out_specs=out_specs, out_shape=out_shape,
        compiler_params=_cparams(("parallel",)),
        name="ln_residual",
    )(*args)


def _mm_kernel(a_ref, w_ref, o_ref):
    o_ref[...] = jnp.dot(a_ref[...], w_ref[...], preferred_element_type=F32).astype(o_ref.dtype)


def _mm(a, w, tm, tn, out_dtype, name, widx=()):
    m, k = a.shape
    n = w.shape[-1]
    return pl.pallas_call(
        _mm_kernel,
        grid=(m // tm, n // tn),
        in_specs=[pl.BlockSpec((tm, k), lambda i, j: (i, 0)),
                  pl.BlockSpec((None,) * len(widx) + (k, tn), lambda i, j: widx + (0, j))],
        out_specs=pl.BlockSpec((tm, tn), lambda i, j: (i, j)),
        out_shape=jax.ShapeDtypeStruct((m, n), out_dtype),
        compiler_params=_cparams(("parallel", "parallel")),
        name=name,
    )(a, w)


def _mm2_kernel(a1_ref, a2_ref, w1_ref, w2_ref, o_ref):
    acc = jnp.dot(a1_ref[...], w1_ref[...], preferred_element_type=F32)
    acc += jnp.dot(a2_ref[...], w2_ref[...], preferred_element_type=F32)
    o_ref[...] = acc.astype(o_ref.dtype)


def _mm2(a1, a2, w, tm, tn, out_dtype, name):
    m, k1 = a1.shape
    k2 = a2.shape[1]
    assert k1 == k2
    n = w.shape[1]
    return pl.pallas_call(
        _mm2_kernel,
        grid=(m // tm, n // tn),
        in_specs=[pl.BlockSpec((tm, k1), lambda i, j: (i, 0)),
                  pl.BlockSpec((tm, k2), lambda i, j: (i, 0)),
                  pl.BlockSpec((k1, tn), lambda i, j: (0, j)),
                  pl.BlockSpec((k2, tn), lambda i, j: (1, j))],
        out_specs=pl.BlockSpec((tm, tn), lambda i, j: (i, j)),
        out_shape=jax.ShapeDtypeStruct((m, n), out_dtype),
        compiler_params=_cparams(("parallel", "parallel")),
        name=name,
    )(a1, a2, w, w)


def _swiglu_kernel(a_ref, wg_ref, wu_ref, o_ref, wgb_ref, wub_ref):
    @pl.when(pl.program_id(1) == 0)
    def _():
        wgb_ref[...] = wg_ref[...].astype(BF16)
        wub_ref[...] = wu_ref[...].astype(BF16)

    a = a_ref[...]
    g = jnp.dot(a, wgb_ref[...], preferred_element_type=F32)
    u = jnp.dot(a, wub_ref[...], preferred_element_type=F32)
    o_ref[...] = (g * jax.nn.sigmoid(g) * u).astype(o_ref.dtype)


def _swiglu_in(u, w_in, l, j):
    tm, tn = 1024, 256
    nb = D_FF // tn
    return pl.pallas_call(
        _swiglu_kernel,
        grid=(nb, ROWS // tm),
        in_specs=[pl.BlockSpec((tm, D), lambda n, m: (m, 0)),
                  pl.BlockSpec((None, None, D, tn), lambda n, m: (l, j, 0, n)),
                  pl.BlockSpec((None, None, D, tn), lambda n, m: (l, j, 0, nb + n))],
        out_specs=pl.BlockSpec((tm, tn), lambda n, m: (m, n)),
        out_shape=jax.ShapeDtypeStruct((ROWS, D_FF), BF16),
        scratch_shapes=[pltpu.VMEM((D, tn), BF16), pltpu.VMEM((D, tn), BF16)],
        compiler_params=_cparams(("arbitrary", "arbitrary")),
        name="swiglu_in",
    )(u, w_in, w_in)


def _even_tables():
    t = np.arange(RB)
    band = np.zeros((2, 4, RB, RB), np.float32)
    icnt = np.zeros((2, 4, RB, 1), np.float32)
    cn = np.zeros((2, RB, RB), np.float64)
    sn = np.zeros((2, RB, RB), np.float64)
    for kind, n in enumerate((PROMPT_LEN, SAMPLE_LEN)):
        seq = t // n
        pos = t % n
        for gi, w in enumerate(POOL_WINDOWS):
            lo = np.clip(pos - w // 2, 0, n)
            hi = np.clip(pos + w // 2, 0, n)
            same = seq[:, None] == seq[None, :]
            band[kind, gi] = (same & (pos[None, :] >= lo[:, None]) & (pos[None, :] < hi[:, None]))
            icnt[kind, gi, :, 0] = 1.0 / (hi - lo)
        same = (seq[:, None] == seq[None, :]).astype(np.float64)
        ang = 2.0 * np.pi * ((pos[:, None] * pos[None, :]) % n) / n
        scale = (n * GW) ** -0.5
        cn[kind] = np.cos(ang) * same * scale
        sn[kind] = np.sin(ang) * same * scale
    c = np.arange(GW)
    angc = 2.0 * np.pi * ((c[:, None] * c[None, :]) % GW) / GW
    return (band, icnt, cn.astype(np.float32), sn.astype(np.float32),
            np.cos(angc).astype(np.float32), np.sin(angc).astype(np.float32))


def _even_mid_kernel(p_ref, band_ref, icnt_ref, pw_ref, ps_ref, cn_ref, sn_ref, cc_ref, sc_ref, o_ref):
    j = pl.program_id(1)

    @pl.when(j < 4)
    def _():
        x = p_ref[...]
        xh = x.astype(BF16)
        xl = (x - xh.astype(F32)).astype(BF16)
        band = band_ref[...]
        wsum = (jnp.dot(band, xh, preferred_element_type=F32)
                + jnp.dot(band, xl, preferred_element_type=F32))
        pooled = wsum * icnt_ref[...] - x
        y = jnp.dot(pooled.astype(BF16), pw_ref[...], preferred_element_type=F32)
        o_ref[...] = (y * ps_ref[...]).astype(o_ref.dtype)

    @pl.when(j >= 4)
    def _():
        x = p_ref[...].astype(BF16)
        xc = jnp.dot(x, cc_ref[...], preferred_element_type=F32).astype(BF16)
        xs = jnp.dot(x, sc_ref[...], preferred_element_type=F32).astype(BF16)
        y = (jnp.dot(cn_ref[...], xc, preferred_element_type=F32)
             - jnp.dot(sn_ref[...], xs, preferred_element_type=F32))
        o_ref[...] = y.astype(o_ref.dtype)


def _even_mid(p, pool_w, pool_scale):
    band, icnt, cn, sn, cc, sc = _even_tables()
    band = jnp.asarray(band).astype(BF16)
    cn, sn, cc, sc = (jnp.asarray(t).astype(BF16) for t in (cn, sn, cc, sc))
    kind = lambda i: jnp.where(i * RB < PROMPT_ROWS, 0, 1)
    grp = lambda j: jnp.minimum(j, 3)
    return pl.pallas_call(
        _even_mid_kernel,
        grid=(ROWS // RB, 8),
        in_specs=[pl.BlockSpec((RB, GW), lambda i, j: (i, j)),
                  pl.BlockSpec((None, None, RB, RB), lambda i, j: (kind(i), grp(j), 0, 0)),
                  pl.BlockSpec((None, None, RB, 1), lambda i, j: (kind(i), grp(j), 0, 0)),
                  pl.BlockSpec((None, GW, GW), lambda i, j: (grp(j), 0, 0)),
                  pl.BlockSpec((None, 1, GW), lambda i, j: (grp(j), 0, 0)),
                  pl.BlockSpec((None, RB, RB), lambda i, j: (kind(i), 0, 0)),
                  pl.BlockSpec((None, RB, RB), lambda i, j: (kind(i), 0, 0)),
                  pl.BlockSpec((GW, GW), lambda i, j: (0, 0)),
                  pl.BlockSpec((GW, GW), lambda i, j: (0, 0))],
        out_specs=pl.BlockSpec((RB, GW), lambda i, j: (i, j)),
        out_shape=jax.ShapeDtypeStruct((ROWS, D), BF16),
        compiler_params=_cparams(("parallel", "arbitrary")),
        name="even_mid",
    )(p, band, jnp.asarray(icnt), pool_w, pool_scale, cn, sn, cc, sc)


def _head_sum_matrix(width, value):
    h = np.arange(width) // HEAD
    return jnp.asarray((h[:, None] == h[None, :]).astype(np.float32) * value).astype(BF16)


def _split_bf16(x, n):
    parts = []
    for _ in range(n - 1):
        h = x.astype(BF16)
        parts.append(h)
        x = x - h.astype(F32)
    parts.append(x.astype(BF16))
    return parts


def _head_sum(x, e):
    hi, lo = _split_bf16(x, 2)
    return jnp.dot(hi, e, preferred_element_type=F32) + jnp.dot(lo, e, preferred_element_type=F32)


def _token_shift(x, mu, seq_len):
    n = x.shape[0]
    pos = lax.broadcasted_iota(jnp.int32, x.shape, 0) % seq_len
    prev = jnp.where(pos == 0, 0.0, pltpu.roll(x, 1, 0))
    nxt = jnp.where(pos == seq_len - 1, 0.0, pltpu.roll(x, n - 1, 0))
    return x + (0.5 * (prev + nxt) - x) * mu


def _rwkv_prep_kernel(pr_ref, pk_ref, pv_ref, pg_ref, pw_ref, pa_ref,
                      mr_ref, mk_ref, mv_ref, mg_ref, mw_ref, ma_ref,
                      w0_ref, w2_ref, a0_ref, aw2_ref, gw2_ref, kk_ref, ka_ref, e_ref,
                      r_o, k_o, v_o, kk_o, kka_o, lw_o, g_o):
    i = pl.program_id(0)

    def body(seq_len):
        r = _token_shift(pr_ref[...], mr_ref[...], seq_len)
        k = _token_shift(pk_ref[...], mk_ref[...], seq_len)
        v = _token_shift(pv_ref[...], mv_ref[...], seq_len)
        g_lo = _token_shift(pg_ref[...], mg_ref[...], seq_len)
        w_lo = _token_shift(pw_ref[...], mw_ref[...], seq_len)
        a_lo = _token_shift(pa_ref[...], ma_ref[...], seq_len)
        tw = jnp.tanh(w_lo).astype(BF16)
        for d in range(2):
            x = w0_ref[d] + jnp.dot(tw[:, d * DECAY_LORA:(d + 1) * DECAY_LORA], w2_ref[d],
                                    preferred_element_type=F32)
            z = -x
            softplus = jnp.maximum(z, 0.0) + jnp.log(1.0 + jnp.exp(-jnp.abs(z)))
            lw_o[d] = -jnp.exp(-softplus - 0.5)
        a = jax.nn.sigmoid(a0_ref[...] + jnp.dot(a_lo.astype(BF16), aw2_ref[...], preferred_element_type=F32))
        g_o[...] = jnp.dot(jax.nn.sigmoid(g_lo).astype(BF16), gw2_ref[...], preferred_element_type=F32)
        kk = k * kk_ref[...]
        ss = _head_sum(kk * kk, e_ref[...])
        kk = kk * lax.rsqrt(ss + 1e-12)
        r_o[...] = r
        k_o[...] = k * (1.0 + (a - 1.0) * ka_ref[...])
        v_o[...] = v
        kk_o[...] = kk
        kka_o[...] = kk * a

    @pl.when(i * RB < PROMPT_ROWS)
    def _():
        body(PROMPT_LEN)

    @pl.when(i * RB >= PROMPT_ROWS)
    def _():
        body(SAMPLE_LEN)


def _rwkv_prep(p, mu, w0, w2, a0, aw2, gw2, k_k, k_a):
    cw = 256
    ncb = RWKV_W // cw
    e = _head_sum_matrix(cw, 1.0)
    pcol = lambda base, width: pl.BlockSpec((RB, width), lambda i, c: (i, base // width))
    pch = lambda base: pl.BlockSpec((RB, cw), lambda i, c: (i, base // cw + c))
    mcol = lambda base, width: pl.BlockSpec((1, width), lambda i, c: (0, base // width))
    mch = lambda base: pl.BlockSpec((1, cw), lambda i, c: (0, base // cw + c))
    vch = pl.BlockSpec((1, cw), lambda i, c: (0, c))
    out = pl.BlockSpec((RB, cw), lambda i, c: (i, c))
    sds = jax.ShapeDtypeStruct((ROWS, RWKV_W), F32)
    return pl.pallas_call(
        _rwkv_prep_kernel,
        grid=(ROWS // RB, ncb),
        in_specs=[pch(C_R), pch(C_K), pch(C_V), pcol(C_G, GATE_PAD), pcol(C_W, 2 * DECAY_LORA), pcol(C_A, ICLR_LORA),
                  mch(C_R), mch(C_K), mch(C_V), mcol(C_G, GATE_PAD), mcol(C_W, 2 * DECAY_LORA), mcol(C_A, ICLR_LORA),
                  pl.BlockSpec((2, 1, cw), lambda i, c: (0, 0, c)),
                  pl.BlockSpec((2, DECAY_LORA, cw), lambda i, c: (0, 0, c)),
                  vch,
                  pl.BlockSpec((ICLR_LORA, cw), lambda i, c: (0, c)),
                  pl.BlockSpec((GATE_PAD, cw), lambda i, c: (0, c)),
                  vch, vch,
                  pl.BlockSpec((cw, cw), lambda i, c: (0, 0))],
        out_specs=[out, out, out, out, out,
                   pl.BlockSpec((2, RB, cw), lambda i, c: (0, i, c)), out],
        out_shape=[sds, sds, sds, sds, sds, jax.ShapeDtypeStruct((2, ROWS, RWKV_W), F32), sds],
        compiler_params=_cparams(("parallel", "arbitrary")),
        name="rwkv_prep",
    )(p, p, p, p, p, p, mu, mu, mu, mu, mu, mu, w0, w2, a0, aw2, gw2, k_k, k_a, e)


SCAN_PASSES = {"scores": 1, "ws": 1, "mv": 1, "inv": 1, "u": 1, "y": 1, "state": 1}


def _mmp(a, b, passes, nt=False):
    dims = (((1,), (1 if nt else 0,)), ((), ()))
    dot = lambda x, y: lax.dot_general(x, y, dims, preferred_element_type=F32)
    if passes == 1:
        return dot(a.astype(BF16), b.astype(BF16))
    ah, al = _split_bf16(a, 2)
    bh, bl = _split_bf16(b, 2)
    return dot(ah, bh) + (dot(al, bh) + dot(ah, bl))


def _rwkv_scan_kernel(*refs, reverse, has_init, pairs):
    if has_init:
        lw_ref, r_ref, k_ref, v_ref, a_ref, b_ref, s0_ref, y_ref, s_ref, sp_ref = refs
    else:
        lw_ref, r_ref, k_ref, v_ref, a_ref, b_ref, y_ref, s_ref, sp_ref = refs
    c_idx = pl.program_id(2)
    n_chunks = pl.num_programs(2)
    L = CHUNK
    W2 = 2 * HEAD
    P = SCAN_PASSES

    @pl.when(c_idx == 0)
    def _():
        sp_ref[...] = jnp.zeros_like(sp_ref)
        if has_init:
            for p in range(pairs):
                sp_ref[p, 0:HEAD, 0:HEAD] = s0_ref[2 * p]
                sp_ref[p, HEAD:W2, HEAD:W2] = s0_ref[2 * p + 1]

    row = lax.broadcasted_iota(jnp.int32, (L, W2), 0)
    lane = lax.broadcasted_iota(jnp.int32, (L, W2), 1)
    col = lane % HEAD
    first = lane < HEAD
    incl = (col >= row) if reverse else (col <= row)
    strict = (col > row) if reverse else (col < row)
    eye = (col == row).astype(F32)
    blk8 = (row // 8) == (col // 8)
    off = [((row // (2 * b)) == (col // (2 * b))) & ((row // b) != (col // b)) for b in (8, 16, 32)]
    row_s = lax.broadcasted_iota(jnp.int32, (W2, W2), 0)
    lane_s = lax.broadcasted_iota(jnp.int32, (W2, W2), 1)
    diag_blocks = (row_s < HEAD) == (lane_s < HEAD)
    trow = lax.broadcasted_iota(jnp.int32, (L, L), 0)
    tcol = lax.broadcasted_iota(jnp.int32, (L, L), 1)
    tri = ((tcol >= trow) if reverse else (tcol <= trow)).astype(BF16)

    def bd(x):
        return jnp.concatenate([jnp.where(first, x, 0.0), jnp.where(first, 0.0, x)], axis=0)

    lw = lw_ref[...]
    c = sum(jnp.dot(tri, t, preferred_element_type=F32) for t in _split_bf16(lw, 3))
    g_in = jnp.exp(c)
    g_inv = jnp.exp(-c)
    at_all = a_ref[...] * jnp.exp(c - lw)
    rt_all = r_ref[...] * g_in
    bt_all = b_ref[...] * g_inv
    kt_all = k_ref[...] * g_inv
    v_all = v_ref[...]
    ps = [slice(p * W2, (p + 1) * W2) for p in range(pairs)]
    each = lambda f, *xs: [f(*z) for z in zip(*xs)]

    ar = [jnp.concatenate([at_all[:, s], rt_all[:, s]], axis=0) for s in ps]
    bt = [bt_all[:, s] for s in ps]
    kt = [kt_all[:, s] for s in ps]
    v = [v_all[:, s] for s in ps]
    s0 = [sp_ref[p] for p in range(pairs)]
    npq = each(lambda x, y: _mmp(x, bd(y), P["scores"], nt=True), ar, bt)
    mq = each(lambda x, y: _mmp(x, bd(y), P["scores"], nt=True), ar, kt)
    ws = each(lambda x, y: _mmp(x, y, P["ws"], nt=True), ar, s0)
    n_mat = [jnp.where(strict, x[:L], 0.0) for x in npq]
    m_mat = [jnp.where(strict, x[:L], 0.0) for x in mq]
    p_mat = [jnp.where(incl, x[L:], 0.0) for x in npq]
    q_mat = [jnp.where(incl, x[L:], 0.0) for x in mq]
    w = each(lambda x, m, y: x[:L] + _mmp(m, bd(y), P["mv"]), ws, m_mat, v)
    n8 = [jnp.where(blk8, x, 0.0) for x in n_mat]
    n8_2 = each(lambda x: _mmp(x, bd(x), P["inv"]), n8)
    n8_4 = each(lambda x: _mmp(x, bd(x), P["inv"]), n8_2)
    t = each(lambda x, y: _mmp(eye - x, bd(eye + y), P["inv"]), n8, n8_2)
    t = each(lambda x, y: _mmp(x, bd(eye + y), P["inv"]), t, n8_4)
    for m in off:
        x = each(lambda n, y: _mmp(jnp.where(m, n, 0.0), bd(y), P["inv"]), n_mat, t)
        t = each(lambda y, z: y - _mmp(y, bd(z), P["inv"]), t, x)
    u = each(lambda x, y: -_mmp(x, bd(y), P["u"]), t, w)
    y = each(lambda x, pm, qm, uu, vv: x[L:] + _mmp(jnp.concatenate([pm, qm], axis=1),
                                                     jnp.concatenate([bd(uu), bd(vv)], axis=0), P["y"]),
             ws, p_mat, q_mat, u, v)
    for p in range(pairs):
        y_ref[:, ps[p]] = y[p]
    g_last = g_in[0:1, :] if reverse else g_in[L - 1:L, :]
    upd = each(lambda uu, vv, b, k: _mmp(jnp.concatenate([uu, vv], axis=0).T,
                                         jnp.concatenate([b, k], axis=0), P["state"]), u, v, bt, kt)
    s_new = [(s0[p] + jnp.where(diag_blocks, upd[p], 0.0)) * g_last[:, ps[p]] for p in range(pairs)]
    for p in range(pairs):
        sp_ref[p] = s_new[p]

    @pl.when(c_idx == n_chunks - 1)
    def _():
        for p in range(pairs):
            s_ref[2 * p] = s_new[p][0:HEAD, 0:HEAD]
            s_ref[2 * p + 1] = s_new[p][HEAD:W2, HEAD:W2]


def _rwkv_scan(lw, r, k, v, a, b, s0, *, direction, n_seq, seq_len, row0):
    pairs = SCAN_PAIRS
    heads = 2 * pairs
    hb = N_HEADS // heads
    nc = seq_len // CHUNK
    cw = heads * HEAD
    reverse = direction == 1

    def rows(s, h, c):
        ce = nc - 1 - c if reverse else c
        return row0 // CHUNK + s * nc + ce

    blk = pl.BlockSpec((CHUNK, cw), lambda s, h, c: (rows(s, h, c), h))
    in_specs = [pl.BlockSpec((None, CHUNK, cw), lambda s, h, c: (direction, rows(s, h, c), h)),
                blk, blk, blk, blk, blk]
    args = [lw, r, k, v, a, b]
    if s0 is not None:
        in_specs.append(pl.BlockSpec((None, None, heads, HEAD, HEAD), lambda s, h, c: (s, direction, h, 0, 0)))
        args.append(s0)
    n_rows = n_seq * seq_len
    y, s = pl.pallas_call(
        functools.partial(_rwkv_scan_kernel, reverse=reverse, has_init=s0 is not None, pairs=pairs),
        grid=(n_seq, hb, nc),
        in_specs=in_specs,
        out_specs=[pl.BlockSpec((CHUNK, cw), lambda s, h, c: (s * nc + (nc - 1 - c if reverse else c), h)),
                   pl.BlockSpec((None, heads, HEAD, HEAD), lambda s, h, c: (s, h, 0, 0))],
        out_shape=[jax.ShapeDtypeStruct((n_rows, RWKV_W), F32),
                   jax.ShapeDtypeStruct((n_seq, N_HEADS, HEAD, HEAD), F32)],
        scratch_shapes=[pltpu.VMEM((pairs, 2 * HEAD, 2 * HEAD), F32)],
        compiler_params=_cparams(("parallel", "parallel", "arbitrary")),
        name="rwkv_scan",
    )(*args)
    return y, s


def _rwkv_post_kernel(yf_ref, yb_ref, r_ref, k_ref, v_ref, g_ref, gg_ref, gb_ref, rk_ref, e_ref, o_ref):
    y = yf_ref[...] + yb_ref[...]
    e = e_ref[...]
    mu = _head_sum(y, e)
    yc = y - mu
    var = _head_sum(yc * yc, e)
    yn = yc * lax.rsqrt(var + GN_EPS) * gg_ref[...] + gb_ref[...]
    bonus = _head_sum(r_ref[...] * k_ref[...] * rk_ref[...], e) * float(HEAD) * v_ref[...]
    o_ref[...] = ((yn + bonus) * g_ref[...]).astype(o_ref.dtype)


def _rwkv_post(yf, yb, r, k, v, g, gn_g, gn_b, r_k):
    tr, cw = 512, 512
    e = _head_sum_matrix(cw, 1.0 / HEAD)
    blk = pl.BlockSpec((tr, cw), lambda i, c: (i, c))
    vec = pl.BlockSpec((1, cw), lambda i, c: (0, c))
    return pl.pallas_call(
        _rwkv_post_kernel,
        grid=(ROWS // tr, RWKV_W // cw),
        in_specs=[blk, blk, blk, blk, blk, blk, vec, vec, vec, pl.BlockSpec((cw, cw), lambda i, c: (0, 0))],
        out_specs=blk,
        out_shape=jax.ShapeDtypeStruct((ROWS, RWKV_W), BF16),
        compiler_params=_cparams(("parallel", "parallel")),
        name="rwkv_post",
    )(yf, yb, r, k, v, g, gn_g, gn_b, r_k, e)


def _rms(x, gain):
    return x * lax.rsqrt(jnp.mean(x * x, axis=-1, keepdims=True) + 1e-6) * gain


def _rope(x, cos, sin_signed):
    lane = lax.broadcasted_iota(jnp.int32, x.shape, 1)
    quarter = HEAD_DIM // 4
    partner = jnp.where(lane % (2 * quarter) < quarter,
                        pltpu.roll(x, HEAD_DIM - quarter, 1), pltpu.roll(x, quarter, 1))
    return x * cos + partner * sin_signed


def _rope_tables():
    n = SAMPLE_LEN
    quarter = HEAD_DIM // 4
    row = (np.arange(n) // GRID_W).astype(np.float64)
    col = (np.arange(n) % GRID_W).astype(np.float64)
    inv_freq = ROPE_THETA ** (-np.arange(quarter, dtype=np.float64) / quarter)
    ar = row[:, None] * inv_freq[None, :]
    ac = col[:, None] * inv_freq[None, :]
    cos = np.concatenate([np.cos(ar), np.cos(ar), np.cos(ac), np.cos(ac)], axis=1)
    sin = np.concatenate([-np.sin(ar), np.sin(ar), -np.sin(ac), np.sin(ac)], axis=1)
    return cos.astype(np.float32), sin.astype(np.float32)


def _attn_prompt_kernel(q_ref, k_ref, v_ref, qn_ref, kn_ref, o_ref, ko_ref, vo_ref):
    scale = HEAD_DIM ** -0.5
    kn = _rms(k_ref[...], kn_ref[...])
    v = v_ref[...]
    ko_ref[...] = kn
    vo_ref[...] = v
    kb = kn.astype(BF16)
    vb = v.astype(BF16)
    for j in range(ATT_W // KV_W):
        hs = slice(j * HEAD_DIM, (j + 1) * HEAD_DIM)
        q = _rms(q_ref[:, hs], qn_ref[...]).astype(BF16)
        s = lax.dot_general(q, kb, (((1,), (1,)), ((), ())), preferred_element_type=F32) * scale
        e = jnp.exp(s - jnp.max(s, axis=-1, keepdims=True))
        pr = e / jnp.sum(e, axis=-1, keepdims=True)
        o_ref[:, hs] = jnp.dot(pr.astype(BF16), vb, preferred_element_type=F32).astype(o_ref.dtype)


def _attn_prompt(p, q_norm, k_norm):
    n = PROMPT_LEN
    qw = ATT_W // N_KV
    return pl.pallas_call(
        _attn_prompt_kernel,
        grid=(N_PROMPT_SEQ, N_KV),
        in_specs=[pl.BlockSpec((n, qw), lambda s, g: (s, C_Q // qw + g)),
                  pl.BlockSpec((n, HEAD_DIM), lambda s, g: (s, C_AK // HEAD_DIM + g)),
                  pl.BlockSpec((n, HEAD_DIM), lambda s, g: (s, C_AV // HEAD_DIM + g)),
                  pl.BlockSpec((1, HEAD_DIM), lambda s, g: (0, 0)),
                  pl.BlockSpec((1, HEAD_DIM), lambda s, g: (0, 0))],
        out_specs=[pl.BlockSpec((n, qw), lambda s, g: (s, g)),
                   pl.BlockSpec((n, HEAD_DIM), lambda s, g: (s, g)),
                   pl.BlockSpec((n, HEAD_DIM), lambda s, g: (s, g))],
        out_shape=[jax.ShapeDtypeStruct((PROMPT_ROWS, ATT_W), BF16),
                   jax.ShapeDtypeStruct((PROMPT_ROWS, KV_W), F32),
                   jax.ShapeDtypeStruct((PROMPT_ROWS, KV_W), F32)],
        compiler_params=_cparams(("parallel", "parallel")),
        name="attn_prompt",
    )(p, p, p, q_norm, k_norm)


def _attn_sample_kernel(q_ref, k_ref, v_ref, ck_ref, cv_ref, qn_ref, kn_ref, cos_ref, sin_ref, o_ref):
    scale = HEAD_DIM ** -0.5
    tq = 256
    cos = cos_ref[...]
    sin = sin_ref[...]
    ks = _rope(_rms(k_ref[...], kn_ref[...]), cos, sin).astype(BF16)
    vs = v_ref[...].astype(BF16)
    kc = ck_ref[...].astype(BF16)
    vc = cv_ref[...].astype(BF16)
    nt = (((1,), (1,)), ((), ()))
    for j in range(ATT_W // KV_W):
        hs = slice(j * HEAD_DIM, (j + 1) * HEAD_DIM)
        for qb in range(SAMPLE_LEN // tq):
            rs = slice(qb * tq, (qb + 1) * tq)
            q = _rope(_rms(q_ref[rs, hs], qn_ref[...]), cos[rs], sin[rs]).astype(BF16)
            s1 = lax.dot_general(q, kc, nt, preferred_element_type=F32) * scale
            s2 = lax.dot_general(q, ks, nt, preferred_element_type=F32) * scale
            m = jnp.maximum(jnp.max(s1, axis=-1, keepdims=True), jnp.max(s2, axis=-1, keepdims=True))
            e1 = jnp.exp(s1 - m)
            e2 = jnp.exp(s2 - m)
            inv = 1.0 / (jnp.sum(e1, axis=-1, keepdims=True) + jnp.sum(e2, axis=-1, keepdims=True))
            o = (jnp.dot((e1 * inv).astype(BF16), vc, preferred_element_type=F32)
                 + jnp.dot((e2 * inv).astype(BF16), vs, preferred_element_type=F32))
            o_ref[rs, hs] = o.astype(o_ref.dtype)


def _attn_sample(p, cache_k, cache_v, q_norm, k_norm):
    n = SAMPLE_LEN
    qw = ATT_W // N_KV
    r0 = PROMPT_ROWS // n
    cos, sin = _rope_tables()
    return pl.pallas_call(
        _attn_sample_kernel,
        grid=(N_SAMPLE_SEQ, N_KV),
        in_specs=[pl.BlockSpec((n, qw), lambda s, g: (r0 + s, C_Q // qw + g)),
                  pl.BlockSpec((n, HEAD_DIM), lambda s, g: (r0 + s, C_AK // HEAD_DIM + g)),
                  pl.BlockSpec((n, HEAD_DIM), lambda s, g: (r0 + s, C_AV // HEAD_DIM + g)),
                  pl.BlockSpec((PAST_LEN, HEAD_DIM), lambda s, g: (s, g)),
                  pl.BlockSpec((PAST_LEN, HEAD_DIM), lambda s, g: (s, g)),
                  pl.BlockSpec((1, HEAD_DIM), lambda s, g: (0, 0)),
                  pl.BlockSpec((1, HEAD_DIM), lambda s, g: (0, 0)),
                  pl.BlockSpec((n, HEAD_DIM), lambda s, g: (0, 0)),
                  pl.BlockSpec((n, HEAD_DIM), lambda s, g: (0, 0))],
        out_specs=pl.BlockSpec((n, qw), lambda s, g: (s, g)),
        out_shape=jax.ShapeDtypeStruct((N_SAMPLE_SEQ * n, ATT_W), BF16),
        compiler_params=_cparams(("parallel", "parallel")),
        name="attn_sample",
    )(p, p, p, cache_k, cache_v, q_norm, k_norm, jnp.asarray(cos), jnp.asarray(sin))


def _pack_odd_w_in(w):
    r, k, v = w[:, 0:2048], w[:, 2048:4096], w[:, 4096:6144]
    g = w[:, 6144:6144 + GATE_LORA]
    wl = w[:, 6624:6880]
    al = w[:, 6880:7008]
    q, ak, av = w[:, 7008:9056], w[:, 9056:9568], w[:, 9568:10080]
    z = lambda n: jnp.zeros((w.shape[0], n), w.dtype)
    return jnp.concatenate([r, k, v, q, ak, av, g, z(GATE_PAD - GATE_LORA), wl, al, z(ODD_COLS - C_A - ICLR_LORA)], axis=1)


def _pack_shift_mu(mu):
    z = lambda n: jnp.zeros((n,), mu.dtype)
    out = jnp.concatenate([mu[0:6144], z(C_G - C_Q), mu[6144:6144 + GATE_LORA], z(GATE_PAD - GATE_LORA),
                           mu[6624:6880], mu[6880:7008], z(ODD_COLS - C_A - ICLR_LORA)])
    return out[None, :]


def kernel(x_prompt, x_sample, cache_k, cache_v, state_rwkv, c, c_ctx, w_ada, b_ada, ln_g, ln_b, ffn_w_in,
           ffn_w_out, even_w_in, pool_w, pool_scale, even_w_out, odd_w_in, shift_mu, decay_w0, decay_w2,
           iclr_a0, iclr_w2, gate_w2, k_k, k_a, r_k, gn_g, gn_b, q_norm, k_norm, odd_w_out):
    x = jnp.concatenate([x_prompt.reshape(PROMPT_ROWS, D), x_sample.reshape(ROWS - PROMPT_ROWS, D)], axis=0)
    cond = jnp.concatenate([c_ctx[None, :], c, jnp.zeros((N_GROUPS - 1 - c.shape[0], D), F32)], axis=0)
    mods = _ada(cond, w_ada, b_ada[:, None, :])
    mods = mods.reshape(DEPTH, N_GROUPS, 3 * N_SUB, D).transpose(0, 2, 1, 3)[:, :, :, None, :]
    ln_g4 = ln_g[:, :, None, :]
    ln_b4 = ln_b[:, :, None, :]
    ffn_w_out_b = ffn_w_out.astype(BF16)

    u = _modulate(x, mods, 0, 0)
    new_k = new_v = new_s = None
    for l in range(DEPTH):
        i = l // 2
        h = _swiglu_in(u, ffn_w_in, l, 0)
        y = _mm(h, ffn_w_out_b, 512, 256, F32, "ffn_out", widx=(l, 0))
        x, u = _ln_residual(x, y, mods, ln_g4, ln_b4, l, 0, 0.5, (l, 1))
        if l % 2 == 0:
            p = _mm(u, even_w_in[i].astype(BF16), 1024, 512, F32, "even_in")
            mid = _even_mid(p, pool_w[i].astype(BF16), pool_scale[i].reshape(4, 1, GW))
            y = _mm(mid, even_w_out[i].astype(BF16), 1024, 512, F32, "even_out")
        else:
            p = _mm(u, _pack_odd_w_in(odd_w_in[i]).astype(BF16), 1024, 512, F32, "odd_in")
            gw2 = jnp.concatenate([gate_w2[i], jnp.zeros((GATE_PAD - GATE_LORA, RWKV_W), F32)], axis=0)
            r, k, v, kk, kka, lw, g = _rwkv_prep(
                p, _pack_shift_mu(shift_mu[i]), decay_w0[i][:, None, :], decay_w2[i].astype(BF16),
                iclr_a0[i][None, :], iclr_w2[i].astype(BF16), gw2.astype(BF16), k_k[i][None, :], k_a[i][None, :])
            ys, ss = [], []
            for d in range(2):
                yp, sp = _rwkv_scan(lw, r, k, v, kk, kka, None, direction=d,
                                    n_seq=N_PROMPT_SEQ, seq_len=PROMPT_LEN, row0=0)
                yl, _ = _rwkv_scan(lw, r, k, v, kk, kka, state_rwkv[:, i], direction=d,
                                   n_seq=N_SAMPLE_SEQ, seq_len=SAMPLE_LEN, row0=PROMPT_ROWS)
                ys.append(jnp.concatenate([yp, yl], axis=0))
                ss.append(sp)
            y_rwkv = _rwkv_post(ys[0], ys[1], r, k, v, g, gn_g[i][None, :], gn_b[i][None, :],
                                r_k[i].reshape(1, RWKV_W))
            o_p, nk, nv = _attn_prompt(p, q_norm[i][None, :], k_norm[i][None, :])
            o_s = _attn_sample(p, cache_k[:, i].reshape(N_SAMPLE_SEQ * PAST_LEN, KV_W),
                               cache_v[:, i].reshape(N_SAMPLE_SEQ * PAST_LEN, KV_W),
                               q_norm[i][None, :], k_norm[i][None, :])
            o = jnp.concatenate([o_p, o_s], axis=0)
            y = _mm2(y_rwkv, o, odd_w_out[i].astype(BF16), 1024, 512, F32, "odd_out")
            new_k = nk.reshape(N_PROMPT_SEQ, 1, PROMPT_LEN, N_KV, HEAD_DIM)
            new_v = nv.reshape(N_PROMPT_SEQ, 1, PROMPT_LEN, N_KV, HEAD_DIM)
            new_s = jnp.stack(ss, axis=1)[:, None]
        x, u = _ln_residual(x, y, mods, ln_g4, ln_b4, l, 1, 1.0, (l, 2))
        h = _swiglu_in(u, ffn_w_in, l, 1)
        y = _mm(h, ffn_w_out_b, 512, 256, F32, "ffn_out", widx=(l, 1))
        if l + 1 < DEPTH:
            x, u = _ln_residual(x, y, mods, ln_g4, ln_b4, l, 2, 0.5, (l + 1, 0))
        else:
            x = _ln_residual(x, y, mods, ln_g4, ln_b4, l, 2, 0.5, None)

    y_prompt = x[:PROMPT_ROWS].reshape(N_PROMPT_SEQ, PROMPT_LEN, D)
    y_sample = x[PROMPT_ROWS:].reshape(N_SAMPLE_SEQ, SAMPLE_LEN, D)
    return (y_prompt, y_sample, new_k, new_v, new_s)
```

```python
import functools

import numpy as np
import jax
import jax.numpy as jnp
from jax import lax
from jax.experimental import pallas as pl
from jax.experimental.pallas import tpu as pltpu

F32 = jnp.float32
BF16 = jnp.bfloat16

D = 4096
DEPTH = 2
N_SUB = 3
D_FF = 11008
N_PROMPT_SEQ = 16
PROMPT_LEN = 256
N_SAMPLE_SEQ = 4
SAMPLE_LEN = 1024
PAST_LEN = 512
GRID_W = 64
PROMPT_ROWS = N_PROMPT_SEQ * PROMPT_LEN
ROWS = PROMPT_ROWS + N_SAMPLE_SEQ * SAMPLE_LEN
RB = 1024
N_GROUPS = 8
POOL_WINDOWS = (2, 4, 8, 16)
GW = 512
RWKV_W = 2048
HEAD = 64
N_HEADS = 32
GATE_LORA = 480
GATE_PAD = 512
DECAY_LORA = 128
ICLR_LORA = 128
ATT_W = 2048
HEAD_DIM = 128
N_KV = 4
KV_W = N_KV * HEAD_DIM
ROPE_THETA = 10000.0
ALPHA = (2 * DEPTH) ** 0.25
LN_EPS = 1e-5
GN_EPS = 64e-5
CHUNK = 64
SCAN_PAIRS = 16

C_R, C_K, C_V, C_Q = 0, 2048, 4096, 6144
C_AK, C_AV, C_G, C_W, C_A = 8192, 8704, 9216, 9728, 9984
ODD_COLS = 10240

VMEM_LIMIT = 56 * 1024 * 1024


def _cparams(sem):
    return pltpu.CompilerParams(dimension_semantics=sem, vmem_limit_bytes=VMEM_LIMIT)


def _group_of_block(i, rows_per_block):
    r0 = i * rows_per_block
    return jnp.where(r0 < PROMPT_ROWS, 0, 1 + (r0 - PROMPT_ROWS) // SAMPLE_LEN)


def _ada_kernel(c_ref, w_ref, b_ref, o_ref):
    c = c_ref[...]
    s = (c * jax.nn.sigmoid(c)).astype(BF16)
    o_ref[...] = jnp.dot(s, w_ref[...].astype(BF16), preferred_element_type=F32) + b_ref[...]


def _ada(cond, w_ada, b_ada):
    tn = 512
    n = w_ada.shape[-1]
    return pl.pallas_call(
        _ada_kernel,
        grid=(DEPTH, n // tn),
        in_specs=[pl.BlockSpec((N_GROUPS, D), lambda l, j: (0, 0)),
                  pl.BlockSpec((None, D, tn), lambda l, j: (l, 0, j)),
                  pl.BlockSpec((None, 1, tn), lambda l, j: (l, 0, j))],
        out_specs=pl.BlockSpec((None, N_GROUPS, tn), lambda l, j: (l, 0, j)),
        out_shape=jax.ShapeDtypeStruct((DEPTH, N_GROUPS, n), F32),
        compiler_params=_cparams(("parallel", "parallel")),
        name="ada",
    )(cond, w_ada, b_ada)


def _mod_spec(tr, l, sub, kind):
    return pl.BlockSpec((None, None, None, 1, D),
                        lambda i: (l, sub * 3 + kind, _group_of_block(i, tr), 0, 0))


def _row_block_specs(x, tr, tn, rank):
    col = (lambda ij: ij[1]) if rank == 2 else (lambda ij: 0)
    if not isinstance(x, tuple):
        return [pl.BlockSpec((tr, tn), lambda *ij: (ij[0], col(ij)))], [x]
    nb = PROMPT_ROWS // tr
    return ([pl.BlockSpec((tr, tn), lambda *ij: (jnp.minimum(ij[0], nb - 1), col(ij))),
             pl.BlockSpec((tr, tn), lambda *ij: (jnp.maximum(ij[0] - nb, 0), col(ij)))], list(x))


def _pick_rows(x_refs, tr):
    if len(x_refs) == 1:
        return x_refs[0][...]
    return jnp.where(pl.program_id(0) * tr < PROMPT_ROWS, x_refs[0][...], x_refs[1][...])


def _modulate_kernel(*refs, n_x, tr):
    shift_ref, scale_ref, u_ref = refs[n_x:]
    x = _pick_rows(refs[:n_x], tr)
    u_ref[...] = (x * (1.0 + scale_ref[...]) + shift_ref[...]).astype(u_ref.dtype)


def _modulate(x, mods, l, sub):
    tr = 256
    x_specs, x_args = _row_block_specs(x, tr, D, 1)
    return pl.pallas_call(
        functools.partial(_modulate_kernel, n_x=len(x_args), tr=tr),
        grid=(ROWS // tr,),
        in_specs=x_specs + [_mod_spec(tr, l, sub, 0), _mod_spec(tr, l, sub, 1)],
        out_specs=pl.BlockSpec((tr, D), lambda i: (i, 0)),
        out_shape=jax.ShapeDtypeStruct((ROWS, D), BF16),
        compiler_params=_cparams(("parallel",)),
        name="modulate",
    )(*x_args, mods, mods)


def _ln_kernel(z_ref, g_ref, b_ref, *rest, emit_u):
    z = z_ref[...]
    mu = jnp.mean(z, axis=-1, keepdims=True)
    zc = z - mu
    var = jnp.mean(zc * zc, axis=-1, keepdims=True)
    xn = zc * lax.rsqrt(var + LN_EPS) * g_ref[...] + b_ref[...]
    if emit_u:
        shift_ref, scale_ref, xo_ref, u_ref = rest
        xo_ref[...] = xn
        u_ref[...] = (xn * (1.0 + scale_ref[...]) + shift_ref[...]).astype(u_ref.dtype)
    else:
        (xo_ref,) = rest
        xo_ref[...] = xn


def _ln(z, mods, ln_g, ln_b, l, sub, nxt):
    tr = 256
    row = pl.BlockSpec((tr, D), lambda i: (i, 0))
    vec = pl.BlockSpec((None, None, 1, D), lambda i: (l, sub, 0, 0))
    in_specs = [row, vec, vec]
    args = [z, ln_g, ln_b]
    emit_u = nxt is not None
    if emit_u:
        in_specs += [_mod_spec(tr, nxt[0], nxt[1], 0), _mod_spec(tr, nxt[0], nxt[1], 1)]
        args += [mods, mods]
        out_specs = [row, row]
        out_shape = [jax.ShapeDtypeStruct((ROWS, D), F32), jax.ShapeDtypeStruct((ROWS, D), BF16)]
    else:
        out_specs = row
        out_shape = jax.ShapeDtypeStruct((ROWS, D), F32)
    return pl.pallas_call(
        functools.partial(_ln_kernel, emit_u=emit_u),
        grid=(ROWS // tr,),
        in_specs=in_specs, out_specs=out_specs, out_shape=out_shape,
        compiler_params=_cparams(("parallel",)),
        name="ln",
    )(*args)


def _mm_kernel(*refs, n_a, n_x, tm, y_scale):
    o_ref = refs[-1]
    acc = jnp.dot(refs[0][...], refs[n_a][...], preferred_element_type=F32)
    for r in range(1, n_a):
        acc += jnp.dot(refs[r][...], refs[n_a + r][...], preferred_element_type=F32)
    if n_x:
        gate_ref = refs[2 * n_a + n_x]
        acc = ALPHA * _pick_rows(refs[2 * n_a:2 * n_a + n_x], tm) + (y_scale * gate_ref[...]) * acc
    o_ref[...] = acc.astype(o_ref.dtype)


def _mm(a_parts, w, tm, tn, out_dtype, name, widx=(), resid=None):
    m, k = a_parts[0].shape
    assert all(a.shape == (m, k) for a in a_parts)
    n = w.shape[-1]
    lead = (None,) * len(widx)
    in_specs = [pl.BlockSpec((tm, k), lambda i, j: (i, 0)) for _ in a_parts]
    in_specs += [pl.BlockSpec(lead + (k, tn), lambda i, j, r=r: widx + (r, j)) for r in range(len(a_parts))]
    args = list(a_parts) + [w] * len(a_parts)
    n_x, y_scale = 0, 1.0
    if resid is not None:
        x, mods, l, sub, y_scale = resid
        x_specs, x_args = _row_block_specs(x, tm, tn, 2)
        n_x = len(x_args)
        in_specs += x_specs + [pl.BlockSpec((None, None, None, 1, tn),
                                            lambda i, j: (l, sub * 3 + 2, _group_of_block(i, tm), 0, j))]
        args += x_args + [mods]
    return pl.pallas_call(
        functools.partial(_mm_kernel, n_a=len(a_parts), n_x=n_x, tm=tm, y_scale=y_scale),
        grid=(m // tm, n // tn),
        in_specs=in_specs,
        out_specs=pl.BlockSpec((tm, tn), lambda i, j: (i, j)),
        out_shape=jax.ShapeDtypeStruct((m, n), out_dtype),
        compiler_params=_cparams(("parallel", "parallel")),
        name=name,
    )(*args)


def _swiglu_kernel(a_ref, wg_ref, wu_ref, o_ref, wgb_ref, wub_ref):
    @pl.when(pl.program_id(1) == 0)
    def _():
        wgb_ref[...] = wg_ref[...].astype(BF16)
        wub_ref[...] = wu_ref[...].astype(BF16)

    a = a_ref[...]
    g = jnp.dot(a, wgb_ref[...], preferred_element_type=F32)
    u = jnp.dot(a, wub_ref[...], preferred_element_type=F32)
    o_ref[...] = (g * jax.nn.sigmoid(g) * u).astype(o_ref.dtype)


def _swiglu_in(u, w_in, l, j):
    tm, tn = 1024, 256
    nb = D_FF // tn
    return pl.pallas_call(
        _swiglu_kernel,
        grid=(nb, ROWS // tm),
        in_specs=[pl.BlockSpec((tm, D), lambda n, m: (m, 0)),
                  pl.BlockSpec((None, None, D, tn), lambda n, m: (l, j, 0, n)),
                  pl.BlockSpec((None, None, D, tn), lambda n, m: (l, j, 0, nb + n))],
        out_specs=pl.BlockSpec((tm, tn), lambda n, m: (m, n)),
        out_shape=jax.ShapeDtypeStruct((ROWS, D_FF), BF16),
        scratch_shapes=[pltpu.VMEM((D, tn), BF16), pltpu.VMEM((D, tn), BF16)],
        compiler_params=_cparams(("arbitrary", "arbitrary")),
        name="swiglu_in",
    )(u, w_in, w_in)


def _even_tables():
    t = np.arange(RB)
    band = np.zeros((2, 4, RB, RB), np.float32)
    icnt = np.zeros((2, 4, RB, 1), np.float32)
    cn = np.zeros((2, RB, RB), np.float64)
    sn = np.zeros((2, RB, RB), np.float64)
    for kind, n in enumerate((PROMPT_LEN, SAMPLE_LEN)):
        seq = t // n
        pos = t % n
        for gi, w in enumerate(POOL_WINDOWS):
            lo = np.clip(pos - w // 2, 0, n)
            hi = np.clip(pos + w // 2, 0, n)
            same = seq[:, None] == seq[None, :]
            band[kind, gi] = (same & (pos[None, :] >= lo[:, None]) & (pos[None, :] < hi[:, None]))
            icnt[kind, gi, :, 0] = 1.0 / (hi - lo)
        same = (seq[:, None] == seq[None, :]).astype(np.float64)
        ang = 2.0 * np.pi * ((pos[:, None] * pos[None, :]) % n) / n
        scale = (n * GW) ** -0.5
        cn[kind] = np.cos(ang) * same * scale
        sn[kind] = np.sin(ang) * same * scale
    c = np.arange(GW)
    angc = 2.0 * np.pi * ((c[:, None] * c[None, :]) % GW) / GW
    return (band, icnt, cn.astype(np.float32), sn.astype(np.float32),
            np.cos(angc).astype(np.float32), np.sin(angc).astype(np.float32))


def _even_mid_kernel(p_ref, band_ref, icnt_ref, pw_ref, ps_ref, cn_ref, sn_ref, cc_ref, sc_ref, o_ref):
    j = pl.program_id(1)

    @pl.when(j < 4)
    def _():
        x = p_ref[...]
        xh = x.astype(BF16)
        xl = (x - xh.astype(F32)).astype(BF16)
        band = band_ref[...]
        wsum = (jnp.dot(band, xh, preferred_element_type=F32)
                + jnp.dot(band, xl, preferred_element_type=F32))
        pooled = wsum * icnt_ref[...] - x
        y = jnp.dot(pooled.astype(BF16), pw_ref[...], preferred_element_type=F32)
        o_ref[...] = (y * ps_ref[...]).astype(o_ref.dtype)

    @pl.when(j >= 4)
    def _():
        x = p_ref[...].astype(BF16)
        xc = jnp.dot(x, cc_ref[...], preferred_element_type=F32).astype(BF16)
        xs = jnp.dot(x, sc_ref[...], preferred_element_type=F32).astype(BF16)
        y = (jnp.dot(cn_ref[...], xc, preferred_element_type=F32)
             - jnp.dot(sn_ref[...], xs, preferred_element_type=F32))
        o_ref[...] = y.astype(o_ref.dtype)


def _even_mid(p, pool_w, pool_scale):
    band, icnt, cn, sn, cc, sc = _even_tables()
    band = jnp.asarray(band).astype(BF16)
    cn, sn, cc, sc = (jnp.asarray(t).astype(BF16) for t in (cn, sn, cc, sc))
    kind = lambda i: jnp.where(i * RB < PROMPT_ROWS, 0, 1)
    grp = lambda j: jnp.minimum(j, 3)
    return pl.pallas_call(
        _even_mid_kernel,
        grid=(ROWS // RB, 8),
        in_specs=[pl.BlockSpec((RB, GW), lambda i, j: (i, j)),
                  pl.BlockSpec((None, None, RB, RB), lambda i, j: (kind(i), grp(j), 0, 0)),
                  pl.BlockSpec((None, None, RB, 1), lambda i, j: (kind(i), grp(j), 0, 0)),
                  pl.BlockSpec((None, GW, GW), lambda i, j: (grp(j), 0, 0)),
                  pl.BlockSpec((None, 1, GW), lambda i, j: (grp(j), 0, 0)),
                  pl.BlockSpec((None, RB, RB), lambda i, j: (kind(i), 0, 0)),
                  pl.BlockSpec((None, RB, RB), lambda i, j: (kind(i), 0, 0)),
                  pl.BlockSpec((GW, GW), lambda i, j: (0, 0)),
                  pl.BlockSpec((GW, GW), lambda i, j: (0, 0))],
        out_specs=pl.BlockSpec((RB, GW), lambda i, j: (i, j)),
        out_shape=jax.ShapeDtypeStruct((ROWS, D), BF16),
        compiler_params=_cparams(("parallel", "arbitrary")),
        name="even_mid",
    )(p, band, jnp.asarray(icnt), pool_w, pool_scale, cn, sn, cc, sc)


def _head_sum_matrix(width, value):
    h = np.arange(width) // HEAD
    return jnp.asarray((h[:, None] == h[None, :]).astype(np.float32) * value).astype(BF16)


def _split_bf16(x, n):
    parts = []
    for _ in range(n - 1):
        h = x.astype(BF16)
        parts.append(h)
        x = x - h.astype(F32)
    parts.append(x.astype(BF16))
    return parts


def _head_sum(x, e):
    hi, lo = _split_bf16(x, 2)
    return jnp.dot(hi, e, preferred_element_type=F32) + jnp.dot(lo, e, preferred_element_type=F32)


def _token_shift(x, mu, seq_len):
    n = x.shape[0]
    pos = lax.broadcasted_iota(jnp.int32, x.shape, 0) % seq_len
    prev = jnp.where(pos == 0, 0.0, pltpu.roll(x, 1, 0))
    nxt = jnp.where(pos == seq_len - 1, 0.0, pltpu.roll(x, n - 1, 0))
    return x + (0.5 * (prev + nxt) - x) * mu


def _rwkv_prep_kernel(pr_ref, pk_ref, pv_ref, pg_ref, pw_ref, pa_ref,
                      mr_ref, mk_ref, mv_ref, mg_ref, mw_ref, ma_ref,
                      w0_ref, w2_ref, a0_ref, aw2_ref, gw2_ref, kk_ref, ka_ref, e_ref,
                      r_o, k_o, v_o, kk_o, kka_o, lw_o, g_o):
    i = pl.program_id(0)

    def body(seq_len):
        r = _token_shift(pr_ref[...], mr_ref[...], seq_len)
        k = _token_shift(pk_ref[...], mk_ref[...], seq_len)
        v = _token_shift(pv_ref[...], mv_ref[...], seq_len)
        g_lo = _token_shift(pg_ref[...], mg_ref[...], seq_len)
        w_lo = _token_shift(pw_ref[...], mw_ref[...], seq_len)
        a_lo = _token_shift(pa_ref[...], ma_ref[...], seq_len)
        tw = jnp.tanh(w_lo).astype(BF16)
        for d in range(2):
            x = w0_ref[d] + jnp.dot(tw[:, d * DECAY_LORA:(d + 1) * DECAY_LORA], w2_ref[d],
                                    preferred_element_type=F32)
            z = -x
            softplus = jnp.maximum(z, 0.0) + jnp.log(1.0 + jnp.exp(-jnp.abs(z)))
            lw_o[d] = -jnp.exp(-softplus - 0.5)
        a = jax.nn.sigmoid(a0_ref[...] + jnp.dot(a_lo.astype(BF16), aw2_ref[...], preferred_element_type=F32))
        g_o[...] = jnp.dot(jax.nn.sigmoid(g_lo).astype(BF16), gw2_ref[...], preferred_element_type=F32)
        kk = k * kk_ref[...]
        ss = _head_sum(kk * kk, e_ref[...])
        kk = kk * lax.rsqrt(ss + 1e-12)
        r_o[...] = r
        k_o[...] = k * (1.0 + (a - 1.0) * ka_ref[...])
        v_o[...] = v
        kk_o[...] = kk
        kka_o[...] = kk * a

    @pl.when(i * RB < PROMPT_ROWS)
    def _():
        body(PROMPT_LEN)

    @pl.when(i * RB >= PROMPT_ROWS)
    def _():
        body(SAMPLE_LEN)


def _rwkv_prep(p, mu, w0, w2, a0, aw2, gw2, k_k, k_a):
    cw = 256
    ncb = RWKV_W // cw
    e = _head_sum_matrix(cw, 1.0)
    pcol = lambda base, width: pl.BlockSpec((RB, width), lambda i, c: (i, base // width))
    pch = lambda base: pl.BlockSpec((RB, cw), lambda i, c: (i, base // cw + c))
    mcol = lambda base, width: pl.BlockSpec((1, width), lambda i, c: (0, base // width))
    mch = lambda base: pl.BlockSpec((1, cw), lambda i, c: (0, base // cw + c))
    vch = pl.BlockSpec((1, cw), lambda i, c: (0, c))
    out = pl.BlockSpec((RB, cw), lambda i, c: (i, c))
    sds = jax.ShapeDtypeStruct((ROWS, RWKV_W), F32)
    return pl.pallas_call(
        _rwkv_prep_kernel,
        grid=(ROWS // RB, ncb),
        in_specs=[pch(C_R), pch(C_K), pch(C_V), pcol(C_G, GATE_PAD), pcol(C_W, 2 * DECAY_LORA), pcol(C_A, ICLR_LORA),
                  mch(C_R), mch(C_K), mch(C_V), mcol(C_G, GATE_PAD), mcol(C_W, 2 * DECAY_LORA), mcol(C_A, ICLR_LORA),
                  pl.BlockSpec((2, 1, cw), lambda i, c: (0, 0, c)),
                  pl.BlockSpec((2, DECAY_LORA, cw), lambda i, c: (0, 0, c)),
                  vch,
                  pl.BlockSpec((ICLR_LORA, cw), lambda i, c: (0, c)),
                  pl.BlockSpec((GATE_PAD, cw), lambda i, c: (0, c)),
                  vch, vch,
                  pl.BlockSpec((cw, cw), lambda i, c: (0, 0))],
        out_specs=[out, out, out, out, out,
                   pl.BlockSpec((2, RB, cw), lambda i, c: (0, i, c)), out],
        out_shape=[sds, sds, sds, sds, sds, jax.ShapeDtypeStruct((2, ROWS, RWKV_W), F32), sds],
        compiler_params=_cparams(("parallel", "arbitrary")),
        name="rwkv_prep",
    )(p, p, p, p, p, p, mu, mu, mu, mu, mu, mu, w0, w2, a0, aw2, gw2, k_k, k_a, e)


SCAN_PASSES = {"scores": 1, "ws": 1, "mv": 1, "inv": 1, "u": 1, "y": 1, "state": 1}


def _mmp(a, b, passes, nt=False):
    dims = (((1,), (1 if nt else 0,)), ((), ()))
    dot = lambda x, y: lax.dot_general(x, y, dims, preferred_element_type=F32)
    if passes == 1:
        return dot(a.astype(BF16), b.astype(BF16))
    ah, al = _split_bf16(a, 2)
    bh, bl = _split_bf16(b, 2)
    return dot(ah, bh) + (dot(al, bh) + dot(ah, bl))


def _rwkv_scan_kernel(*refs, reverse, has_init, pairs):
    lw_ref, r_ref, k_ref, v_ref, a_ref, b_ref = refs[:6]
    s0_ref = refs[6] if has_init else None
    y_ref, s_ref, sp_ref = refs[-3:]
    c_idx = pl.program_id(2)
    n_chunks = pl.num_programs(2)
    L = CHUNK
    W2 = 2 * HEAD
    P = SCAN_PASSES

    @pl.when(c_idx == 0)
    def _():
        sp_ref[...] = jnp.zeros_like(sp_ref)
        if has_init:
            for p in range(pairs):
                sp_ref[p, 0:HEAD, 0:HEAD] = s0_ref[2 * p]
                sp_ref[p, HEAD:W2, HEAD:W2] = s0_ref[2 * p + 1]

    row = lax.broadcasted_iota(jnp.int32, (L, W2), 0)
    lane = lax.broadcasted_iota(jnp.int32, (L, W2), 1)
    col = lane % HEAD
    first = lane < HEAD
    incl = (col >= row) if reverse else (col <= row)
    strict = (col > row) if reverse else (col < row)
    eye = (col == row).astype(F32)
    blk8 = (row // 8) == (col // 8)
    off = [((row // (2 * b)) == (col // (2 * b))) & ((row // b) != (col // b)) for b in (8, 16, 32)]
    row_s = lax.broadcasted_iota(jnp.int32, (W2, W2), 0)
    lane_s = lax.broadcasted_iota(jnp.int32, (W2, W2), 1)
    diag_blocks = (row_s < HEAD) == (lane_s < HEAD)
    trow = lax.broadcasted_iota(jnp.int32, (L, L), 0)
    tcol = lax.broadcasted_iota(jnp.int32, (L, L), 1)
    tri = ((tcol >= trow) if reverse else (tcol <= trow)).astype(BF16)

    def bd(x):
        return jnp.concatenate([jnp.where(first, x, 0.0), jnp.where(first, 0.0, x)], axis=0)

    lw = lw_ref[...]
    c = sum(jnp.dot(tri, t, preferred_element_type=F32) for t in _split_bf16(lw, 3))
    g_in = jnp.exp(c)
    g_inv = jnp.exp(-c)
    at_all = a_ref[...] * jnp.exp(c - lw)
    rt_all = r_ref[...] * g_in
    bt_all = b_ref[...] * g_inv
    kt_all = k_ref[...] * g_inv
    v_all = v_ref[...]
    ps = [slice(p * W2, (p + 1) * W2) for p in range(pairs)]
    each = lambda f, *xs: [f(*z) for z in zip(*xs)]

    ar = [jnp.concatenate([at_all[:, s], rt_all[:, s]], axis=0) for s in ps]
    bt = [bt_all[:, s] for s in ps]
    kt = [kt_all[:, s] for s in ps]
    v = [v_all[:, s] for s in ps]
    s0 = [sp_ref[p] for p in range(pairs)]
    npq = each(lambda x, y: _mmp(x, bd(y), P["scores"], nt=True), ar, bt)
    mq = each(lambda x, y: _mmp(x, bd(y), P["scores"], nt=True), ar, kt)
    ws = each(lambda x, y: _mmp(x, y, P["ws"], nt=True), ar, s0)
    n_mat = [jnp.where(strict, x[:L], 0.0) for x in npq]
    m_mat = [jnp.where(strict, x[:L], 0.0) for x in mq]
    p_mat = [jnp.where(incl, x[L:], 0.0) for x in npq]
    q_mat = [jnp.where(incl, x[L:], 0.0) for x in mq]
    w = each(lambda x, m, y: x[:L] + _mmp(m, bd(y), P["mv"]), ws, m_mat, v)
    n8 = [jnp.where(blk8, x, 0.0) for x in n_mat]
    n8_2 = each(lambda x: _mmp(x, bd(x), P["inv"]), n8)
    n8_4 = each(lambda x: _mmp(x, bd(x), P["inv"]), n8_2)
    t = each(lambda x, y: _mmp(eye - x, bd(eye + y), P["inv"]), n8, n8_2)
    t = each(lambda x, y: _mmp(x, bd(eye + y), P["inv"]), t, n8_4)
    for m in off:
        x = each(lambda n, y: _mmp(jnp.where(m, n, 0.0), bd(y), P["inv"]), n_mat, t)
        t = each(lambda y, z: y - _mmp(y, bd(z), P["inv"]), t, x)
    u = each(lambda x, y: -_mmp(x, bd(y), P["u"]), t, w)
    y = each(lambda x, pm, qm, uu, vv: x[L:] + _mmp(jnp.concatenate([pm, qm], axis=1),
                                                     jnp.concatenate([bd(uu), bd(vv)], axis=0), P["y"]),
             ws, p_mat, q_mat, u, v)
    for p in range(pairs):
        y_ref[:, ps[p]] = y[p]
    g_last = g_in[0:1, :] if reverse else g_in[L - 1:L, :]
    upd = each(lambda uu, vv, b, k: _mmp(jnp.concatenate([uu, vv], axis=0).T,
                                         jnp.concatenate([b, k], axis=0), P["state"]), u, v, bt, kt)
    s_new = [(s0[p] + jnp.where(diag_blocks, upd[p], 0.0)) * g_last[:, ps[p]] for p in range(pairs)]
    for p in range(pairs):
        sp_ref[p] = s_new[p]

    @pl.when(c_idx == n_chunks - 1)
    def _():
        for p in range(pairs):
            s_ref[2 * p] = s_new[p][0:HEAD, 0:HEAD]
            s_ref[2 * p + 1] = s_new[p][HEAD:W2, HEAD:W2]


def _rwkv_scan(lw, r, k, v, a, b, s0, y_prev, *, direction, n_seq, seq_len, row0):
    pairs = SCAN_PAIRS
    heads = 2 * pairs
    hb = N_HEADS // heads
    nc = seq_len // CHUNK
    cw = heads * HEAD
    reverse = direction == 1

    def rows(s, h, c):
        ce = nc - 1 - c if reverse else c
        return row0 // CHUNK + s * nc + ce

    blk = pl.BlockSpec((CHUNK, cw), lambda s, h, c: (rows(s, h, c), h))
    in_specs = [pl.BlockSpec((None, CHUNK, cw), lambda s, h, c: (direction, rows(s, h, c), h)),
                blk, blk, blk, blk, blk]
    args = [lw, r, k, v, a, b]
    if s0 is not None:
        in_specs.append(pl.BlockSpec((None, None, heads, HEAD, HEAD), lambda s, h, c: (s, direction, h, 0, 0)))
        args.append(s0)
    aliases = {}
    if y_prev is not None:
        aliases = {len(args): 0}
        in_specs.append(pl.BlockSpec(memory_space=pl.ANY))
        args.append(y_prev)
    y, s = pl.pallas_call(
        functools.partial(_rwkv_scan_kernel, reverse=reverse, has_init=s0 is not None, pairs=pairs),
        grid=(n_seq, hb, nc),
        in_specs=in_specs,
        out_specs=[pl.BlockSpec((CHUNK, cw), lambda s, h, c: (rows(s, h, c), h)),
                   pl.BlockSpec((None, heads, HEAD, HEAD), lambda s, h, c: (s, h, 0, 0))],
        out_shape=[jax.ShapeDtypeStruct((ROWS, RWKV_W), F32),
                   jax.ShapeDtypeStruct((n_seq, N_HEADS, HEAD, HEAD), F32)],
        scratch_shapes=[pltpu.VMEM((pairs, 2 * HEAD, 2 * HEAD), F32)],
        input_output_aliases=aliases,
        compiler_params=_cparams(("parallel", "parallel", "arbitrary")),
        name="rwkv_scan",
    )(*args)
    return y, s


def _rwkv_post_kernel(yf_ref, yb_ref, r_ref, k_ref, v_ref, g_ref, gg_ref, gb_ref, rk_ref, e_ref, o_ref):
    y = yf_ref[...] + yb_ref[...]
    e = e_ref[...]
    mu = _head_sum(y, e)
    yc = y - mu
    var = _head_sum(yc * yc, e)
    yn = yc * lax.rsqrt(var + GN_EPS) * gg_ref[...] + gb_ref[...]
    bonus = _head_sum(r_ref[...] * k_ref[...] * rk_ref[...], e) * float(HEAD) * v_ref[...]
    o_ref[...] = ((yn + bonus) * g_ref[...]).astype(o_ref.dtype)


def _rwkv_post(yf, yb, r, k, v, g, gn_g, gn_b, r_k):
    tr, cw = 512, 512
    e = _head_sum_matrix(cw, 1.0 / HEAD)
    blk = pl.BlockSpec((tr, cw), lambda i, c: (i, c))
    vec = pl.BlockSpec((1, cw), lambda i, c: (0, c))
    return pl.pallas_call(
        _rwkv_post_kernel,
        grid=(ROWS // tr, RWKV_W // cw),
        in_specs=[blk, blk, blk, blk, blk, blk, vec, vec, vec, pl.BlockSpec((cw, cw), lambda i, c: (0, 0))],
        out_specs=blk,
        out_shape=jax.ShapeDtypeStruct((ROWS, RWKV_W), BF16),
        compiler_params=_cparams(("parallel", "parallel")),
        name="rwkv_post",
    )(yf, yb, r, k, v, g, gn_g, gn_b, r_k, e)


def _rms(x, gain):
    return x * lax.rsqrt(jnp.mean(x * x, axis=-1, keepdims=True) + 1e-6) * gain


def _rope(x, cos, sin_signed):
    lane = lax.broadcasted_iota(jnp.int32, x.shape, 1)
    quarter = HEAD_DIM // 4
    partner = jnp.where(lane % (2 * quarter) < quarter,
                        pltpu.roll(x, HEAD_DIM - quarter, 1), pltpu.roll(x, quarter, 1))
    return x * cos + partner * sin_signed


def _rope_tables():
    n = SAMPLE_LEN
    quarter = HEAD_DIM // 4
    row = (np.arange(n) // GRID_W).astype(np.float64)
    col = (np.arange(n) % GRID_W).astype(np.float64)
    inv_freq = ROPE_THETA ** (-np.arange(quarter, dtype=np.float64) / quarter)
    ar = row[:, None] * inv_freq[None, :]
    ac = col[:, None] * inv_freq[None, :]
    cos = np.concatenate([np.cos(ar), np.cos(ar), np.cos(ac), np.cos(ac)], axis=1)
    sin = np.concatenate([-np.sin(ar), np.sin(ar), -np.sin(ac), np.sin(ac)], axis=1)
    return cos.astype(np.float32), sin.astype(np.float32)


def _attn_prompt_kernel(q_ref, k_ref, v_ref, qn_ref, kn_ref, o_ref, ko_ref, vo_ref):
    scale = HEAD_DIM ** -0.5
    kn = _rms(k_ref[...], kn_ref[...])
    v = v_ref[...]
    ko_ref[...] = kn
    vo_ref[...] = v
    kb = kn.astype(BF16)
    vb = v.astype(BF16)
    for j in range(ATT_W // KV_W):
        hs = slice(j * HEAD_DIM, (j + 1) * HEAD_DIM)
        q = _rms(q_ref[:, hs], qn_ref[...]).astype(BF16)
        s = lax.dot_general(q, kb, (((1,), (1,)), ((), ())), preferred_element_type=F32) * scale
        e = jnp.exp(s - jnp.max(s, axis=-1, keepdims=True))
        pr = e / jnp.sum(e, axis=-1, keepdims=True)
        o_ref[:, hs] = jnp.dot(pr.astype(BF16), vb, preferred_element_type=F32).astype(o_ref.dtype)


def _attn_prompt(p, q_norm, k_norm):
    n = PROMPT_LEN
    qw = ATT_W // N_KV
    return pl.pallas_call(
        _attn_prompt_kernel,
        grid=(N_PROMPT_SEQ, N_KV),
        in_specs=[pl.BlockSpec((n, qw), lambda s, g: (s, C_Q // qw + g)),
                  pl.BlockSpec((n, HEAD_DIM), lambda s, g: (s, C_AK // HEAD_DIM + g)),
                  pl.BlockSpec((n, HEAD_DIM), lambda s, g: (s, C_AV // HEAD_DIM + g)),
                  pl.BlockSpec((1, HEAD_DIM), lambda s, g: (0, 0)),
                  pl.BlockSpec((1, HEAD_DIM), lambda s, g: (0, 0))],
        out_specs=[pl.BlockSpec((n, qw), lambda s, g: (s, g)),
                   pl.BlockSpec((n, HEAD_DIM), lambda s, g: (s, g)),
                   pl.BlockSpec((n, HEAD_DIM), lambda s, g: (s, g))],
        out_shape=[jax.ShapeDtypeStruct((ROWS, ATT_W), BF16),
                   jax.ShapeDtypeStruct((PROMPT_ROWS, KV_W), F32),
                   jax.ShapeDtypeStruct((PROMPT_ROWS, KV_W), F32)],
        compiler_params=_cparams(("parallel", "parallel")),
        name="attn_prompt",
    )(p, p, p, q_norm, k_norm)


def _attn_sample_kernel(q_ref, k_ref, v_ref, ck_ref, cv_ref, qn_ref, kn_ref, cos_ref, sin_ref, _o_prev, o_ref):
    scale = HEAD_DIM ** -0.5
    tq = 256
    cos = cos_ref[...]
    sin = sin_ref[...]
    ks = _rope(_rms(k_ref[...], kn_ref[...]), cos, sin).astype(BF16)
    vs = v_ref[...].astype(BF16)
    kc = ck_ref[...].astype(BF16)
    vc = cv_ref[...].astype(BF16)
    nt = (((1,), (1,)), ((), ()))
    for j in range(ATT_W // KV_W):
        hs = slice(j * HEAD_DIM, (j + 1) * HEAD_DIM)
        for qb in range(SAMPLE_LEN // tq):
            rs = slice(qb * tq, (qb + 1) * tq)
            q = _rope(_rms(q_ref[rs, hs], qn_ref[...]), cos[rs], sin[rs]).astype(BF16)
            s1 = lax.dot_general(q, kc, nt, preferred_element_type=F32) * scale
            s2 = lax.dot_general(q, ks, nt, preferred_element_type=F32) * scale
            m = jnp.maximum(jnp.max(s1, axis=-1, keepdims=True), jnp.max(s2, axis=-1, keepdims=True))
            e1 = jnp.exp(s1 - m)
            e2 = jnp.exp(s2 - m)
            inv = 1.0 / (jnp.sum(e1, axis=-1, keepdims=True) + jnp.sum(e2, axis=-1, keepdims=True))
            o = (jnp.dot((e1 * inv).astype(BF16), vc, preferred_element_type=F32)
                 + jnp.dot((e2 * inv).astype(BF16), vs, preferred_element_type=F32))
            o_ref[rs, hs] = o.astype(o_ref.dtype)


def _attn_sample(p, cache_k, cache_v, q_norm, k_norm, o_prev):
    n = SAMPLE_LEN
    qw = ATT_W // N_KV
    r0 = PROMPT_ROWS // n
    cos, sin = _rope_tables()
    return pl.pallas_call(
        _attn_sample_kernel,
        grid=(N_SAMPLE_SEQ, N_KV),
        in_specs=[pl.BlockSpec((n, qw), lambda s, g: (r0 + s, C_Q // qw + g)),
                  pl.BlockSpec((n, HEAD_DIM), lambda s, g: (r0 + s, C_AK // HEAD_DIM + g)),
                  pl.BlockSpec((n, HEAD_DIM), lambda s, g: (r0 + s, C_AV // HEAD_DIM + g)),
                  pl.BlockSpec((PAST_LEN, HEAD_DIM), lambda s, g: (s, g)),
                  pl.BlockSpec((PAST_LEN, HEAD_DIM), lambda s, g: (s, g)),
                  pl.BlockSpec((1, HEAD_DIM), lambda s, g: (0, 0)),
                  pl.BlockSpec((1, HEAD_DIM), lambda s, g: (0, 0)),
                  pl.BlockSpec((n, HEAD_DIM), lambda s, g: (0, 0)),
                  pl.BlockSpec((n, HEAD_DIM), lambda s, g: (0, 0)),
                  pl.BlockSpec(memory_space=pl.ANY)],
        out_specs=pl.BlockSpec((n, qw), lambda s, g: (r0 + s, g)),
        out_shape=jax.ShapeDtypeStruct((ROWS, ATT_W), BF16),
        input_output_aliases={9: 0},
        compiler_params=_cparams(("parallel", "parallel")),
        name="attn_sample",
    )(p, p, p, cache_k, cache_v, q_norm, k_norm, jnp.asarray(cos), jnp.asarray(sin), o_prev)


_ODD_SEGMENTS = ((0, C_R, 6144), (7008, C_Q, ATT_W), (9056, C_AK, KV_W), (9568, C_AV, KV_W),
                 (6144, C_G, GATE_LORA), (6624, C_W, 2 * DECAY_LORA), (6880, C_A, ICLR_LORA))
_ODD_ZERO = ((C_G + GATE_LORA, GATE_PAD - GATE_LORA), (C_A + ICLR_LORA, ODD_COLS - C_A - ICLR_LORA))


def _pack_odd_kernel(w_ref, o_ref):
    for src, dst, width in _ODD_SEGMENTS:
        o_ref[:, dst:dst + width] = w_ref[:, src:src + width].astype(o_ref.dtype)
    for dst, width in _ODD_ZERO:
        o_ref[:, dst:dst + width] = jnp.zeros((o_ref.shape[0], width), o_ref.dtype)


def _pack_odd_w_in(w):
    tr = 256
    k, n = w.shape
    return pl.pallas_call(
        _pack_odd_kernel,
        grid=(k // tr,),
        in_specs=[pl.BlockSpec((tr, n), lambda i: (i, 0))],
        out_specs=pl.BlockSpec((tr, ODD_COLS), lambda i: (i, 0)),
        out_shape=jax.ShapeDtypeStruct((k, ODD_COLS), BF16),
        compiler_params=_cparams(("parallel",)),
        name="pack_odd_w_in",
    )(w)


def _pack_shift_mu(mu):
    z = lambda n: jnp.zeros((n,), mu.dtype)
    out = jnp.concatenate([mu[0:6144], z(C_G - C_Q), mu[6144:6144 + GATE_LORA], z(GATE_PAD - GATE_LORA),
                           mu[6624:6880], mu[6880:7008], z(ODD_COLS - C_A - ICLR_LORA)])
    return out[None, :]


def kernel(x_prompt, x_sample, cache_k, cache_v, state_rwkv, c, c_ctx, w_ada, b_ada, ln_g, ln_b, ffn_w_in,
           ffn_w_out, even_w_in, pool_w, pool_scale, even_w_out, odd_w_in, shift_mu, decay_w0, decay_w2,
           iclr_a0, iclr_w2, gate_w2, k_k, k_a, r_k, gn_g, gn_b, q_norm, k_norm, odd_w_out):
    x = (x_prompt.reshape(PROMPT_ROWS, D), x_sample.reshape(ROWS - PROMPT_ROWS, D))
    cond =jnp.concatenate([c_ctx[None, :], c, jnp.zeros((N_GROUPS - 1 - c.shape[0], D), F32)], axis=0)
    mods = _ada(cond, w_ada, b_ada[:, None, :])
    mods = mods.reshape(DEPTH, N_GROUPS, 3 * N_SUB, D).transpose(0, 2, 1, 3)[:, :, :, None, :]
    ln_g4 = ln_g[:, :, None, :]
    ln_b4 = ln_b[:, :, None, :]
    ffn_w_out_b = ffn_w_out.astype(BF16)

    u = _modulate(x, mods, 0, 0)
    new_k = new_v = new_s = None
    for l in range(DEPTH):
        i = l // 2
        h = _swiglu_in(u, ffn_w_in, l, 0)
        z = _mm([h], ffn_w_out_b, 512, 256, F32, "ffn_out", widx=(l, 0), resid=(x, mods, l, 0, 0.5))
        x, u = _ln(z, mods, ln_g4, ln_b4, l, 0, (l, 1))
        if l % 2 == 0:
            p = _mm([u], even_w_in[i].astype(BF16), 1024, 512, F32, "even_in")
            mid = _even_mid(p, pool_w[i].astype(BF16), pool_scale[i].reshape(4, 1, GW))
            z = _mm([mid], even_w_out[i].astype(BF16), 1024, 512, F32, "even_out", resid=(x, mods, l, 1, 1.0))
        else:
            p = _mm([u], _pack_odd_w_in(odd_w_in[i]), 1024, 512, F32, "odd_in")
            gw2 = jnp.concatenate([gate_w2[i], jnp.zeros((GATE_PAD - GATE_LORA, RWKV_W), F32)], axis=0)
            r, k, v, kk, kka, lw, g = _rwkv_prep(
                p, _pack_shift_mu(shift_mu[i]), decay_w0[i][:, None, :], decay_w2[i].astype(BF16),
                iclr_a0[i][None, :], iclr_w2[i].astype(BF16), gw2.astype(BF16), k_k[i][None, :], k_a[i][None, :])
            ys, ss = [], []
            for d in range(2):
                yd, sp = _rwkv_scan(lw, r, k, v, kk, kka, None, None, direction=d,
                                    n_seq=N_PROMPT_SEQ, seq_len=PROMPT_LEN, row0=0)
                yd, _ = _rwkv_scan(lw, r, k, v, kk, kka, state_rwkv[:, i], yd, direction=d,
                                   n_seq=N_SAMPLE_SEQ, seq_len=SAMPLE_LEN, row0=PROMPT_ROWS)
                ys.append(yd)
                ss.append(sp)
            y_rwkv = _rwkv_post(ys[0], ys[1], r, k, v, g, gn_g[i][None, :], gn_b[i][None, :],
                                r_k[i].reshape(1, RWKV_W))
            o_p, nk, nv = _attn_prompt(p, q_norm[i][None, :], k_norm[i][None, :])
            o = _attn_sample(p, cache_k[:, i].reshape(N_SAMPLE_SEQ * PAST_LEN, KV_W),
                             cache_v[:, i].reshape(N_SAMPLE_SEQ * PAST_LEN, KV_W),
                             q_norm[i][None, :], k_norm[i][None, :], o_p)
            z = _mm([y_rwkv, o], odd_w_out[i].astype(BF16), 1024, 512, F32, "odd_out", resid=(x, mods, l, 1, 1.0))
            new_k = nk.reshape(N_PROMPT_SEQ, 1, PROMPT_LEN, N_KV, HEAD_DIM)
            new_v = nv.reshape(N_PROMPT_SEQ, 1, PROMPT_LEN, N_KV, HEAD_DIM)
            new_s = jnp.stack(ss, axis=1)[:, None]
        x, u = _ln(z, mods, ln_g4, ln_b4, l, 1, (l, 2))
        h = _swiglu_in(u, ffn_w_in, l, 1)
        z = _mm([h], ffn_w_out_b, 512, 256, F32, "ffn_out", widx=(l, 1), resid=(x, mods, l, 2, 0.5))
        if l + 1 < DEPTH:
            x, u = _ln(z, mods, ln_g4, ln_b4, l, 2, (l + 1, 0))
        else:
            x = _ln(z, mods, ln_g4, ln_b4, l, 2, None)

    y_prompt = x[:PROMPT_ROWS].reshape(N_PROMPT_SEQ, PROMPT_LEN, D)
    y_sample = x[PROMPT_ROWS:].reshape(N_SAMPLE_SEQ, SAMPLE_LEN, D)
    return (y_prompt, y_sample, new_k, new_v, new_s)
```

```python
import functools

import numpy as np
import jax
import jax.numpy as jnp
from jax import lax
from jax.experimental import pallas as pl
from jax.experimental.pallas import tpu as pltpu

F32 = jnp.float32
BF16 = jnp.bfloat16

D = 4096
DEPTH = 2
N_SUB = 3
D_FF = 11008
N_PROMPT_SEQ = 16
PROMPT_LEN = 256
N_SAMPLE_SEQ = 4
SAMPLE_LEN = 1024
PAST_LEN = 512
GRID_W = 64
PROMPT_ROWS = N_PROMPT_SEQ * PROMPT_LEN
ROWS = PROMPT_ROWS + N_SAMPLE_SEQ * SAMPLE_LEN
RB = 1024
N_GROUPS = 8
POOL_WINDOWS = (2, 4, 8, 16)
GW = 512
RWKV_W = 2048
HEAD = 64
N_HEADS = 32
GATE_LORA = 480
GATE_PAD = 512
DECAY_LORA = 128
ICLR_LORA = 128
ATT_W = 2048
HEAD_DIM = 128
N_KV = 4
KV_W = N_KV * HEAD_DIM
ROPE_THETA = 10000.0
ALPHA = (2 * DEPTH) ** 0.25
LN_EPS = 1e-5
GN_EPS = 64e-5
CHUNK = 64

C_R, C_K, C_V, C_Q = 0, 2048, 4096, 6144
C_AK, C_AV, C_G, C_W, C_A = 8192, 8704, 9216, 9728, 9984
ODD_COLS = 10240

VMEM_LIMIT = 56 * 1024 * 1024


def _cparams(sem):
    return pltpu.CompilerParams(dimension_semantics=sem, vmem_limit_bytes=VMEM_LIMIT)


def _group_of_block(i, rows_per_block):
    r0 = i * rows_per_block
    return jnp.where(r0 < PROMPT_ROWS, 0, 1 + (r0 - PROMPT_ROWS) // SAMPLE_LEN)


def _ada_kernel(c_ref, w_ref, b_ref, o_ref):
    c = c_ref[...]
    s = (c * jax.nn.sigmoid(c)).astype(BF16)
    o_ref[...] = jnp.dot(s, w_ref[...].astype(BF16), preferred_element_type=F32) + b_ref[...]


def _ada(cond, w_ada, b_ada):
    tn = 512
    n = w_ada.shape[-1]
    return pl.pallas_call(
        _ada_kernel,
        grid=(DEPTH, n // tn),
        in_specs=[pl.BlockSpec((N_GROUPS, D), lambda l, j: (0, 0)),
                  pl.BlockSpec((None, D, tn), lambda l, j: (l, 0, j)),
                  pl.BlockSpec((None, 1, tn), lambda l, j: (l, 0, j))],
        out_specs=pl.BlockSpec((None, N_GROUPS, tn), lambda l, j: (l, 0, j)),
        out_shape=jax.ShapeDtypeStruct((DEPTH, N_GROUPS, n), F32),
        compiler_params=_cparams(("parallel", "parallel")),
        name="ada",
    )(cond, w_ada, b_ada)


def _mod_spec(tr, l, sub, kind):
    return pl.BlockSpec((None, None, None, 1, D),
                        lambda i: (l, sub * 3 + kind, _group_of_block(i, tr), 0, 0))


def _row_block_specs(x, tr, tn, rank):
    col = (lambda ij: ij[1]) if rank == 2 else (lambda ij: 0)
    if not isinstance(x, tuple):
        return [pl.BlockSpec((tr, tn), lambda *ij: (ij[0], col(ij)))], [x]
    nb = PROMPT_ROWS // tr
    return ([pl.BlockSpec((tr, tn), lambda *ij: (jnp.minimum(ij[0], nb - 1), col(ij))),
             pl.BlockSpec((tr, tn), lambda *ij: (jnp.maximum(ij[0] - nb, 0), col(ij)))], list(x))


def _pick_rows(x_refs, tr):
    if len(x_refs) == 1:
        return x_refs[0][...]
    return jnp.where(pl.program_id(0) * tr < PROMPT_ROWS, x_refs[0][...], x_refs[1][...])


def _modulate_kernel(*refs, n_x, tr):
    shift_ref, scale_ref, u_ref = refs[n_x:]
    x = _pick_rows(refs[:n_x], tr)
    u_ref[...] = (x * (1.0 + scale_ref[...]) + shift_ref[...]).astype(u_ref.dtype)


def _modulate(x, mods, l, sub):
    tr = 256
    x_specs, x_args = _row_block_specs(x, tr, D, 1)
    return pl.pallas_call(
        functools.partial(_modulate_kernel, n_x=len(x_args), tr=tr),
        grid=(ROWS // tr,),
        in_specs=x_specs + [_mod_spec(tr, l, sub, 0), _mod_spec(tr, l, sub, 1)],
        out_specs=pl.BlockSpec((tr, D), lambda i: (i, 0)),
        out_shape=jax.ShapeDtypeStruct((ROWS, D), BF16),
        compiler_params=_cparams(("parallel",)),
        name="modulate",
    )(*x_args, mods, mods)


def _ln_kernel(z_ref, g_ref, b_ref, *rest, emit_u):
    z = z_ref[...]
    mu = jnp.mean(z, axis=-1, keepdims=True)
    zc = z - mu
    var = jnp.mean(zc * zc, axis=-1, keepdims=True)
    xn = zc * lax.rsqrt(var + LN_EPS) * g_ref[...] + b_ref[...]
    if emit_u:
        shift_ref, scale_ref, xo_ref, u_ref = rest
        xo_ref[...] = xn
        u_ref[...] = (xn * (1.0 + scale_ref[...]) + shift_ref[...]).astype(u_ref.dtype)
    else:
        (xo_ref,) = rest
        xo_ref[...] = xn


def _ln(z, mods, ln_g, ln_b, l, sub, nxt, row0=0, n_rows=ROWS):
    tr = 256
    row = pl.BlockSpec((tr, D), lambda i: (i, 0))
    vec = pl.BlockSpec((None, None, 1, D), lambda i: (l, sub, 0, 0))
    in_specs = [pl.BlockSpec((tr, D), lambda i: (row0 // tr + i, 0)), vec, vec]
    args = [z, ln_g, ln_b]
    emit_u = nxt is not None
    if emit_u:
        assert row0 == 0 and n_rows == ROWS
        in_specs += [_mod_spec(tr, nxt[0], nxt[1], 0), _mod_spec(tr, nxt[0], nxt[1], 1)]
        args += [mods, mods]
        out_specs = [row, row]
        out_shape = [jax.ShapeDtypeStruct((ROWS, D), F32), jax.ShapeDtypeStruct((ROWS, D), BF16)]
    else:
        out_specs = row
        out_shape = jax.ShapeDtypeStruct((n_rows, D), F32)
    return pl.pallas_call(
        functools.partial(_ln_kernel, emit_u=emit_u),
        grid=(n_rows // tr,),
        in_specs=in_specs, out_specs=out_specs, out_shape=out_shape,
        compiler_params=_cparams(("parallel",)),
        name="ln",
    )(*args)


def _mm_kernel(*refs, n_a, n_x, tm, y_scale, w_rows_out):
    o_ref = refs[-1]
    dims = (((1,), (1 if w_rows_out else 0,)), ((), ()))
    dot = lambda a_ref, w_ref: lax.dot_general(a_ref[...], w_ref[...], dims, preferred_element_type=F32)
    acc = dot(refs[0], refs[n_a])
    for r in range(1, n_a):
        acc += dot(refs[r], refs[n_a + r])
    if n_x:
        gate_ref = refs[2 * n_a + n_x]
        acc = ALPHA * _pick_rows(refs[2 * n_a:2 * n_a + n_x], tm) + (y_scale * gate_ref[...]) * acc
    o_ref[...] = acc.astype(o_ref.dtype)


def _mm(a_parts, w, tm, tn, out_dtype, name, widx=(), resid=None, w_rows_out=False):
    m, k = a_parts[0].shape
    assert all(a.shape == (m, k) for a in a_parts)
    n = w.shape[-2] if w_rows_out else w.shape[-1]
    lead = (None,) * len(widx)
    in_specs = [pl.BlockSpec((tm, k), lambda i, j: (i, 0)) for _ in a_parts]
    if w_rows_out:
        in_specs += [pl.BlockSpec(lead + (tn, k), lambda i, j, r=r: widx + (j, r)) for r in range(len(a_parts))]
    else:
        in_specs += [pl.BlockSpec(lead + (k, tn), lambda i, j, r=r: widx + (r, j)) for r in range(len(a_parts))]
    args = list(a_parts) + [w] * len(a_parts)
    n_x, y_scale = 0, 1.0
    if resid is not None:
        x, mods, l, sub, y_scale = resid
        x_specs, x_args = _row_block_specs(x, tm, tn, 2)
        n_x = len(x_args)
        in_specs += x_specs + [pl.BlockSpec((None, None, None, 1, tn),
                                            lambda i, j: (l, sub * 3 + 2, _group_of_block(i, tm), 0, j))]
        args += x_args + [mods]
    return pl.pallas_call(
        functools.partial(_mm_kernel, n_a=len(a_parts), n_x=n_x, tm=tm, y_scale=y_scale, w_rows_out=w_rows_out),
        grid=(m // tm, n // tn),
        in_specs=in_specs,
        out_specs=pl.BlockSpec((tm, tn), lambda i, j: (i, j)),
        out_shape=jax.ShapeDtypeStruct((m, n), out_dtype),
        compiler_params=_cparams(("parallel", "parallel")),
        name=name,
    )(*args)


def _swiglu_kernel(a_ref, wg_ref, wu_ref, o_ref, wgb_ref, wub_ref):
    @pl.when(pl.program_id(1) == 0)
    def _():
        wgb_ref[...] = wg_ref[...].astype(BF16)
        wub_ref[...] = wu_ref[...].astype(BF16)

    a = a_ref[...]
    g = jnp.dot(a, wgb_ref[...], preferred_element_type=F32)
    u = jnp.dot(a, wub_ref[...], preferred_element_type=F32)
    o_ref[...] = (g * jax.nn.sigmoid(g) * u).astype(o_ref.dtype)


def _swiglu_in(u, w_in, l, j):
    tm, tn = 1024, 256
    nb = D_FF // tn
    return pl.pallas_call(
        _swiglu_kernel,
        grid=(nb, ROWS // tm),
        in_specs=[pl.BlockSpec((tm, D), lambda n, m: (m, 0)),
                  pl.BlockSpec((None, None, D, tn), lambda n, m: (l, j, 0, n)),
                  pl.BlockSpec((None, None, D, tn), lambda n, m: (l, j, 0, nb + n))],
        out_specs=pl.BlockSpec((tm, tn), lambda n, m: (m, n)),
        out_shape=jax.ShapeDtypeStruct((ROWS, D_FF), BF16),
        scratch_shapes=[pltpu.VMEM((D, tn), BF16), pltpu.VMEM((D, tn), BF16)],
        compiler_params=_cparams(("arbitrary", "arbitrary")),
        name="swiglu_in",
    )(u, w_in, w_in)


def _even_tables():
    t = np.arange(RB)
    band = np.zeros((2, 4, RB, RB), np.float32)
    icnt = np.zeros((2, 4, RB, 1), np.float32)
    cn = np.zeros((2, RB, RB), np.float64)
    sn = np.zeros((2, RB, RB), np.float64)
    for kind, n in enumerate((PROMPT_LEN, SAMPLE_LEN)):
        seq = t // n
        pos = t % n
        for gi, w in enumerate(POOL_WINDOWS):
            lo = np.clip(pos - w // 2, 0, n)
            hi = np.clip(pos + w // 2, 0, n)
            same = seq[:, None] == seq[None, :]
            band[kind, gi] = (same & (pos[None, :] >= lo[:, None]) & (pos[None, :] < hi[:, None]))
            icnt[kind, gi, :, 0] = 1.0 / (hi - lo)
        same = (seq[:, None] == seq[None, :]).astype(np.float64)
        ang = 2.0 * np.pi * ((pos[:, None] * pos[None, :]) % n) / n
        scale = (n * GW) ** -0.5
        cn[kind] = np.cos(ang) * same * scale
        sn[kind] = np.sin(ang) * same * scale
    c = np.arange(GW)
    angc = 2.0 * np.pi * ((c[:, None] * c[None, :]) % GW) / GW
    return (band, icnt, cn.astype(np.float32), sn.astype(np.float32),
            np.cos(angc).astype(np.float32), np.sin(angc).astype(np.float32))


def _even_mid_kernel(p_ref, band_ref, icnt_ref, pw_ref, ps_ref, cn_ref, sn_ref, cc_ref, sc_ref, o_ref):
    j = pl.program_id(1)

    @pl.when(j < 4)
    def _():
        x = p_ref[...]
        xh = x.astype(BF16)
        xl = (x - xh.astype(F32)).astype(BF16)
        band = band_ref[...]
        wsum = (jnp.dot(band, xh, preferred_element_type=F32)
                + jnp.dot(band, xl, preferred_element_type=F32))
        pooled = wsum * icnt_ref[...] - x
        y = jnp.dot(pooled.astype(BF16), pw_ref[...], preferred_element_type=F32)
        o_ref[...] = (y * ps_ref[...]).astype(o_ref.dtype)

    @pl.when(j >= 4)
    def _():
        x = p_ref[...].astype(BF16)
        xc = jnp.dot(x, cc_ref[...], preferred_element_type=F32).astype(BF16)
        xs = jnp.dot(x, sc_ref[...], preferred_element_type=F32).astype(BF16)
        y = (jnp.dot(cn_ref[...], xc, preferred_element_type=F32)
             - jnp.dot(sn_ref[...], xs, preferred_element_type=F32))
        o_ref[...] = y.astype(o_ref.dtype)


def _even_mid(p, pool_w, pool_scale):
    band, icnt, cn, sn, cc, sc = _even_tables()
    band = jnp.asarray(band).astype(BF16)
    cn, sn, cc, sc = (jnp.asarray(t).astype(BF16) for t in (cn, sn, cc, sc))
    kind = lambda i: jnp.where(i * RB < PROMPT_ROWS, 0, 1)
    grp = lambda j: jnp.minimum(j, 3)
    return pl.pallas_call(
        _even_mid_kernel,
        grid=(ROWS // RB, 8),
        in_specs=[pl.BlockSpec((RB, GW), lambda i, j: (i, j)),
                  pl.BlockSpec((None, None, RB, RB), lambda i, j: (kind(i), grp(j), 0, 0)),
                  pl.BlockSpec((None, None, RB, 1), lambda i, j: (kind(i), grp(j), 0, 0)),
                  pl.BlockSpec((None, GW, GW), lambda i, j: (grp(j), 0, 0)),
                  pl.BlockSpec((None, 1, GW), lambda i, j: (grp(j), 0, 0)),
                  pl.BlockSpec((None, RB, RB), lambda i, j: (kind(i), 0, 0)),
                  pl.BlockSpec((None, RB, RB), lambda i, j: (kind(i), 0, 0)),
                  pl.BlockSpec((GW, GW), lambda i, j: (0, 0)),
                  pl.BlockSpec((GW, GW), lambda i, j: (0, 0))],
        out_specs=pl.BlockSpec((RB, GW), lambda i, j: (i, j)),
        out_shape=jax.ShapeDtypeStruct((ROWS, D), BF16),
        compiler_params=_cparams(("parallel", "arbitrary")),
        name="even_mid",
    )(p, band, jnp.asarray(icnt), pool_w, pool_scale, cn, sn, cc, sc)


def _head_sum_matrix(width, value):
    h = np.arange(width) // HEAD
    return jnp.asarray((h[:, None] == h[None, :]).astype(np.float32) * value).astype(BF16)


def _split_bf16(x, n):
    parts = []
    for _ in range(n - 1):
        h = x.astype(BF16)
        parts.append(h)
        x = x - h.astype(F32)
    parts.append(x.astype(BF16))
    return parts


def _head_sum(x, e):
    hi, lo = _split_bf16(x, 2)
    return jnp.dot(hi, e, preferred_element_type=F32) + jnp.dot(lo, e, preferred_element_type=F32)


def _token_shift(x, mu, seq_len):
    n = x.shape[0]
    pos = lax.broadcasted_iota(jnp.int32, x.shape, 0) % seq_len
    prev = jnp.where(pos == 0, 0.0, pltpu.roll(x, 1, 0))
    nxt = jnp.where(pos == seq_len - 1, 0.0, pltpu.roll(x, n - 1, 0))
    return x + (0.5 * (prev + nxt) - x) * mu


def _rwkv_prep_kernel(pr_ref, pk_ref, pv_ref, pg_ref, pw_ref, pa_ref,
                      mr_ref, mk_ref, mv_ref, mg_ref, mw_ref, ma_ref,
                      w0_ref, w2_ref, a0_ref, aw2_ref, gw2_ref, kk_ref, ka_ref, e_ref,
                      r_o, k_o, v_o, kk_o, kka_o, lw_o, g_o, sg_ref, tw_ref, al_ref):
    i = pl.program_id(0)

    def body(seq_len):
        @pl.when(pl.program_id(1) == 0)
        def _():
            sg_ref[...] = jax.nn.sigmoid(_token_shift(pg_ref[...], mg_ref[...], seq_len)).astype(BF16)
            tw_ref[...] = jnp.tanh(_token_shift(pw_ref[...], mw_ref[...], seq_len)).astype(BF16)
            al_ref[...] = _token_shift(pa_ref[...], ma_ref[...], seq_len).astype(BF16)

        r = _token_shift(pr_ref[...], mr_ref[...], seq_len)
        k = _token_shift(pk_ref[...], mk_ref[...], seq_len)
        v = _token_shift(pv_ref[...], mv_ref[...], seq_len)
        for d in range(2):
            x = w0_ref[d] + jnp.dot(tw_ref[:, d * DECAY_LORA:(d + 1) * DECAY_LORA], w2_ref[d],
                                    preferred_element_type=F32)
            z = -x
            softplus = jnp.maximum(z, 0.0) + jnp.log(1.0 + jnp.exp(-jnp.abs(z)))
            lw_o[d] = -jnp.exp(-softplus - 0.5)
        a = jax.nn.sigmoid(a0_ref[...] + jnp.dot(al_ref[...], aw2_ref[...], preferred_element_type=F32))
        g_o[...] = jnp.dot(sg_ref[...], gw2_ref[...], preferred_element_type=F32)
        kk = k * kk_ref[...]
        ss = _head_sum(kk * kk, e_ref[...])
        kk = kk * lax.rsqrt(ss + 1e-12)
        r_o[...] = r
        k_o[...] = k * (1.0 + (a - 1.0) * ka_ref[...])
        v_o[...] = v
        kk_o[...] = kk
        kka_o[...] = kk * a

    @pl.when(i * RB < PROMPT_ROWS)
    def _():
        body(PROMPT_LEN)

    @pl.when(i * RB >= PROMPT_ROWS)
    def _():
        body(SAMPLE_LEN)


def _rwkv_prep(p, mu, w0, w2, a0, aw2, gw2, k_k, k_a):
    cw = 256
    ncb = RWKV_W // cw
    e = _head_sum_matrix(cw, 1.0)
    pcol = lambda base, width: pl.BlockSpec((RB, width), lambda i, c: (i, base // width))
    pch = lambda base: pl.BlockSpec((RB, cw), lambda i, c: (i, base // cw + c))
    mcol = lambda base, width: pl.BlockSpec((1, width), lambda i, c: (0, base // width))
    mch = lambda base: pl.BlockSpec((1, cw), lambda i, c: (0, base // cw + c))
    vch = pl.BlockSpec((1, cw), lambda i, c: (0, c))
    out = pl.BlockSpec((RB, cw), lambda i, c: (i, c))
    sds = jax.ShapeDtypeStruct((ROWS, RWKV_W), F32)
    return pl.pallas_call(
        _rwkv_prep_kernel,
        grid=(ROWS // RB, ncb),
        in_specs=[pch(C_R), pch(C_K), pch(C_V), pcol(C_G, GATE_PAD), pcol(C_W, 2 * DECAY_LORA), pcol(C_A, ICLR_LORA),
                  mch(C_R), mch(C_K), mch(C_V), mcol(C_G, GATE_PAD), mcol(C_W, 2 * DECAY_LORA), mcol(C_A, ICLR_LORA),
                  pl.BlockSpec((2, 1, cw), lambda i, c: (0, 0, c)),
                  pl.BlockSpec((2, DECAY_LORA, cw), lambda i, c: (0, 0, c)),
                  vch,
                  pl.BlockSpec((ICLR_LORA, cw), lambda i, c: (0, c)),
                  pl.BlockSpec((GATE_PAD, cw), lambda i, c: (0, c)),
                  vch, vch,
                  pl.BlockSpec((cw, cw), lambda i, c: (0, 0))],
        out_specs=[out, out, out, out, out,
                   pl.BlockSpec((2, RB, cw), lambda i, c: (0, i, c)), out],
        out_shape=[sds, sds, sds, sds, sds, jax.ShapeDtypeStruct((2, ROWS, RWKV_W), F32), sds],
        scratch_shapes=[pltpu.VMEM((RB, GATE_PAD), BF16), pltpu.VMEM((RB, 2 * DECAY_LORA), BF16),
                        pltpu.VMEM((RB, ICLR_LORA), BF16)],
        compiler_params=_cparams(("parallel", "arbitrary")),
        name="rwkv_prep",
    )(p, p, p, p, p, p, mu, mu, mu, mu, mu, mu, w0, w2, a0, aw2, gw2, k_k, k_a, e)


SCAN_PASSES = {"scores": 1, "ws": 1, "mv": 1, "inv": 1, "u": 1, "y": 1, "state": 1}


def _mmp(a, b, passes, nt=False):
    dims = (((1,), (1 if nt else 0,)), ((), ()))
    dot = lambda x, y: lax.dot_general(x, y, dims, preferred_element_type=F32)
    if passes == 1:
        return dot(a.astype(BF16), b.astype(BF16))
    ah, al = _split_bf16(a, 2)
    bh, bl = _split_bf16(b, 2)
    return dot(ah, bh) + (dot(al, bh) + dot(ah, bl))


def _rwkv_scan_kernel(*refs, has_init, pairs):
    in_refs = (refs[0:6], refs[6:12])
    s0_ref = refs[12] if has_init else None
    y_refs = refs[-4:-2]
    s_ref, sp_ref = refs[-2:]
    c_idx = pl.program_id(1)
    n_chunks = pl.num_programs(1)
    L = CHUNK
    W2 = 2 * HEAD
    P = SCAN_PASSES

    @pl.when(c_idx == 0)
    def _():
        sp_ref[...] = jnp.zeros_like(sp_ref)
        if has_init:
            for d in range(2):
                for p in range(pairs):
                    sp_ref[d * pairs + p, 0:HEAD, 0:HEAD] = s0_ref[d, 2 * p]
                    sp_ref[d * pairs + p, HEAD:W2, HEAD:W2] = s0_ref[d, 2 * p + 1]

    row = lax.broadcasted_iota(jnp.int32, (L, W2), 0)
    lane = lax.broadcasted_iota(jnp.int32, (L, W2), 1)
    col = lane % HEAD
    first = lane < HEAD
    incl_d = (col <= row, col >= row)
    strict_d = (col < row, col > row)
    eye = (col == row).astype(F32)
    blk8 = (row // 8) == (col // 8)
    off = [((row // (2 * b)) == (col // (2 * b))) & ((row // b) != (col // b)) for b in (8, 16, 32)]
    row_s = lax.broadcasted_iota(jnp.int32, (W2, W2), 0)
    lane_s = lax.broadcasted_iota(jnp.int32, (W2, W2), 1)
    diag_blocks = (row_s < HEAD) == (lane_s < HEAD)
    trow = lax.broadcasted_iota(jnp.int32, (L, L), 0)
    tcol = lax.broadcasted_iota(jnp.int32, (L, L), 1)
    tri_d = ((tcol <= trow).astype(BF16), (tcol >= trow).astype(BF16))

    def bd(x):
        return jnp.concatenate([jnp.where(first, x, 0.0), jnp.where(first, 0.0, x)], axis=0)

    ps = [slice(p * W2, (p + 1) * W2) for p in range(pairs)]
    each = lambda f, *xs: [f(*z) for z in zip(*xs)]
    ar, bt, kt, v, g_last = [], [], [], [], []
    for d in range(2):
        lw_ref, r_ref, k_ref, v_ref, a_ref, b_ref = in_refs[d]
        lw = lw_ref[...]
        c = sum(jnp.dot(tri_d[d], t, preferred_element_type=F32) for t in _split_bf16(lw, 3))
        g_in = jnp.exp(c)
        g_inv = jnp.exp(-c)
        at_all = a_ref[...] * jnp.exp(c - lw)
        rt_all = r_ref[...] * g_in
        bt_all = b_ref[...] * g_inv
        kt_all = k_ref[...] * g_inv
        v_all = v_ref[...]
        g_end = g_in[0:1, :] if d else g_in[L - 1:L, :]
        ar += [jnp.concatenate([at_all[:, s], rt_all[:, s]], axis=0) for s in ps]
        bt += [bt_all[:, s] for s in ps]
        kt += [kt_all[:, s] for s in ps]
        v += [v_all[:, s] for s in ps]
        g_last += [g_end[:, s] for s in ps]
    n_inst = 2 * pairs
    strict = [strict_d[i // pairs] for i in range(n_inst)]
    incl = [incl_d[i // pairs] for i in range(n_inst)]
    s0 = [sp_ref[i] for i in range(n_inst)]
    npq = each(lambda x, y: _mmp(x, bd(y), P["scores"], nt=True), ar, bt)
    mq = each(lambda x, y: _mmp(x, bd(y), P["scores"], nt=True), ar, kt)
    ws = each(lambda x, y: _mmp(x, y, P["ws"], nt=True), ar, s0)
    n_mat = each(lambda m, x: jnp.where(m, x[:L], 0.0), strict, npq)
    m_mat = each(lambda m, x: jnp.where(m, x[:L], 0.0), strict, mq)
    p_mat = each(lambda m, x: jnp.where(m, x[L:], 0.0), incl, npq)
    q_mat = each(lambda m, x: jnp.where(m, x[L:], 0.0), incl, mq)
    w = each(lambda x, m, y: x[:L] + _mmp(m, bd(y), P["mv"]), ws, m_mat, v)
    n8 = [jnp.where(blk8, x, 0.0) for x in n_mat]
    n8_2 = each(lambda x: _mmp(x, bd(x), P["inv"]), n8)
    n8_4 = each(lambda x: _mmp(x, bd(x), P["inv"]), n8_2)
    t = each(lambda x, y: _mmp(eye - x, bd(eye + y), P["inv"]), n8, n8_2)
    t = each(lambda x, y: _mmp(x, bd(eye + y), P["inv"]), t, n8_4)
    for m in off:
        x = each(lambda n, y: _mmp(jnp.where(m, n, 0.0), bd(y), P["inv"]), n_mat, t)
        t = each(lambda y, z: y - _mmp(y, bd(z), P["inv"]), t, x)
    u = each(lambda x, y: -_mmp(x, bd(y), P["u"]), t, w)
    y = each(lambda x, pm, qm, uu, vv: x[L:] + _mmp(jnp.concatenate([pm, qm], axis=1),
                                                     jnp.concatenate([bd(uu), bd(vv)], axis=0), P["y"]),
             ws, p_mat, q_mat, u, v)
    for i in range(n_inst):
        y_refs[i // pairs][:, ps[i % pairs]] = y[i]
    upd = each(lambda uu, vv, b, k: _mmp(jnp.concatenate([uu, vv], axis=0).T,
                                         jnp.concatenate([b, k], axis=0), P["state"]), u, v, bt, kt)
    s_new = each(lambda s, x, g: (s + jnp.where(diag_blocks, x, 0.0)) * g, s0, upd, g_last)
    for i in range(n_inst):
        sp_ref[i] = s_new[i]

    @pl.when(c_idx == n_chunks - 1)
    def _():
        for i in range(n_inst):
            d, p = divmod(i, pairs)
            s_ref[d, 2 * p] = s_new[i][0:HEAD, 0:HEAD]
            s_ref[d, 2 * p + 1] = s_new[i][HEAD:W2, HEAD:W2]


def _rwkv_scan(lw, r, k, v, a, b, s0, y_prev, *, n_seq, seq_len, row0):
    pairs = N_HEADS // 2
    nc = seq_len // CHUNK

    def rows(d):
        return lambda s, c: row0 // CHUNK + s * nc + (nc - 1 - c if d else c)

    in_specs, args = [], []
    for d in range(2):
        in_specs.append(pl.BlockSpec((None, CHUNK, RWKV_W), lambda s, c, d=d: (d, rows(d)(s, c), 0)))
        in_specs += [pl.BlockSpec((CHUNK, RWKV_W), lambda s, c, d=d: (rows(d)(s, c), 0))] * 5
        args += [lw, r, k, v, a, b]
    if s0 is not None:
        in_specs.append(pl.BlockSpec((None, 2, N_HEADS, HEAD, HEAD), lambda s, c: (s, 0, 0, 0, 0)))
        args.append(s0)
    aliases = {}
    if y_prev is not None:
        aliases = {len(args): 0, len(args) + 1: 1}
        in_specs += [pl.BlockSpec(memory_space=pl.ANY)] * 2
        args += list(y_prev)
    y_sds = jax.ShapeDtypeStruct((ROWS, RWKV_W), F32)
    return pl.pallas_call(
        functools.partial(_rwkv_scan_kernel, has_init=s0 is not None, pairs=pairs),
        grid=(n_seq, nc),
        in_specs=in_specs,
        out_specs=[pl.BlockSpec((CHUNK, RWKV_W), lambda s, c: (rows(0)(s, c), 0)),
                   pl.BlockSpec((CHUNK, RWKV_W), lambda s, c: (rows(1)(s, c), 0)),
                   pl.BlockSpec((None, 2, N_HEADS, HEAD, HEAD), lambda s, c: (s, 0, 0, 0, 0))],
        out_shape=[y_sds, y_sds, jax.ShapeDtypeStruct((n_seq, 2, N_HEADS, HEAD, HEAD), F32)],
        scratch_shapes=[pltpu.VMEM((2 * pairs, 2 * HEAD, 2 * HEAD), F32)],
        input_output_aliases=aliases,
        compiler_params=_cparams(("parallel", "arbitrary")),
        name="rwkv_scan",
    )(*args)


def _rwkv_post_kernel(yf_ref, yb_ref, r_ref, k_ref, v_ref, g_ref, gg_ref, gb_ref, rk_ref, e_ref, o_ref):
    y = yf_ref[...] + yb_ref[...]
    e = e_ref[...]
    mu = _head_sum(y, e)
    yc = y - mu
    var = _head_sum(yc * yc, e)
    yn = yc * lax.rsqrt(var + GN_EPS) * gg_ref[...] + gb_ref[...]
    bonus = _head_sum(r_ref[...] * k_ref[...] * rk_ref[...], e) * float(HEAD) * v_ref[...]
    o_ref[...] = ((yn + bonus) * g_ref[...]).astype(o_ref.dtype)


def _rwkv_post(yf, yb, r, k, v, g, gn_g, gn_b, r_k):
    tr, cw = 512, 512
    e = _head_sum_matrix(cw, 1.0 / HEAD)
    blk = pl.BlockSpec((tr, cw), lambda i, c: (i, c))
    vec = pl.BlockSpec((1, cw), lambda i, c: (0, c))
    return pl.pallas_call(
        _rwkv_post_kernel,
        grid=(ROWS // tr, RWKV_W // cw),
        in_specs=[blk, blk, blk, blk, blk, blk, vec, vec, vec, pl.BlockSpec((cw, cw), lambda i, c: (0, 0))],
        out_specs=blk,
        out_shape=jax.ShapeDtypeStruct((ROWS, RWKV_W), BF16),
        compiler_params=_cparams(("parallel", "parallel")),
        name="rwkv_post",
    )(yf, yb, r, k, v, g, gn_g, gn_b, r_k, e)


def _rms(x, gain):
    return x * lax.rsqrt(jnp.mean(x * x, axis=-1, keepdims=True) + 1e-6) * gain


def _rope(x, cos, sin_signed):
    lane = lax.broadcasted_iota(jnp.int32, x.shape, 1)
    quarter = HEAD_DIM // 4
    partner = jnp.where(lane % (2 * quarter) < quarter,
                        pltpu.roll(x, HEAD_DIM - quarter, 1), pltpu.roll(x, quarter, 1))
    return x * cos + partner * sin_signed


def _rope_tables():
    n = SAMPLE_LEN
    quarter = HEAD_DIM // 4
    row = (np.arange(n) // GRID_W).astype(np.float64)
    col = (np.arange(n) % GRID_W).astype(np.float64)
    inv_freq = ROPE_THETA ** (-np.arange(quarter, dtype=np.float64) / quarter)
    ar = row[:, None] * inv_freq[None, :]
    ac = col[:, None] * inv_freq[None, :]
    cos = np.concatenate([np.cos(ar), np.cos(ar), np.cos(ac), np.cos(ac)], axis=1)
    sin = np.concatenate([-np.sin(ar), np.sin(ar), -np.sin(ac), np.sin(ac)], axis=1)
    return cos.astype(np.float32), sin.astype(np.float32)


def _attn_prompt_kernel(q_ref, k_ref, v_ref, qn_ref, kn_ref, o_ref, ko_ref, vo_ref):
    scale = HEAD_DIM ** -0.5
    kn = _rms(k_ref[...], kn_ref[...])
    v = v_ref[...]
    ko_ref[...] = kn
    vo_ref[...] = v
    kb = kn.astype(BF16)
    vb = v.astype(BF16)
    for j in range(ATT_W // KV_W):
        hs = slice(j * HEAD_DIM, (j + 1) * HEAD_DIM)
        q = (_rms(q_ref[:, hs], qn_ref[...]) * scale).astype(BF16)
        s = lax.dot_general(q, kb, (((1,), (1,)), ((), ())), preferred_element_type=F32)
        e = jnp.exp(s - jnp.max(s, axis=-1, keepdims=True))
        inv = 1.0 / jnp.sum(e, axis=-1, keepdims=True)
        o_ref[:, hs] = (jnp.dot(e.astype(BF16), vb, preferred_element_type=F32) * inv).astype(o_ref.dtype)


def _attn_prompt(p, q_norm, k_norm):
    n = PROMPT_LEN
    qw = ATT_W // N_KV
    return pl.pallas_call(
        _attn_prompt_kernel,
        grid=(N_PROMPT_SEQ, N_KV),
        in_specs=[pl.BlockSpec((n, qw), lambda s, g: (s, C_Q // qw + g)),
                  pl.BlockSpec((n, HEAD_DIM), lambda s, g: (s, C_AK // HEAD_DIM + g)),
                  pl.BlockSpec((n, HEAD_DIM), lambda s, g: (s, C_AV // HEAD_DIM + g)),
                  pl.BlockSpec((1, HEAD_DIM), lambda s, g: (0, 0)),
                  pl.BlockSpec((1, HEAD_DIM), lambda s, g: (0, 0))],
        out_specs=[pl.BlockSpec((n, qw), lambda s, g: (s, g)),
                   pl.BlockSpec((n, HEAD_DIM), lambda s, g: (s, g)),
                   pl.BlockSpec((n, HEAD_DIM), lambda s, g: (s, g))],
        out_shape=[jax.ShapeDtypeStruct((ROWS, ATT_W), BF16),
                   jax.ShapeDtypeStruct((PROMPT_ROWS, KV_W), F32),
                   jax.ShapeDtypeStruct((PROMPT_ROWS, KV_W), F32)],
        compiler_params=_cparams(("parallel", "parallel")),
        name="attn_prompt",
    )(p, p, p, q_norm, k_norm)


def _attn_sample_kernel(q_ref, k_ref, v_ref, ck_ref, cv_ref, qn_ref, kn_ref, cos_ref, sin_ref, _o_prev, o_ref):
    scale = HEAD_DIM ** -0.5
    tq = 256
    cos = cos_ref[...]
    sin = sin_ref[...]
    ks = _rope(_rms(k_ref[...], kn_ref[...]), cos, sin).astype(BF16)
    vs = v_ref[...].astype(BF16)
    kc = ck_ref[...].astype(BF16)
    vc = cv_ref[...].astype(BF16)
    nt = (((1,), (1,)), ((), ()))
    for j in range(ATT_W // KV_W):
        hs = slice(j * HEAD_DIM, (j + 1) * HEAD_DIM)
        for qb in range(SAMPLE_LEN // tq):
            rs = slice(qb * tq, (qb + 1) * tq)
            q = (_rope(_rms(q_ref[rs, hs], qn_ref[...]), cos[rs], sin[rs]) * scale).astype(BF16)
            s1 = lax.dot_general(q, kc, nt, preferred_element_type=F32)
            s2 = lax.dot_general(q, ks, nt, preferred_element_type=F32)
            m = jnp.maximum(jnp.max(s1, axis=-1, keepdims=True), jnp.max(s2, axis=-1, keepdims=True))
            e1 = jnp.exp(s1 - m)
            e2 = jnp.exp(s2 - m)
            inv = 1.0 / (jnp.sum(e1, axis=-1, keepdims=True) + jnp.sum(e2, axis=-1, keepdims=True))
            o = (jnp.dot(e1.astype(BF16), vc, preferred_element_type=F32)
                 + jnp.dot(e2.astype(BF16), vs, preferred_element_type=F32))
            o_ref[rs, hs] = (o * inv).astype(o_ref.dtype)


def _attn_sample(p, cache_k, cache_v, q_norm, k_norm, o_prev):
    n = SAMPLE_LEN
    qw = ATT_W // N_KV
    r0 = PROMPT_ROWS // n
    cos, sin = _rope_tables()
    return pl.pallas_call(
        _attn_sample_kernel,
        grid=(N_SAMPLE_SEQ, N_KV),
        in_specs=[pl.BlockSpec((n, qw), lambda s, g: (r0 + s, C_Q // qw + g)),
                  pl.BlockSpec((n, HEAD_DIM), lambda s, g: (r0 + s, C_AK // HEAD_DIM + g)),
                  pl.BlockSpec((n, HEAD_DIM), lambda s, g: (r0 + s, C_AV // HEAD_DIM + g)),
                  pl.BlockSpec((PAST_LEN, HEAD_DIM), lambda s, g: (s, g)),
                  pl.BlockSpec((PAST_LEN, HEAD_DIM), lambda s, g: (s, g)),
                  pl.BlockSpec((1, HEAD_DIM), lambda s, g: (0, 0)),
                  pl.BlockSpec((1, HEAD_DIM), lambda s, g: (0, 0)),
                  pl.BlockSpec((n, HEAD_DIM), lambda s, g: (0, 0)),
                  pl.BlockSpec((n, HEAD_DIM), lambda s, g: (0, 0)),
                  pl.BlockSpec(memory_space=pl.ANY)],
        out_specs=pl.BlockSpec((n, qw), lambda s, g: (r0 + s, g)),
        out_shape=jax.ShapeDtypeStruct((ROWS, ATT_W), BF16),
        input_output_aliases={9: 0},
        compiler_params=_cparams(("parallel", "parallel")),
        name="attn_sample",
    )(p, p, p, cache_k, cache_v, q_norm, k_norm, jnp.asarray(cos), jnp.asarray(sin), o_prev)


_ODD_SEGMENTS = ((0, C_R, 6144), (7008, C_Q, ATT_W), (9056, C_AK, KV_W), (9568, C_AV, KV_W),
                 (6144, C_G, GATE_LORA), (6624, C_W, 2 * DECAY_LORA), (6880, C_A, ICLR_LORA))
_ODD_ZERO = ((C_G + GATE_LORA, GATE_PAD - GATE_LORA), (C_A + ICLR_LORA, ODD_COLS - C_A - ICLR_LORA))


def _pack_odd_kernel(w_ref, o_ref):
    for src, dst, width in _ODD_SEGMENTS:
        o_ref[dst:dst + width, :] = w_ref[src:src + width, :].astype(o_ref.dtype)
    for dst, width in _ODD_ZERO:
        o_ref[dst:dst + width, :] = jnp.zeros((width, o_ref.shape[1]), o_ref.dtype)


def _pack_odd_w_in(wt):
    tc = 256
    n, k = wt.shape
    return pl.pallas_call(
        _pack_odd_kernel,
        grid=(k // tc,),
        in_specs=[pl.BlockSpec((n, tc), lambda j: (0, j))],
        out_specs=pl.BlockSpec((ODD_COLS, tc), lambda j: (0, j)),
        out_shape=jax.ShapeDtypeStruct((ODD_COLS, k), BF16),
        compiler_params=_cparams(("parallel",)),
        name="pack_odd_w_in",
    )(wt)


def _pack_shift_mu(mu):
    z = lambda n: jnp.zeros((n,), mu.dtype)
    out = jnp.concatenate([mu[0:6144], z(C_G - C_Q), mu[6144:6144 + GATE_LORA], z(GATE_PAD - GATE_LORA),
                           mu[6624:6880], mu[6880:7008], z(ODD_COLS - C_A - ICLR_LORA)])
    return out[None, :]


def kernel(x_prompt, x_sample, cache_k, cache_v, state_rwkv, c, c_ctx, w_ada, b_ada, ln_g, ln_b, ffn_w_in,
           ffn_w_out, even_w_in, pool_w, pool_scale, even_w_out, odd_w_in, shift_mu, decay_w0, decay_w2,
           iclr_a0, iclr_w2, gate_w2, k_k, k_a, r_k, gn_g, gn_b, q_norm, k_norm, odd_w_out):
    x = (x_prompt.reshape(PROMPT_ROWS, D), x_sample.reshape(ROWS - PROMPT_ROWS, D))
    cond =jnp.concatenate([c_ctx[None, :], c, jnp.zeros((N_GROUPS - 1 - c.shape[0], D), F32)], axis=0)
    mods = _ada(cond, w_ada, b_ada[:, None, :])
    mods = mods.reshape(DEPTH, N_GROUPS, 3 * N_SUB, D).transpose(0, 2, 1, 3)[:, :, :, None, :]
    ln_g4 = ln_g[:, :, None, :]
    ln_b4 = ln_b[:, :, None, :]
    ffn_w_out_b = ffn_w_out.astype(BF16)

    u = _modulate(x, mods, 0, 0)
    new_k = new_v = new_s = None
    for l in range(DEPTH):
        i = l // 2
        h = _swiglu_in(u, ffn_w_in, l, 0)
        z = _mm([h], ffn_w_out_b, 512, 256, F32, "ffn_out", widx=(l, 0), resid=(x, mods, l, 0, 0.5))
        x, u = _ln(z, mods, ln_g4, ln_b4, l, 0, (l, 1))
        if l % 2 == 0:
            p = _mm([u], even_w_in[i].astype(BF16), 1024, 512, F32, "even_in")
            mid = _even_mid(p, pool_w[i].astype(BF16), pool_scale[i].reshape(4, 1, GW))
            z = _mm([mid], even_w_out[i].astype(BF16), 1024, 512, F32, "even_out", resid=(x, mods, l, 1, 1.0))
        else:
            p = _mm([u], _pack_odd_w_in(jnp.swapaxes(odd_w_in[i], 0, 1)), 1024, 512, F32, "odd_in",
                    w_rows_out=True)
            gw2 = jnp.concatenate([gate_w2[i], jnp.zeros((GATE_PAD - GATE_LORA, RWKV_W), F32)], axis=0)
            r, k, v, kk, kka, lw, g = _rwkv_prep(
                p, _pack_shift_mu(shift_mu[i]), decay_w0[i][:, None, :], decay_w2[i].astype(BF16),
                iclr_a0[i][None, :], iclr_w2[i].astype(BF16), gw2.astype(BF16), k_k[i][None, :], k_a[i][None, :])
            yf, yb, s_prompt = _rwkv_scan(lw, r, k, v, kk, kka, None, None,
                                          n_seq=N_PROMPT_SEQ, seq_len=PROMPT_LEN, row0=0)
            yf, yb, _ = _rwkv_scan(lw, r, k, v, kk, kka, state_rwkv[:, i], (yf, yb),
                                   n_seq=N_SAMPLE_SEQ, seq_len=SAMPLE_LEN, row0=PROMPT_ROWS)
            y_rwkv = _rwkv_post(yf, yb, r, k, v, g, gn_g[i][None, :], gn_b[i][None, :],
                                r_k[i].reshape(1, RWKV_W))
            o_p, nk, nv = _attn_prompt(p, q_norm[i][None, :], k_norm[i][None, :])
            o = _attn_sample(p, cache_k[:, i].reshape(N_SAMPLE_SEQ * PAST_LEN, KV_W),
                             cache_v[:, i].reshape(N_SAMPLE_SEQ * PAST_LEN, KV_W),
                             q_norm[i][None, :], k_norm[i][None, :], o_p)
            z = _mm([y_rwkv, o], odd_w_out[i].astype(BF16), 1024, 512, F32, "odd_out", resid=(x, mods, l, 1, 1.0))
            new_k = nk.reshape(N_PROMPT_SEQ, 1, PROMPT_LEN, N_KV, HEAD_DIM)
            new_v = nv.reshape(N_PROMPT_SEQ, 1, PROMPT_LEN, N_KV, HEAD_DIM)
            new_s = s_prompt[:, None]
        x, u = _ln(z, mods, ln_g4, ln_b4, l, 1, (l, 2))
        h = _swiglu_in(u, ffn_w_in, l, 1)
        z = _mm([h], ffn_w_out_b, 512, 256, F32, "ffn_out", widx=(l, 1), resid=(x, mods, l, 2, 0.5))
        if l + 1 < DEPTH:
            x, u = _ln(z, mods, ln_g4, ln_b4, l, 2, (l + 1, 0))
        else:
            y_prompt = _ln(z, mods, ln_g4, ln_b4, l, 2, None, 0, PROMPT_ROWS)
            y_sample = _ln(z, mods, ln_g4, ln_b4, l, 2, None, PROMPT_ROWS, ROWS - PROMPT_ROWS)

    return (y_prompt.reshape(N_PROMPT_SEQ, PROMPT_LEN, D), y_sample.reshape(N_SAMPLE_SEQ, SAMPLE_LEN, D),
            new_k, new_v, new_s)
```

```python
import functools

import numpy as np
import jax
import jax.numpy as jnp
from jax import lax
from jax.experimental import pallas as pl
from jax.experimental.pallas import tpu as pltpu

F32 = jnp.float32
BF16 = jnp.bfloat16

D = 4096
DEPTH = 2
N_SUB = 3
D_FF = 11008
N_PROMPT_SEQ = 16
PROMPT_LEN = 256
N_SAMPLE_SEQ = 4
SAMPLE_LEN = 1024
PAST_LEN = 512
GRID_W = 64
PROMPT_ROWS = N_PROMPT_SEQ * PROMPT_LEN
ROWS = PROMPT_ROWS + N_SAMPLE_SEQ * SAMPLE_LEN
RB = 1024
N_GROUPS = 8
POOL_WINDOWS = (2, 4, 8, 16)
GW = 512
RWKV_W = 2048
HEAD = 64
N_HEADS = 32
GATE_LORA = 480
GATE_PAD = 512
DECAY_LORA = 128
ICLR_LORA = 128
ATT_W = 2048
HEAD_DIM = 128
N_KV = 4
KV_W = N_KV * HEAD_DIM
ROPE_THETA = 10000.0
ALPHA = (2 * DEPTH) ** 0.25
LN_EPS = 1e-5
GN_EPS = 64e-5
CHUNK = 64

C_R, C_K, C_V, C_Q = 0, 2048, 4096, 6144
C_AK, C_AV, C_G, C_W, C_A = 8192, 8704, 9216, 9728, 9984
ODD_COLS = 10240

VMEM_LIMIT = 56 * 1024 * 1024


def _cparams(sem):
    return pltpu.CompilerParams(dimension_semantics=sem, vmem_limit_bytes=VMEM_LIMIT)


def _group_of_block(i, rows_per_block):
    r0 = i * rows_per_block
    return jnp.where(r0 < PROMPT_ROWS, 0, 1 + (r0 - PROMPT_ROWS) // SAMPLE_LEN)


def _ada_kernel(c_ref, w_ref, b_ref, o_ref):
    c = c_ref[...]
    s = (c * jax.nn.sigmoid(c)).astype(BF16)
    o_ref[...] = jnp.dot(s, w_ref[...].astype(BF16), preferred_element_type=F32) + b_ref[...]


def _ada(cond, w_ada, b_ada):
    tn = 512
    n = w_ada.shape[-1]
    return pl.pallas_call(
        _ada_kernel,
        grid=(DEPTH, n // tn),
        in_specs=[pl.BlockSpec((N_GROUPS, D), lambda l, j: (0, 0)),
                  pl.BlockSpec((None, D, tn), lambda l, j: (l, 0, j)),
                  pl.BlockSpec((None, 1, tn), lambda l, j: (l, 0, j))],
        out_specs=pl.BlockSpec((None, N_GROUPS, tn), lambda l, j: (l, 0, j)),
        out_shape=jax.ShapeDtypeStruct((DEPTH, N_GROUPS, n), F32),
        compiler_params=_cparams(("parallel", "parallel")),
        name="ada",
    )(cond, w_ada, b_ada)


def _mod_spec(tr, l, sub, kind):
    return pl.BlockSpec((None, None, None, 1, D),
                        lambda i: (l, sub * 3 + kind, _group_of_block(i, tr), 0, 0))


def _row_block_specs(x, tr, tn, rank):
    col = (lambda ij: ij[1]) if rank == 2 else (lambda ij: 0)
    if not isinstance(x, tuple):
        return [pl.BlockSpec((tr, tn), lambda *ij: (ij[0], col(ij)))], [x]
    nb = PROMPT_ROWS // tr
    return ([pl.BlockSpec((tr, tn), lambda *ij: (jnp.minimum(ij[0], nb - 1), col(ij))),
             pl.BlockSpec((tr, tn), lambda *ij: (jnp.maximum(ij[0] - nb, 0), col(ij)))], list(x))


def _pick_rows(x_refs, tr):
    if len(x_refs) == 1:
        return x_refs[0][...]
    return jnp.where(pl.program_id(0) * tr < PROMPT_ROWS, x_refs[0][...], x_refs[1][...])


def _modulate_kernel(*refs, n_x, tr):
    shift_ref, scale_ref, u_ref = refs[n_x:]
    x = _pick_rows(refs[:n_x], tr)
    u_ref[...] = (x * (1.0 + scale_ref[...]) + shift_ref[...]).astype(u_ref.dtype)


def _modulate(x, mods, l, sub):
    tr = 256
    x_specs, x_args = _row_block_specs(x, tr, D, 1)
    return pl.pallas_call(
        functools.partial(_modulate_kernel, n_x=len(x_args), tr=tr),
        grid=(ROWS // tr,),
        in_specs=x_specs + [_mod_spec(tr, l, sub, 0), _mod_spec(tr, l, sub, 1)],
        out_specs=pl.BlockSpec((tr, D), lambda i: (i, 0)),
        out_shape=jax.ShapeDtypeStruct((ROWS, D), BF16),
        compiler_params=_cparams(("parallel",)),
        name="modulate",
    )(*x_args, mods, mods)


def _ln_kernel(z_ref, g_ref, b_ref, *rest, emit_u):
    z = z_ref[...]
    mu = jnp.mean(z, axis=-1, keepdims=True)
    zc = z - mu
    var = jnp.mean(zc * zc, axis=-1, keepdims=True)
    xn = zc * lax.rsqrt(var + LN_EPS) * g_ref[...] + b_ref[...]
    if emit_u:
        shift_ref, scale_ref, xo_ref, u_ref = rest
        xo_ref[...] = xn
        u_ref[...] = (xn * (1.0 + scale_ref[...]) + shift_ref[...]).astype(u_ref.dtype)
    else:
        (xo_ref,) = rest
        xo_ref[...] = xn


def _ln(z, mods, ln_g, ln_b, l, sub, nxt, row0=0, n_rows=ROWS):
    tr = 256
    row = pl.BlockSpec((tr, D), lambda i: (i, 0))
    vec = pl.BlockSpec((None, None, 1, D), lambda i: (l, sub, 0, 0))
    in_specs = [pl.BlockSpec((tr, D), lambda i: (row0 // tr + i, 0)), vec, vec]
    args = [z, ln_g, ln_b]
    emit_u = nxt is not None
    if emit_u:
        assert row0 == 0 and n_rows == ROWS
        in_specs += [_mod_spec(tr, nxt[0], nxt[1], 0), _mod_spec(tr, nxt[0], nxt[1], 1)]
        args += [mods, mods]
        out_specs = [row, row]
        out_shape = [jax.ShapeDtypeStruct((ROWS, D), F32), jax.ShapeDtypeStruct((ROWS, D), BF16)]
    else:
        out_specs = row
        out_shape = jax.ShapeDtypeStruct((n_rows, D), F32)
    return pl.pallas_call(
        functools.partial(_ln_kernel, emit_u=emit_u),
        grid=(n_rows // tr,),
        in_specs=in_specs, out_specs=out_specs, out_shape=out_shape,
        compiler_params=_cparams(("parallel",)),
        name="ln",
    )(*args)


def _mm_kernel(*refs, n_a, n_x, tm, y_scale, w_rows_out):
    o_ref = refs[-1]
    dims = (((1,), (1 if w_rows_out else 0,)), ((), ()))
    dot = lambda a_ref, w_ref: lax.dot_general(a_ref[...], w_ref[...], dims, preferred_element_type=F32)
    acc = dot(refs[0], refs[n_a])
    for r in range(1, n_a):
        acc += dot(refs[r], refs[n_a + r])
    if n_x:
        gate_ref = refs[2 * n_a + n_x]
        acc = ALPHA * _pick_rows(refs[2 * n_a:2 * n_a + n_x], tm) + (y_scale * gate_ref[...]) * acc
    o_ref[...] = acc.astype(o_ref.dtype)


def _mm(a_parts, w, tm, tn, out_dtype, name, widx=(), resid=None, w_rows_out=False):
    m, k = a_parts[0].shape
    assert all(a.shape == (m, k) for a in a_parts)
    n = w.shape[-2] if w_rows_out else w.shape[-1]
    lead = (None,) * len(widx)
    in_specs = [pl.BlockSpec((tm, k), lambda i, j: (i, 0)) for _ in a_parts]
    if w_rows_out:
        in_specs += [pl.BlockSpec(lead + (tn, k), lambda i, j, r=r: widx + (j, r)) for r in range(len(a_parts))]
    else:
        in_specs += [pl.BlockSpec(lead + (k, tn), lambda i, j, r=r: widx + (r, j)) for r in range(len(a_parts))]
    args = list(a_parts) + [w] * len(a_parts)
    n_x, y_scale = 0, 1.0
    if resid is not None:
        x, mods, l, sub, y_scale = resid
        x_specs, x_args = _row_block_specs(x, tm, tn, 2)
        n_x = len(x_args)
        in_specs += x_specs + [pl.BlockSpec((None, None, None, 1, tn),
                                            lambda i, j: (l, sub * 3 + 2, _group_of_block(i, tm), 0, j))]
        args += x_args + [mods]
    return pl.pallas_call(
        functools.partial(_mm_kernel, n_a=len(a_parts), n_x=n_x, tm=tm, y_scale=y_scale, w_rows_out=w_rows_out),
        grid=(m // tm, n // tn),
        in_specs=in_specs,
        out_specs=pl.BlockSpec((tm, tn), lambda i, j: (i, j)),
        out_shape=jax.ShapeDtypeStruct((m, n), out_dtype),
        compiler_params=_cparams(("parallel", "parallel")),
        name=name,
    )(*args)


def _swiglu_kernel(a_ref, wg_ref, wu_ref, o_ref, wgb_ref, wub_ref):
    @pl.when(pl.program_id(1) == 0)
    def _():
        wgb_ref[...] = wg_ref[...].astype(BF16)
        wub_ref[...] = wu_ref[...].astype(BF16)

    a = a_ref[...]
    g = jnp.dot(a, wgb_ref[...], preferred_element_type=F32)
    u = jnp.dot(a, wub_ref[...], preferred_element_type=F32)
    o_ref[...] = (g * jax.nn.sigmoid(g) * u).astype(o_ref.dtype)


def _swiglu_in(u, w_in, l, j):
    tm, tn = 1024, 256
    nb = D_FF // tn
    return pl.pallas_call(
        _swiglu_kernel,
        grid=(nb, ROWS // tm),
        in_specs=[pl.BlockSpec((tm, D), lambda n, m: (m, 0)),
                  pl.BlockSpec((None, None, D, tn), lambda n, m: (l, j, 0, n)),
                  pl.BlockSpec((None, None, D, tn), lambda n, m: (l, j, 0, nb + n))],
        out_specs=pl.BlockSpec((tm, tn), lambda n, m: (m, n)),
        out_shape=jax.ShapeDtypeStruct((ROWS, D_FF), BF16),
        scratch_shapes=[pltpu.VMEM((D, tn), BF16), pltpu.VMEM((D, tn), BF16)],
        compiler_params=_cparams(("arbitrary", "arbitrary")),
        name="swiglu_in",
    )(u, w_in, w_in)


def _dft_tables(n):
    t = np.arange(n)
    ang = 2.0 * np.pi * ((t[:, None] * t[None, :]) % n) / n
    return np.cos(ang).astype(np.float32), np.sin(ang).astype(np.float32)


def _even_tables():
    icnt = np.zeros((2, 4, RB, 1), np.float32)
    for kind, n in enumerate((PROMPT_LEN, SAMPLE_LEN)):
        pos = np.arange(RB) % n
        for gi, w in enumerate(POOL_WINDOWS):
            icnt[kind, gi, :, 0] = 1.0 / (np.clip(pos + w // 2, 0, n) - np.clip(pos - w // 2, 0, n))
    pos_tables = []
    for n in (PROMPT_LEN, SAMPLE_LEN):
        c, s = _dft_tables(n)
        pos_tables += [c * np.float32((n * GW) ** -0.5), s * np.float32((n * GW) ** -0.5)]
    return (icnt,) + tuple(pos_tables) + _dft_tables(GW)


def _window_sum(x, half, seq_len):
    n = x.shape[0]
    pos = lax.broadcasted_iota(jnp.int32, x.shape, 0) % seq_len
    down = lambda y, k: jnp.where(pos >= k, pltpu.roll(y, k, 0), 0.0)
    up = lambda y, k: jnp.where(pos < seq_len - k, pltpu.roll(y, n - k, 0), 0.0)
    trail = lead = x
    k = 1
    while k < half:
        trail = trail + down(trail, k)
        lead = lead + up(lead, k)
        k *= 2
    return down(trail, 1) + lead


def _even_mid_kernel(p_ref, icnt_ref, pw_ref, ps_ref, cnp_ref, snp_ref, cns_ref, sns_ref, cc_ref, sc_ref, o_ref):
    j = pl.program_id(1)
    is_prompt = pl.program_id(0) * RB < PROMPT_ROWS

    def pool(half, seq_len):
        x = p_ref[...]
        pooled = _window_sum(x, half, seq_len) * icnt_ref[...] - x
        y = jnp.dot(pooled.astype(BF16), pw_ref[...], preferred_element_type=F32)
        o_ref[...] = (y * ps_ref[...]).astype(o_ref.dtype)

    def fourier(cn_ref, sn_ref, seq_len):
        x = p_ref[...].astype(BF16)
        xc = jnp.dot(x, cc_ref[...], preferred_element_type=F32).astype(BF16)
        xs = jnp.dot(x, sc_ref[...], preferred_element_type=F32).astype(BF16)
        for s in range(RB // seq_len):
            rs = slice(s * seq_len, (s + 1) * seq_len)
            y = (jnp.dot(cn_ref[...], xc[rs], preferred_element_type=F32)
                 - jnp.dot(sn_ref[...], xs[rs], preferred_element_type=F32))
            o_ref[rs, :] = y.astype(o_ref.dtype)

    for gi, w in enumerate(POOL_WINDOWS):
        pl.when((j == gi) & is_prompt)(functools.partial(pool, w // 2, PROMPT_LEN))
        pl.when((j == gi) & jnp.logical_not(is_prompt))(functools.partial(pool, w // 2, SAMPLE_LEN))
    pl.when((j >= 4) & is_prompt)(functools.partial(fourier, cnp_ref, snp_ref, PROMPT_LEN))
    pl.when((j >= 4) & jnp.logical_not(is_prompt))(functools.partial(fourier, cns_ref, sns_ref, SAMPLE_LEN))


def _even_mid(p, pool_w, pool_scale):
    icnt, *dft = _even_tables()
    cnp, snp, cns, sns, cc, sc = (jnp.asarray(t).astype(BF16) for t in dft)
    kind = lambda i: jnp.where(i * RB < PROMPT_ROWS, 0, 1)
    grp = lambda j: jnp.minimum(j, 3)
    whole = lambda t: pl.BlockSpec(t.shape, lambda i, j: (0, 0))
    return pl.pallas_call(
        _even_mid_kernel,
        grid=(ROWS // RB, 8),
        in_specs=[pl.BlockSpec((RB, GW), lambda i, j: (i, j)),
                  pl.BlockSpec((None, None, RB, 1), lambda i, j: (kind(i), grp(j), 0, 0)),
                  pl.BlockSpec((None, GW, GW), lambda i, j: (grp(j), 0, 0)),
                  pl.BlockSpec((None, 1, GW), lambda i, j: (grp(j), 0, 0)),
                  whole(cnp), whole(snp), whole(cns), whole(sns), whole(cc), whole(sc)],
        out_specs=pl.BlockSpec((RB, GW), lambda i, j: (i, j)),
        out_shape=jax.ShapeDtypeStruct((ROWS, D), BF16),
        compiler_params=_cparams(("parallel", "arbitrary")),
        name="even_mid",
    )(p, jnp.asarray(icnt), pool_w, pool_scale, cnp, snp, cns, sns, cc, sc)


def _head_sum_matrix(width, value):
    h = np.arange(width) // HEAD
    return jnp.asarray((h[:, None] == h[None, :]).astype(np.float32) * value).astype(BF16)


def _split_bf16(x, n):
    parts = []
    for _ in range(n - 1):
        h = x.astype(BF16)
        parts.append(h)
        x = x - h.astype(F32)
    parts.append(x.astype(BF16))
    return parts


def _head_sum(x, e):
    hi, lo = _split_bf16(x, 2)
    return jnp.dot(hi, e, preferred_element_type=F32) + jnp.dot(lo, e, preferred_element_type=F32)


def _token_shift(x, mu, seq_len):
    n = x.shape[0]
    pos = lax.broadcasted_iota(jnp.int32, x.shape, 0) % seq_len
    prev = jnp.where(pos == 0, 0.0, pltpu.roll(x, 1, 0))
    nxt = jnp.where(pos == seq_len - 1, 0.0, pltpu.roll(x, n - 1, 0))
    return x + (0.5 * (prev + nxt) - x) * mu


def _rwkv_prep_kernel(pr_ref, pk_ref, pv_ref, pg_ref, pw_ref, pa_ref,
                      mr_ref, mk_ref, mv_ref, mg_ref, mw_ref, ma_ref,
                      w0_ref, w2_ref, a0_ref, aw2_ref, gw2_ref, kk_ref, ka_ref, e_ref,
                      r_o, k_o, v_o, kk_o, kka_o, lw_o, g_o, sg_ref, tw_ref, al_ref):
    i = pl.program_id(0)

    def body(seq_len):
        @pl.when(pl.program_id(1) == 0)
        def _():
            sg_ref[...] = jax.nn.sigmoid(_token_shift(pg_ref[...], mg_ref[...], seq_len)).astype(BF16)
            tw_ref[...] = jnp.tanh(_token_shift(pw_ref[...], mw_ref[...], seq_len)).astype(BF16)
            al_ref[...] = _token_shift(pa_ref[...], ma_ref[...], seq_len).astype(BF16)

        r = _token_shift(pr_ref[...], mr_ref[...], seq_len)
        k = _token_shift(pk_ref[...], mk_ref[...], seq_len)
        v = _token_shift(pv_ref[...], mv_ref[...], seq_len)
        for d in range(2):
            x = w0_ref[d] + jnp.dot(tw_ref[:, d * DECAY_LORA:(d + 1) * DECAY_LORA], w2_ref[d],
                                    preferred_element_type=F32)
            z = -x
            softplus = jnp.maximum(z, 0.0) + jnp.log(1.0 + jnp.exp(-jnp.abs(z)))
            lw_o[d] = -jnp.exp(-softplus - 0.5)
        a = jax.nn.sigmoid(a0_ref[...] + jnp.dot(al_ref[...], aw2_ref[...], preferred_element_type=F32))
        g_o[...] = jnp.dot(sg_ref[...], gw2_ref[...], preferred_element_type=F32)
        kk = k * kk_ref[...]
        ss = _head_sum(kk * kk, e_ref[...])
        kk = kk * lax.rsqrt(ss + 1e-12)
        r_o[...] = r
        k_o[...] = k * (1.0 + (a - 1.0) * ka_ref[...])
        v_o[...] = v
        kk_o[...] = kk
        kka_o[...] = kk * a

    @pl.when(i * RB < PROMPT_ROWS)
    def _():
        body(PROMPT_LEN)

    @pl.when(i * RB >= PROMPT_ROWS)
    def _():
        body(SAMPLE_LEN)


def _rwkv_prep(p, mu, w0, w2, a0, aw2, gw2, k_k, k_a):
    cw = 256
    ncb = RWKV_W // cw
    e = _head_sum_matrix(cw, 1.0)
    pcol = lambda base, width: pl.BlockSpec((RB, width), lambda i, c: (i, base // width))
    pch = lambda base: pl.BlockSpec((RB, cw), lambda i, c: (i, base // cw + c))
    mcol = lambda base, width: pl.BlockSpec((1, width), lambda i, c: (0, base // width))
    mch = lambda base: pl.BlockSpec((1, cw), lambda i, c: (0, base // cw + c))
    vch = pl.BlockSpec((1, cw), lambda i, c: (0, c))
    out = pl.BlockSpec((RB, cw), lambda i, c: (i, c))
    sds = jax.ShapeDtypeStruct((ROWS, RWKV_W), F32)
    return pl.pallas_call(
        _rwkv_prep_kernel,
        grid=(ROWS // RB, ncb),
        in_specs=[pch(C_R), pch(C_K), pch(C_V), pcol(C_G, GATE_PAD), pcol(C_W, 2 * DECAY_LORA), pcol(C_A, ICLR_LORA),
                  mch(C_R), mch(C_K), mch(C_V), mcol(C_G, GATE_PAD), mcol(C_W, 2 * DECAY_LORA), mcol(C_A, ICLR_LORA),
                  pl.BlockSpec((2, 1, cw), lambda i, c: (0, 0, c)),
                  pl.BlockSpec((2, DECAY_LORA, cw), lambda i, c: (0, 0, c)),
                  vch,
                  pl.BlockSpec((ICLR_LORA, cw), lambda i, c: (0, c)),
                  pl.BlockSpec((GATE_PAD, cw), lambda i, c: (0, c)),
                  vch, vch,
                  pl.BlockSpec((cw, cw), lambda i, c: (0, 0))],
        out_specs=[out, out, out, out, out,
                   pl.BlockSpec((2, RB, cw), lambda i, c: (0, i, c)), out],
        out_shape=[sds, sds, sds, sds, sds, jax.ShapeDtypeStruct((2, ROWS, RWKV_W), F32), sds],
        scratch_shapes=[pltpu.VMEM((RB, GATE_PAD), BF16), pltpu.VMEM((RB, 2 * DECAY_LORA), BF16),
                        pltpu.VMEM((RB, ICLR_LORA), BF16)],
        compiler_params=_cparams(("parallel", "arbitrary")),
        name="rwkv_prep",
    )(p, p, p, p, p, p, mu, mu, mu, mu, mu, mu, w0, w2, a0, aw2, gw2, k_k, k_a, e)


SCAN_PASSES = {"scores": 1, "ws": 1, "mv": 1, "inv": 1, "u": 1, "y": 1, "state": 1}


def _mmp(a, b, passes, nt=False):
    dims = (((1,), (1 if nt else 0,)), ((), ()))
    dot = lambda x, y: lax.dot_general(x, y, dims, preferred_element_type=F32)
    if passes == 1:
        return dot(a.astype(BF16), b.astype(BF16))
    ah, al = _split_bf16(a, 2)
    bh, bl = _split_bf16(b, 2)
    return dot(ah, bh) + (dot(al, bh) + dot(ah, bl))


def _rwkv_scan_kernel(*refs, has_init, pairs):
    in_refs = (refs[0:6], refs[6:12])
    s0_ref = refs[12] if has_init else None
    y_refs = refs[-4:-2]
    s_ref, sp_ref = refs[-2:]
    c_idx = pl.program_id(1)
    n_chunks = pl.num_programs(1)
    L = CHUNK
    W2 = 2 * HEAD
    P = SCAN_PASSES

    @pl.when(c_idx == 0)
    def _():
        sp_ref[...] = jnp.zeros_like(sp_ref)
        if has_init:
            for d in range(2):
                for p in range(pairs):
                    sp_ref[d * pairs + p, 0:HEAD, 0:HEAD] = s0_ref[d, 2 * p]
                    sp_ref[d * pairs + p, HEAD:W2, HEAD:W2] = s0_ref[d, 2 * p + 1]

    row = lax.broadcasted_iota(jnp.int32, (L, W2), 0)
    lane = lax.broadcasted_iota(jnp.int32, (L, W2), 1)
    col = lane % HEAD
    first = lane < HEAD
    incl_d = (col <= row, col >= row)
    strict_d = (col < row, col > row)
    eye = (col == row).astype(F32)
    blk8 = (row // 8) == (col // 8)
    off = [((row // (2 * b)) == (col // (2 * b))) & ((row // b) != (col // b)) for b in (8, 16, 32)]
    row_s = lax.broadcasted_iota(jnp.int32, (W2, W2), 0)
    lane_s = lax.broadcasted_iota(jnp.int32, (W2, W2), 1)
    diag_blocks = (row_s < HEAD) == (lane_s < HEAD)
    trow = lax.broadcasted_iota(jnp.int32, (L, L), 0)
    tcol = lax.broadcasted_iota(jnp.int32, (L, L), 1)
    tri_d = ((tcol <= trow).astype(BF16), (tcol >= trow).astype(BF16))

    def bd(x):
        return jnp.concatenate([jnp.where(first, x, 0.0), jnp.where(first, 0.0, x)], axis=0)

    ps = [slice(p * W2, (p + 1) * W2) for p in range(pairs)]
    each = lambda f, *xs: [f(*z) for z in zip(*xs)]
    ar, bt, kt, v, g_last = [], [], [], [], []
    for d in range(2):
        lw_ref, r_ref, k_ref, v_ref, a_ref, b_ref = in_refs[d]
        lw = lw_ref[...]
        c = sum(jnp.dot(tri_d[d], t, preferred_element_type=F32) for t in _split_bf16(lw, 3))
        g_in = jnp.exp(c)
        g_inv = jnp.exp(-c)
        at_all = a_ref[...] * jnp.exp(c - lw)
        rt_all = r_ref[...] * g_in
        bt_all = b_ref[...] * g_inv
        kt_all = k_ref[...] * g_inv
        v_all = v_ref[...]
        g_end = g_in[0:1, :] if d else g_in[L - 1:L, :]
        ar += [jnp.concatenate([at_all[:, s], rt_all[:, s]], axis=0) for s in ps]
        bt += [bt_all[:, s] for s in ps]
        kt += [kt_all[:, s] for s in ps]
        v += [v_all[:, s] for s in ps]
        g_last += [g_end[:, s] for s in ps]
    n_inst = 2 * pairs
    strict = [strict_d[i // pairs] for i in range(n_inst)]
    incl = [incl_d[i // pairs] for i in range(n_inst)]
    s0 = [sp_ref[i] for i in range(n_inst)]
    npq = each(lambda x, y: _mmp(x, bd(y), P["scores"], nt=True), ar, bt)
    mq = each(lambda x, y: _mmp(x, bd(y), P["scores"], nt=True), ar, kt)
    ws = each(lambda x, y: _mmp(x, y, P["ws"], nt=True), ar, s0)
    n_mat = each(lambda m, x: jnp.where(m, x[:L], 0.0), strict, npq)
    m_mat = each(lambda m, x: jnp.where(m, x[:L], 0.0), strict, mq)
    p_mat = each(lambda m, x: jnp.where(m, x[L:], 0.0), incl, npq)
    q_mat = each(lambda m, x: jnp.where(m, x[L:], 0.0), incl, mq)
    w = each(lambda x, m, y: x[:L] + _mmp(m, bd(y), P["mv"]), ws, m_mat, v)
    n8 = [jnp.where(blk8, x, 0.0) for x in n_mat]
    n8_2 = each(lambda x: _mmp(x, bd(x), P["inv"]), n8)
    n8_4 = each(lambda x: _mmp(x, bd(x), P["inv"]), n8_2)
    t = each(lambda x, y: _mmp(eye - x, bd(eye + y), P["inv"]), n8, n8_2)
    t = each(lambda x, y: _mmp(x, bd(eye + y), P["inv"]), t, n8_4)
    for m in off:
        x = each(lambda n, y: _mmp(jnp.where(m, n, 0.0), bd(y), P["inv"]), n_mat, t)
        t = each(lambda y, z: y - _mmp(y, bd(z), P["inv"]), t, x)
    u = each(lambda x, y: -_mmp(x, bd(y), P["u"]), t, w)
    y = each(lambda x, pm, qm, uu, vv: x[L:] + _mmp(jnp.concatenate([pm, qm], axis=1),
                                                     jnp.concatenate([bd(uu), bd(vv)], axis=0), P["y"]),
             ws, p_mat, q_mat, u, v)
    for i in range(n_inst):
        y_refs[i // pairs][:, ps[i % pairs]] = y[i]
    upd = each(lambda uu, vv, b, k: _mmp(jnp.concatenate([uu, vv], axis=0).T,
                                         jnp.concatenate([b, k], axis=0), P["state"]), u, v, bt, kt)
    s_new = each(lambda s, x, g: (s + jnp.where(diag_blocks, x, 0.0)) * g, s0, upd, g_last)
    for i in range(n_inst):
        sp_ref[i] = s_new[i]

    @pl.when(c_idx == n_chunks - 1)
    def _():
        for i in range(n_inst):
            d, p = divmod(i, pairs)
            s_ref[d, 2 * p] = s_new[i][0:HEAD, 0:HEAD]
            s_ref[d, 2 * p + 1] = s_new[i][HEAD:W2, HEAD:W2]


def _rwkv_scan(lw, r, k, v, a, b, s0, y_prev, *, n_seq, seq_len, row0):
    pairs = N_HEADS // 2
    nc = seq_len // CHUNK

    def rows(d):
        return lambda s, c: row0 // CHUNK + s * nc + (nc - 1 - c if d else c)

    in_specs, args = [], []
    for d in range(2):
        in_specs.append(pl.BlockSpec((None, CHUNK, RWKV_W), lambda s, c, d=d: (d, rows(d)(s, c), 0)))
        in_specs += [pl.BlockSpec((CHUNK, RWKV_W), lambda s, c, d=d: (rows(d)(s, c), 0))] * 5
        args += [lw, r, k, v, a, b]
    if s0 is not None:
        in_specs.append(pl.BlockSpec((None, 2, N_HEADS, HEAD, HEAD), lambda s, c: (s, 0, 0, 0, 0)))
        args.append(s0)
    aliases = {}
    if y_prev is not None:
        aliases = {len(args): 0, len(args) + 1: 1}
        in_specs += [pl.BlockSpec(memory_space=pl.ANY)] * 2
        args += list(y_prev)
    y_sds = jax.ShapeDtypeStruct((ROWS, RWKV_W), F32)
    return pl.pallas_call(
        functools.partial(_rwkv_scan_kernel, has_init=s0 is not None, pairs=pairs),
        grid=(n_seq, nc),
        in_specs=in_specs,
        out_specs=[pl.BlockSpec((CHUNK, RWKV_W), lambda s, c: (rows(0)(s, c), 0)),
                   pl.BlockSpec((CHUNK, RWKV_W), lambda s, c: (rows(1)(s, c), 0)),
                   pl.BlockSpec((None, 2, N_HEADS, HEAD, HEAD), lambda s, c: (s, 0, 0, 0, 0))],
        out_shape=[y_sds, y_sds, jax.ShapeDtypeStruct((n_seq, 2, N_HEADS, HEAD, HEAD), F32)],
        scratch_shapes=[pltpu.VMEM((2 * pairs, 2 * HEAD, 2 * HEAD), F32)],
        input_output_aliases=aliases,
        compiler_params=_cparams(("parallel", "arbitrary")),
        name="rwkv_scan",
    )(*args)


def _rwkv_post_kernel(yf_ref, yb_ref, r_ref, k_ref, v_ref, g_ref, gg_ref, gb_ref, rk_ref, e_ref, o_ref):
    y = yf_ref[...] + yb_ref[...]
    e = e_ref[...]
    mu = _head_sum(y, e)
    yc = y - mu
    var = _head_sum(yc * yc, e)
    yn = yc * lax.rsqrt(var + GN_EPS) * gg_ref[...] + gb_ref[...]
    bonus = _head_sum(r_ref[...] * k_ref[...] * rk_ref[...], e) * float(HEAD) * v_ref[...]
    o_ref[...] = ((yn + bonus) * g_ref[...]).astype(o_ref.dtype)


def _rwkv_post(yf, yb, r, k, v, g, gn_g, gn_b, r_k):
    tr, cw = 512, 512
    e = _head_sum_matrix(cw, 1.0 / HEAD)
    blk = pl.BlockSpec((tr, cw), lambda i, c: (i, c))
    vec = pl.BlockSpec((1, cw), lambda i, c: (0, c))
    return pl.pallas_call(
        _rwkv_post_kernel,
        grid=(ROWS // tr, RWKV_W // cw),
        in_specs=[blk, blk, blk, blk, blk, blk, vec, vec, vec, pl.BlockSpec((cw, cw), lambda i, c: (0, 0))],
        out_specs=blk,
        out_shape=jax.ShapeDtypeStruct((ROWS, RWKV_W), BF16),
        compiler_params=_cparams(("parallel", "parallel")),
        name="rwkv_post",
    )(yf, yb, r, k, v, g, gn_g, gn_b, r_k, e)


def _rms(x, gain):
    return x * lax.rsqrt(jnp.mean(x * x, axis=-1, keepdims=True) + 1e-6) * gain


def _rope(x, cos, sin_signed):
    lane = lax.broadcasted_iota(jnp.int32, x.shape, 1)
    quarter = HEAD_DIM // 4
    partner = jnp.where(lane % (2 * quarter) < quarter,
                        pltpu.roll(x, HEAD_DIM - quarter, 1), pltpu.roll(x, quarter, 1))
    return x * cos + partner * sin_signed


def _rope_tables():
    n = SAMPLE_LEN
    quarter = HEAD_DIM // 4
    row = (np.arange(n) // GRID_W).astype(np.float64)
    col = (np.arange(n) % GRID_W).astype(np.float64)
    inv_freq = ROPE_THETA ** (-np.arange(quarter, dtype=np.float64) / quarter)
    ar = row[:, None] * inv_freq[None, :]
    ac = col[:, None] * inv_freq[None, :]
    cos = np.concatenate([np.cos(ar), np.cos(ar), np.cos(ac), np.cos(ac)], axis=1)
    sin = np.concatenate([-np.sin(ar), np.sin(ar), -np.sin(ac), np.sin(ac)], axis=1)
    return cos.astype(np.float32), sin.astype(np.float32)


def _attn_prompt_kernel(q_ref, k_ref, v_ref, qn_ref, kn_ref, o_ref, ko_ref, vo_ref):
    scale = HEAD_DIM ** -0.5
    kn = _rms(k_ref[...], kn_ref[...])
    v = v_ref[...]
    ko_ref[...] = kn
    vo_ref[...] = v
    kb = kn.astype(BF16)
    vb = v.astype(BF16)
    for j in range(ATT_W // KV_W):
        hs = slice(j * HEAD_DIM, (j + 1) * HEAD_DIM)
        q = (_rms(q_ref[:, hs], qn_ref[...]) * scale).astype(BF16)
        s = lax.dot_general(q, kb, (((1,), (1,)), ((), ())), preferred_element_type=F32)
        e = jnp.exp(s - jnp.max(s, axis=-1, keepdims=True))
        inv = 1.0 / jnp.sum(e, axis=-1, keepdims=True)
        o_ref[:, hs] = (jnp.dot(e.astype(BF16), vb, preferred_element_type=F32) * inv).astype(o_ref.dtype)


def _attn_prompt(p, q_norm, k_norm):
    n = PROMPT_LEN
    qw = ATT_W // N_KV
    return pl.pallas_call(
        _attn_prompt_kernel,
        grid=(N_PROMPT_SEQ, N_KV),
        in_specs=[pl.BlockSpec((n, qw), lambda s, g: (s, C_Q // qw + g)),
                  pl.BlockSpec((n, HEAD_DIM), lambda s, g: (s, C_AK // HEAD_DIM + g)),
                  pl.BlockSpec((n, HEAD_DIM), lambda s, g: (s, C_AV // HEAD_DIM + g)),
                  pl.BlockSpec((1, HEAD_DIM), lambda s, g: (0, 0)),
                  pl.BlockSpec((1, HEAD_DIM), lambda s, g: (0, 0))],
        out_specs=[pl.BlockSpec((n, qw), lambda s, g: (s, g)),
                   pl.BlockSpec((n, HEAD_DIM), lambda s, g: (s, g)),
                   pl.BlockSpec((n, HEAD_DIM), lambda s, g: (s, g))],
        out_shape=[jax.ShapeDtypeStruct((ROWS, ATT_W), BF16),
                   jax.ShapeDtypeStruct((PROMPT_ROWS, KV_W), F32),
                   jax.ShapeDtypeStruct((PROMPT_ROWS, KV_W), F32)],
        compiler_params=_cparams(("parallel", "parallel")),
        name="attn_prompt",
    )(p, p, p, q_norm, k_norm)


def _attn_sample_kernel(q_ref, k_ref, v_ref, ck_ref, cv_ref, qn_ref, kn_ref, cos_ref, sin_ref, _o_prev, o_ref):
    scale = HEAD_DIM ** -0.5
    tq = 256
    cos = cos_ref[...]
    sin = sin_ref[...]
    ks = _rope(_rms(k_ref[...], kn_ref[...]), cos, sin).astype(BF16)
    vs = v_ref[...].astype(BF16)
    kc = ck_ref[...].astype(BF16)
    vc = cv_ref[...].astype(BF16)
    nt = (((1,), (1,)), ((), ()))
    for j in range(ATT_W // KV_W):
        hs = slice(j * HEAD_DIM, (j + 1) * HEAD_DIM)
        for qb in range(SAMPLE_LEN // tq):
            rs = slice(qb * tq, (qb + 1) * tq)
            q = (_rope(_rms(q_ref[rs, hs], qn_ref[...]), cos[rs], sin[rs]) * scale).astype(BF16)
            s1 = lax.dot_general(q, kc, nt, preferred_element_type=F32)
            s2 = lax.dot_general(q, ks, nt, preferred_element_type=F32)
            m = jnp.maximum(jnp.max(s1, axis=-1, keepdims=True), jnp.max(s2, axis=-1, keepdims=True))
            e1 = jnp.exp(s1 - m)
            e2 = jnp.exp(s2 - m)
            inv = 1.0 / (jnp.sum(e1, axis=-1, keepdims=True) + jnp.sum(e2, axis=-1, keepdims=True))
            o = (jnp.dot(e1.astype(BF16), vc, preferred_element_type=F32)
                 + jnp.dot(e2.astype(BF16), vs, preferred_element_type=F32))
            o_ref[rs, hs] = (o * inv).astype(o_ref.dtype)


def _attn_sample(p, cache_k, cache_v, q_norm, k_norm, o_prev):
    n = SAMPLE_LEN
    qw = ATT_W // N_KV
    r0 = PROMPT_ROWS // n
    cos, sin = _rope_tables()
    return pl.pallas_call(
        _attn_sample_kernel,
        grid=(N_SAMPLE_SEQ, N_KV),
        in_specs=[pl.BlockSpec((n, qw), lambda s, g: (r0 + s, C_Q // qw + g)),
                  pl.BlockSpec((n, HEAD_DIM), lambda s, g: (r0 + s, C_AK // HEAD_DIM + g)),
                  pl.BlockSpec((n, HEAD_DIM), lambda s, g: (r0 + s, C_AV // HEAD_DIM + g)),
                  pl.BlockSpec((PAST_LEN, HEAD_DIM), lambda s, g: (s, g)),
                  pl.BlockSpec((PAST_LEN, HEAD_DIM), lambda s, g: (s, g)),
                  pl.BlockSpec((1, HEAD_DIM), lambda s, g: (0, 0)),
                  pl.BlockSpec((1, HEAD_DIM), lambda s, g: (0, 0)),
                  pl.BlockSpec((n, HEAD_DIM), lambda s, g: (0, 0)),
                  pl.BlockSpec((n, HEAD_DIM), lambda s, g: (0, 0)),
                  pl.BlockSpec(memory_space=pl.ANY)],
        out_specs=pl.BlockSpec((n, qw), lambda s, g: (r0 + s, g)),
        out_shape=jax.ShapeDtypeStruct((ROWS, ATT_W), BF16),
        input_output_aliases={9: 0},
        compiler_params=_cparams(("parallel", "parallel")),
        name="attn_sample",
    )(p, p, p, cache_k, cache_v, q_norm, k_norm, jnp.asarray(cos), jnp.asarray(sin), o_prev)


_ODD_SEGMENTS = ((0, C_R, 6144), (7008, C_Q, ATT_W), (9056, C_AK, KV_W), (9568, C_AV, KV_W),
                 (6144, C_G, GATE_LORA), (6624, C_W, 2 * DECAY_LORA), (6880, C_A, ICLR_LORA))
_ODD_ZERO = ((C_G + GATE_LORA, GATE_PAD - GATE_LORA), (C_A + ICLR_LORA, ODD_COLS - C_A - ICLR_LORA))


def _pack_odd_kernel(w_ref, o_ref):
    for src, dst, width in _ODD_SEGMENTS:
        o_ref[dst:dst + width, :] = w_ref[src:src + width, :].astype(o_ref.dtype)
    for dst, width in _ODD_ZERO:
        o_ref[dst:dst + width, :] = jnp.zeros((width, o_ref.shape[1]), o_ref.dtype)


def _pack_odd_w_in(wt):
    tc = 256
    n, k = wt.shape
    return pl.pallas_call(
        _pack_odd_kernel,
        grid=(k // tc,),
        in_specs=[pl.BlockSpec((n, tc), lambda j: (0, j))],
        out_specs=pl.BlockSpec((ODD_COLS, tc), lambda j: (0, j)),
        out_shape=jax.ShapeDtypeStruct((ODD_COLS, k), BF16),
        compiler_params=_cparams(("parallel",)),
        name="pack_odd_w_in",
    )(wt)


def _pack_shift_mu(mu):
    z = lambda n: jnp.zeros((n,), mu.dtype)
    out = jnp.concatenate([mu[0:6144], z(C_G - C_Q), mu[6144:6144 + GATE_LORA], z(GATE_PAD - GATE_LORA),
                           mu[6624:6880], mu[6880:7008], z(ODD_COLS - C_A - ICLR_LORA)])
    return out[None, :]


def kernel(x_prompt, x_sample, cache_k, cache_v, state_rwkv, c, c_ctx, w_ada, b_ada, ln_g, ln_b, ffn_w_in,
           ffn_w_out, even_w_in, pool_w, pool_scale, even_w_out, odd_w_in, shift_mu, decay_w0, decay_w2,
           iclr_a0, iclr_w2, gate_w2, k_k, k_a, r_k, gn_g, gn_b, q_norm, k_norm, odd_w_out):
    x = (x_prompt.reshape(PROMPT_ROWS, D), x_sample.reshape(ROWS - PROMPT_ROWS, D))
    cond =jnp.concatenate([c_ctx[None, :], c, jnp.zeros((N_GROUPS - 1 - c.shape[0], D), F32)], axis=0)
    mods = _ada(cond, w_ada, b_ada[:, None, :])
    mods = mods.reshape(DEPTH, N_GROUPS, 3 * N_SUB, D).transpose(0, 2, 1, 3)[:, :, :, None, :]
    ln_g4 = ln_g[:, :, None, :]
    ln_b4 = ln_b[:, :, None, :]
    ffn_w_out_b = ffn_w_out.astype(BF16)

    u = _modulate(x, mods, 0, 0)
    new_k = new_v = new_s = None
    for l in range(DEPTH):
        i = l // 2
        h = _swiglu_in(u, ffn_w_in, l, 0)
        z = _mm([h], ffn_w_out_b, 512, 512, F32, "ffn_out", widx=(l, 0), resid=(x, mods, l, 0, 0.5))
        x, u = _ln(z, mods, ln_g4, ln_b4, l, 0, (l, 1))
        if l % 2 == 0:
            p = _mm([u], even_w_in[i].astype(BF16), 1024, 512, F32, "even_in")
            mid = _even_mid(p, pool_w[i].astype(BF16), pool_scale[i].reshape(4, 1, GW))
            z = _mm([mid], even_w_out[i].astype(BF16), 1024, 512, F32, "even_out", resid=(x, mods, l, 1, 1.0))
        else:
            p = _mm([u], _pack_odd_w_in(jnp.swapaxes(odd_w_in[i], 0, 1)), 1024, 512, F32, "odd_in",
                    w_rows_out=True)
            gw2 = jnp.concatenate([gate_w2[i], jnp.zeros((GATE_PAD - GATE_LORA, RWKV_W), F32)], axis=0)
            r, k, v, kk, kka, lw, g = _rwkv_prep(
                p, _pack_shift_mu(shift_mu[i]), decay_w0[i][:, None, :], decay_w2[i].astype(BF16),
                iclr_a0[i][None, :], iclr_w2[i].astype(BF16), gw2.astype(BF16), k_k[i][None, :], k_a[i][None, :])
            yf, yb, s_prompt = _rwkv_scan(lw, r, k, v, kk, kka, None, None,
                                          n_seq=N_PROMPT_SEQ, seq_len=PROMPT_LEN, row0=0)
            yf, yb, _ = _rwkv_scan(lw, r, k, v, kk, kka, state_rwkv[:, i], (yf, yb),
                                   n_seq=N_SAMPLE_SEQ, seq_len=SAMPLE_LEN, row0=PROMPT_ROWS)
            y_rwkv = _rwkv_post(yf, yb, r, k, v, g, gn_g[i][None, :], gn_b[i][None, :],
                                r_k[i].reshape(1, RWKV_W))
            o_p, nk, nv = _attn_prompt(p, q_norm[i][None, :], k_norm[i][None, :])
            o = _attn_sample(p, cache_k[:, i].reshape(N_SAMPLE_SEQ * PAST_LEN, KV_W),
                             cache_v[:, i].reshape(N_SAMPLE_SEQ * PAST_LEN, KV_W),
                             q_norm[i][None, :], k_norm[i][None, :], o_p)
            z = _mm([y_rwkv, o], odd_w_out[i].astype(BF16), 1024, 512, F32, "odd_out", resid=(x, mods, l, 1, 1.0))
            new_k = nk.reshape(N_PROMPT_SEQ, 1, PROMPT_LEN, N_KV, HEAD_DIM)
            new_v = nv.reshape(N_PROMPT_SEQ, 1, PROMPT_LEN, N_KV, HEAD_DIM)
            new_s = s_prompt[:, None]
        x, u = _ln(z, mods, ln_g4, ln_b4, l, 1, (l, 2))
        h = _swiglu_in(u, ffn_w_in, l, 1)
        z = _mm([h], ffn_w_out_b, 512, 512, F32, "ffn_out", widx=(l, 1), resid=(x, mods, l, 2, 0.5))
        if l + 1 < DEPTH:
            x, u = _ln(z, mods, ln_g4, ln_b4, l, 2, (l + 1, 0))
        else:
            y_prompt = _ln(z, mods, ln_g4, ln_b4, l, 2, None, 0, PROMPT_ROWS)
            y_sample = _ln(z, mods, ln_g4, ln_b4, l, 2, None, PROMPT_ROWS, ROWS - PROMPT_ROWS)

    return (y_prompt.reshape(N_PROMPT_SEQ, PROMPT_LEN, D), y_sample.reshape(N_SAMPLE_SEQ, SAMPLE_LEN, D),
            new_k, new_v, new_s)
```

```python
import functools

import numpy as np
import jax
import jax.numpy as jnp
from jax import lax
from jax.experimental import pallas as pl
from jax.experimental.pallas import tpu as pltpu

F32 = jnp.float32
BF16 = jnp.bfloat16

D = 4096
DEPTH = 2
N_SUB = 3
D_FF = 11008
N_PROMPT_SEQ = 16
PROMPT_LEN = 256
N_SAMPLE_SEQ = 4
SAMPLE_LEN = 1024
PAST_LEN = 512
GRID_W = 64
PROMPT_ROWS = N_PROMPT_SEQ * PROMPT_LEN
ROWS = PROMPT_ROWS + N_SAMPLE_SEQ * SAMPLE_LEN
RB = 1024
N_GROUPS = 8
POOL_WINDOWS = (2, 4, 8, 16)
GW = 512
RWKV_W = 2048
HEAD = 64
N_HEADS = 32
GATE_LORA = 480
GATE_PAD = 512
DECAY_LORA = 128
ICLR_LORA = 128
ATT_W = 2048
HEAD_DIM = 128
N_KV = 4
KV_W = N_KV * HEAD_DIM
ROPE_THETA = 10000.0
ALPHA = (2 * DEPTH) ** 0.25
LN_EPS = 1e-5
GN_EPS = 64e-5
CHUNK = 64

C_R, C_K, C_V, C_Q = 0, 2048, 4096, 6144
C_AK, C_AV, C_G, C_W, C_A = 8192, 8704, 9216, 9728, 9984
ODD_COLS = 10240

VMEM_LIMIT = 56 * 1024 * 1024


def _cparams(sem):
    return pltpu.CompilerParams(dimension_semantics=sem, vmem_limit_bytes=VMEM_LIMIT)


def _group_of_block(i, rows_per_block):
    r0 = i * rows_per_block
    return jnp.where(r0 < PROMPT_ROWS, 0, 1 + (r0 - PROMPT_ROWS) // SAMPLE_LEN)


def _ada_kernel(c_ref, w_ref, b_ref, o_ref):
    c = c_ref[...]
    s = (c * jax.nn.sigmoid(c)).astype(BF16)
    o_ref[...] = jnp.dot(s, w_ref[...].astype(BF16), preferred_element_type=F32) + b_ref[...]


def _ada(cond, w_ada, b_ada):
    tn = 512
    n = w_ada.shape[-1]
    return pl.pallas_call(
        _ada_kernel,
        grid=(DEPTH, n // tn),
        in_specs=[pl.BlockSpec((N_GROUPS, D), lambda l, j: (0, 0)),
                  pl.BlockSpec((None, D, tn), lambda l, j: (l, 0, j)),
                  pl.BlockSpec((None, 1, tn), lambda l, j: (l, 0, j))],
        out_specs=pl.BlockSpec((None, N_GROUPS, tn), lambda l, j: (l, 0, j)),
        out_shape=jax.ShapeDtypeStruct((DEPTH, N_GROUPS, n), F32),
        compiler_params=_cparams(("parallel", "parallel")),
        name="ada",
    )(cond, w_ada, b_ada)


def _mod_spec(tr, l, sub, kind):
    return pl.BlockSpec((None, None, None, 1, D),
                        lambda i: (l, sub * 3 + kind, _group_of_block(i, tr), 0, 0))


def _row_block_specs(x, tr, tn, rank):
    col = (lambda ij: ij[1]) if rank == 2 else (lambda ij: 0)
    if not isinstance(x, tuple):
        return [pl.BlockSpec((tr, tn), lambda *ij: (ij[0], col(ij)))], [x]
    nb = PROMPT_ROWS // tr
    return ([pl.BlockSpec((tr, tn), lambda *ij: (jnp.minimum(ij[0], nb - 1), col(ij))),
             pl.BlockSpec((tr, tn), lambda *ij: (jnp.maximum(ij[0] - nb, 0), col(ij)))], list(x))


def _pick_rows(x_refs, tr):
    if len(x_refs) == 1:
        return x_refs[0][...]
    return jnp.where(pl.program_id(0) * tr < PROMPT_ROWS, x_refs[0][...], x_refs[1][...])


def _modulate_kernel(*refs, n_x, tr):
    shift_ref, scale_ref, u_ref = refs[n_x:]
    x = _pick_rows(refs[:n_x], tr)
    u_ref[...] = (x * (1.0 + scale_ref[...]) + shift_ref[...]).astype(u_ref.dtype)


def _modulate(x, mods, l, sub):
    tr = 256
    x_specs, x_args = _row_block_specs(x, tr, D, 1)
    return pl.pallas_call(
        functools.partial(_modulate_kernel, n_x=len(x_args), tr=tr),
        grid=(ROWS // tr,),
        in_specs=x_specs + [_mod_spec(tr, l, sub, 0), _mod_spec(tr, l, sub, 1)],
        out_specs=pl.BlockSpec((tr, D), lambda i: (i, 0)),
        out_shape=jax.ShapeDtypeStruct((ROWS, D), BF16),
        compiler_params=_cparams(("parallel",)),
        name="modulate",
    )(*x_args, mods, mods)


def _ln_kernel(z_ref, g_ref, b_ref, *rest, emit_u):
    z = z_ref[...]
    mu = jnp.mean(z, axis=-1, keepdims=True)
    zc = z - mu
    var = jnp.mean(zc * zc, axis=-1, keepdims=True)
    xn = zc * lax.rsqrt(var + LN_EPS) * g_ref[...] + b_ref[...]
    if emit_u:
        shift_ref, scale_ref, xo_ref, u_ref = rest
        xo_ref[...] = xn
        u_ref[...] = (xn * (1.0 + scale_ref[...]) + shift_ref[...]).astype(u_ref.dtype)
    else:
        (xo_ref,) = rest
        xo_ref[...] = xn


def _ln(z, mods, ln_g, ln_b, l, sub, nxt, row0=0, n_rows=ROWS):
    tr = 256
    row = pl.BlockSpec((tr, D), lambda i: (i, 0))
    vec = pl.BlockSpec((None, None, 1, D), lambda i: (l, sub, 0, 0))
    in_specs = [pl.BlockSpec((tr, D), lambda i: (row0 // tr + i, 0)), vec, vec]
    args = [z, ln_g, ln_b]
    emit_u = nxt is not None
    if emit_u:
        assert row0 == 0 and n_rows == ROWS
        in_specs += [_mod_spec(tr, nxt[0], nxt[1], 0), _mod_spec(tr, nxt[0], nxt[1], 1)]
        args += [mods, mods]
        out_specs = [row, row]
        out_shape = [jax.ShapeDtypeStruct((ROWS, D), F32), jax.ShapeDtypeStruct((ROWS, D), BF16)]
    else:
        out_specs = row
        out_shape = jax.ShapeDtypeStruct((n_rows, D), F32)
    return pl.pallas_call(
        functools.partial(_ln_kernel, emit_u=emit_u),
        grid=(n_rows // tr,),
        in_specs=in_specs, out_specs=out_specs, out_shape=out_shape,
        compiler_params=_cparams(("parallel",)),
        name="ln",
    )(*args)


def _mm_kernel(*refs, n_a, n_x, tm, y_scale, w_rows_out):
    o_ref = refs[-1]
    dims = (((1,), (1 if w_rows_out else 0,)), ((), ()))
    dot = lambda a_ref, w_ref: lax.dot_general(a_ref[...], w_ref[...], dims, preferred_element_type=F32)
    acc = dot(refs[0], refs[n_a])
    for r in range(1, n_a):
        acc += dot(refs[r], refs[n_a + r])
    if n_x:
        gate_ref = refs[2 * n_a + n_x]
        acc = ALPHA * _pick_rows(refs[2 * n_a:2 * n_a + n_x], tm) + (y_scale * gate_ref[...]) * acc
    o_ref[...] = acc.astype(o_ref.dtype)


def _mm(a_parts, w, tm, tn, out_dtype, name, widx=(), resid=None, w_rows_out=False):
    m, k = a_parts[0].shape
    assert all(a.shape == (m, k) for a in a_parts)
    n = w.shape[-2] if w_rows_out else w.shape[-1]
    lead = (None,) * len(widx)
    in_specs = [pl.BlockSpec((tm, k), lambda i, j: (i, 0)) for _ in a_parts]
    if w_rows_out:
        in_specs += [pl.BlockSpec(lead + (tn, k), lambda i, j, r=r: widx + (j, r)) for r in range(len(a_parts))]
    else:
        in_specs += [pl.BlockSpec(lead + (k, tn), lambda i, j, r=r: widx + (r, j)) for r in range(len(a_parts))]
    args = list(a_parts) + [w] * len(a_parts)
    n_x, y_scale = 0, 1.0
    if resid is not None:
        x, mods, l, sub, y_scale = resid
        x_specs, x_args = _row_block_specs(x, tm, tn, 2)
        n_x = len(x_args)
        in_specs += x_specs + [pl.BlockSpec((None, None, None, 1, tn),
                                            lambda i, j: (l, sub * 3 + 2, _group_of_block(i, tm), 0, j))]
        args += x_args + [mods]
    return pl.pallas_call(
        functools.partial(_mm_kernel, n_a=len(a_parts), n_x=n_x, tm=tm, y_scale=y_scale, w_rows_out=w_rows_out),
        grid=(m // tm, n // tn),
        in_specs=in_specs,
        out_specs=pl.BlockSpec((tm, tn), lambda i, j: (i, j)),
        out_shape=jax.ShapeDtypeStruct((m, n), out_dtype),
        compiler_params=_cparams(("parallel", "parallel")),
        name=name,
    )(*args)


def _swiglu_kernel(*refs, n_cast):
    a_ref, wg_ref, wu_ref = refs[:3]
    cast_in = refs[3:3 + n_cast]
    o_ref = refs[3 + n_cast]
    cast_out = refs[4 + n_cast:4 + 2 * n_cast]
    wgb_ref, wub_ref = refs[-2:]

    @pl.when(pl.program_id(1) == 0)
    def _():
        wgb_ref[...] = wg_ref[...].astype(BF16)
        wub_ref[...] = wu_ref[...].astype(BF16)

    a = a_ref[...]
    g = jnp.dot(a, wgb_ref[...], preferred_element_type=F32)
    u = jnp.dot(a, wub_ref[...], preferred_element_type=F32)
    o_ref[...] = (g * jax.nn.sigmoid(g) * u).astype(o_ref.dtype)
    for src, dst in zip(cast_in, cast_out):
        dst[...] = src[...].astype(BF16)


def _swiglu_in(u, w_in, l, j, to_cast):
    tm, tn = 1024, 256
    nb = D_FF // tn
    mb = ROWS // tm
    wr = 32
    in_specs = [pl.BlockSpec((tm, D), lambda n, m: (m, 0)),
                pl.BlockSpec((None, None, D, tn), lambda n, m: (l, j, 0, n)),
                pl.BlockSpec((None, None, D, tn), lambda n, m: (l, j, 0, nb + n))]
    out_specs = [pl.BlockSpec((tm, tn), lambda n, m: (m, n))]
    out_shape = [jax.ShapeDtypeStruct((ROWS, D_FF), BF16)]
    for w, idx in to_cast:
        rows = w.shape[-2]
        assert w.shape[-1] == D and rows % wr == 0 and rows // wr <= nb * mb
        blk = lambda n, m, last=rows // wr - 1: jnp.minimum(n * mb + m, last)
        in_specs.append(pl.BlockSpec((None,) * len(idx) + (wr, D), lambda n, m, idx=idx, blk=blk: idx + (blk(n, m), 0)))
        out_specs.append(pl.BlockSpec((wr, D), lambda n, m, blk=blk: (blk(n, m), 0)))
        out_shape.append(jax.ShapeDtypeStruct((rows, D), BF16))
    return pl.pallas_call(
        functools.partial(_swiglu_kernel, n_cast=len(to_cast)),
        grid=(nb, mb),
        in_specs=in_specs, out_specs=out_specs, out_shape=out_shape,
        scratch_shapes=[pltpu.VMEM((D, tn), BF16), pltpu.VMEM((D, tn), BF16)],
        compiler_params=_cparams(("arbitrary", "arbitrary")),
        name="swiglu_in",
    )(u, w_in, w_in, *[w for w, _ in to_cast])


def _dft_tables(n):
    t = np.arange(n)
    ang = 2.0 * np.pi * ((t[:, None] * t[None, :]) % n) / n
    return np.cos(ang).astype(np.float32), np.sin(ang).astype(np.float32)


def _even_tables():
    icnt = np.zeros((2, 4, RB, 1), np.float32)
    for kind, n in enumerate((PROMPT_LEN, SAMPLE_LEN)):
        pos = np.arange(RB) % n
        for gi, w in enumerate(POOL_WINDOWS):
            icnt[kind, gi, :, 0] = 1.0 / (np.clip(pos + w // 2, 0, n) - np.clip(pos - w // 2, 0, n))
    pos_tables = []
    for n in (PROMPT_LEN, SAMPLE_LEN):
        c, s = _dft_tables(n)
        pos_tables += [c * np.float32((n * GW) ** -0.5), s * np.float32((n * GW) ** -0.5)]
    return (icnt,) + tuple(pos_tables) + _dft_tables(GW)


def _window_sum(x, half, seq_len):
    n = x.shape[0]
    pos = lax.broadcasted_iota(jnp.int32, x.shape, 0) % seq_len
    down = lambda y, k: jnp.where(pos >= k, pltpu.roll(y, k, 0), 0.0)
    up = lambda y, k: jnp.where(pos < seq_len - k, pltpu.roll(y, n - k, 0), 0.0)
    trail = lead = x
    k = 1
    while k < half:
        trail = trail + down(trail, k)
        lead = lead + up(lead, k)
        k *= 2
    return down(trail, 1) + lead


def _even_mid_kernel(p_ref, icnt_ref, pw_ref, ps_ref, cnp_ref, snp_ref, cns_ref, sns_ref, cc_ref, sc_ref, o_ref):
    j = pl.program_id(1)
    is_prompt = pl.program_id(0) * RB < PROMPT_ROWS

    def pool(half, seq_len):
        x = p_ref[...]
        pooled = _window_sum(x, half, seq_len) * icnt_ref[...] - x
        y = jnp.dot(pooled.astype(BF16), pw_ref[...], preferred_element_type=F32)
        o_ref[...] = (y * ps_ref[...]).astype(o_ref.dtype)

    def fourier(cn_ref, sn_ref, seq_len):
        x = p_ref[...].astype(BF16)
        xc = jnp.dot(x, cc_ref[...], preferred_element_type=F32).astype(BF16)
        xs = jnp.dot(x, sc_ref[...], preferred_element_type=F32).astype(BF16)
        for s in range(RB // seq_len):
            rs = slice(s * seq_len, (s + 1) * seq_len)
            y = (jnp.dot(cn_ref[...], xc[rs], preferred_element_type=F32)
                 - jnp.dot(sn_ref[...], xs[rs], preferred_element_type=F32))
            o_ref[rs, :] = y.astype(o_ref.dtype)

    for gi, w in enumerate(POOL_WINDOWS):
        pl.when((j == gi) & is_prompt)(functools.partial(pool, w // 2, PROMPT_LEN))
        pl.when((j == gi) & jnp.logical_not(is_prompt))(functools.partial(pool, w // 2, SAMPLE_LEN))
    pl.when((j >= 4) & is_prompt)(functools.partial(fourier, cnp_ref, snp_ref, PROMPT_LEN))
    pl.when((j >= 4) & jnp.logical_not(is_prompt))(functools.partial(fourier, cns_ref, sns_ref, SAMPLE_LEN))


def _even_mid(p, pool_w, pool_scale):
    icnt, *dft = _even_tables()
    cnp, snp, cns, sns, cc, sc = (jnp.asarray(t).astype(BF16) for t in dft)
    kind = lambda i: jnp.where(i * RB < PROMPT_ROWS, 0, 1)
    grp = lambda j: jnp.minimum(j, 3)
    whole = lambda t: pl.BlockSpec(t.shape, lambda i, j: (0, 0))
    return pl.pallas_call(
        _even_mid_kernel,
        grid=(ROWS // RB, 8),
        in_specs=[pl.BlockSpec((RB, GW), lambda i, j: (i, j)),
                  pl.BlockSpec((None, None, RB, 1), lambda i, j: (kind(i), grp(j), 0, 0)),
                  pl.BlockSpec((None, GW, GW), lambda i, j: (grp(j), 0, 0)),
                  pl.BlockSpec((None, 1, GW), lambda i, j: (grp(j), 0, 0)),
                  whole(cnp), whole(snp), whole(cns), whole(sns), whole(cc), whole(sc)],
        out_specs=pl.BlockSpec((RB, GW), lambda i, j: (i, j)),
        out_shape=jax.ShapeDtypeStruct((ROWS, D), BF16),
        compiler_params=_cparams(("parallel", "arbitrary")),
        name="even_mid",
    )(p, jnp.asarray(icnt), pool_w, pool_scale, cnp, snp, cns, sns, cc, sc)


def _head_sum_matrix(width, value):
    h = np.arange(width) // HEAD
    return jnp.asarray((h[:, None] == h[None, :]).astype(np.float32) * value).astype(BF16)


def _split_bf16(x, n):
    parts = []
    for _ in range(n - 1):
        h = x.astype(BF16)
        parts.append(h)
        x = x - h.astype(F32)
    parts.append(x.astype(BF16))
    return parts


def _head_sum(x, e):
    hi, lo = _split_bf16(x, 2)
    return jnp.dot(hi, e, preferred_element_type=F32) + jnp.dot(lo, e, preferred_element_type=F32)


def _token_shift(x, mu, seq_len):
    n = x.shape[0]
    pos = lax.broadcasted_iota(jnp.int32, x.shape, 0) % seq_len
    prev = jnp.where(pos == 0, 0.0, pltpu.roll(x, 1, 0))
    nxt = jnp.where(pos == seq_len - 1, 0.0, pltpu.roll(x, n - 1, 0))
    return x + (0.5 * (prev + nxt) - x) * mu


def _rwkv_prep_kernel(pr_ref, pk_ref, pv_ref, pg_ref, pw_ref, pa_ref,
                      mr_ref, mk_ref, mv_ref, mg_ref, mw_ref, ma_ref,
                      w0_ref, w2_ref, a0_ref, aw2_ref, gw2_ref, kk_ref, ka_ref, e_ref,
                      r_o, k_o, v_o, kk_o, kka_o, lw_o, g_o, sg_ref, tw_ref, al_ref):
    i = pl.program_id(0)

    def body(seq_len):
        @pl.when(pl.program_id(1) == 0)
        def _():
            sg_ref[...] = jax.nn.sigmoid(_token_shift(pg_ref[...], mg_ref[...], seq_len)).astype(BF16)
            tw_ref[...] = jnp.tanh(_token_shift(pw_ref[...], mw_ref[...], seq_len)).astype(BF16)
            al_ref[...] = _token_shift(pa_ref[...], ma_ref[...], seq_len).astype(BF16)

        r = _token_shift(pr_ref[...], mr_ref[...], seq_len)
        k = _token_shift(pk_ref[...], mk_ref[...], seq_len)
        v = _token_shift(pv_ref[...], mv_ref[...], seq_len)
        for d in range(2):
            x = w0_ref[d] + jnp.dot(tw_ref[:, d * DECAY_LORA:(d + 1) * DECAY_LORA], w2_ref[d],
                                    preferred_element_type=F32)
            z = -x
            softplus = jnp.maximum(z, 0.0) + jnp.log(1.0 + jnp.exp(-jnp.abs(z)))
            lw_o[d] = -jnp.exp(-softplus - 0.5)
        a = jax.nn.sigmoid(a0_ref[...] + jnp.dot(al_ref[...], aw2_ref[...], preferred_element_type=F32))
        g_o[...] = jnp.dot(sg_ref[...], gw2_ref[...], preferred_element_type=F32)
        kk = k * kk_ref[...]
        ss = _head_sum(kk * kk, e_ref[...])
        kk = kk * lax.rsqrt(ss + 1e-12)
        r_o[...] = r
        k_o[...] = k * (1.0 + (a - 1.0) * ka_ref[...])
        v_o[...] = v
        kk_o[...] = kk
        kka_o[...] = kk * a

    @pl.when(i * RB < PROMPT_ROWS)
    def _():
        body(PROMPT_LEN)

    @pl.when(i * RB >= PROMPT_ROWS)
    def _():
        body(SAMPLE_LEN)


def _rwkv_prep(p, mu, w0, w2, a0, aw2, gw2, k_k, k_a):
    cw = 256
    ncb = RWKV_W // cw
    e = _head_sum_matrix(cw, 1.0)
    pcol = lambda base, width: pl.BlockSpec((RB, width), lambda i, c: (i, base // width))
    pch = lambda base: pl.BlockSpec((RB, cw), lambda i, c: (i, base // cw + c))
    mcol = lambda base, width: pl.BlockSpec((1, width), lambda i, c: (0, base // width))
    mch = lambda base: pl.BlockSpec((1, cw), lambda i, c: (0, base // cw + c))
    vch = pl.BlockSpec((1, cw), lambda i, c: (0, c))
    out = pl.BlockSpec((RB, cw), lambda i, c: (i, c))
    sds = jax.ShapeDtypeStruct((ROWS, RWKV_W), F32)
    return pl.pallas_call(
        _rwkv_prep_kernel,
        grid=(ROWS // RB, ncb),
        in_specs=[pch(C_R), pch(C_K), pch(C_V), pcol(C_G, GATE_PAD), pcol(C_W, 2 * DECAY_LORA), pcol(C_A, ICLR_LORA),
                  mch(C_R), mch(C_K), mch(C_V), mcol(C_G, GATE_PAD), mcol(C_W, 2 * DECAY_LORA), mcol(C_A, ICLR_LORA),
                  pl.BlockSpec((2, 1, cw), lambda i, c: (0, 0, c)),
                  pl.BlockSpec((2, DECAY_LORA, cw), lambda i, c: (0, 0, c)),
                  vch,
                  pl.BlockSpec((ICLR_LORA, cw), lambda i, c: (0, c)),
                  pl.BlockSpec((GATE_PAD, cw), lambda i, c: (0, c)),
                  vch, vch,
                  pl.BlockSpec((cw, cw), lambda i, c: (0, 0))],
        out_specs=[out, out, out, out, out,
                   pl.BlockSpec((2, RB, cw), lambda i, c: (0, i, c)), out],
        out_shape=[sds, sds, sds, sds, sds, jax.ShapeDtypeStruct((2, ROWS, RWKV_W), F32), sds],
        scratch_shapes=[pltpu.VMEM((RB, GATE_PAD), BF16), pltpu.VMEM((RB, 2 * DECAY_LORA), BF16),
                        pltpu.VMEM((RB, ICLR_LORA), BF16)],
        compiler_params=_cparams(("parallel", "arbitrary")),
        name="rwkv_prep",
    )(p, p, p, p, p, p, mu, mu, mu, mu, mu, mu, w0, w2, a0, aw2, gw2, k_k, k_a, e)


SCAN_PASSES = {"scores": 1, "ws": 1, "mv": 1, "inv": 1, "u": 1, "y": 1, "state": 1}


def _mmp(a, b, passes, nt=False):
    dims = (((1,), (1 if nt else 0,)), ((), ()))
    dot = lambda x, y: lax.dot_general(x, y, dims, preferred_element_type=F32)
    if passes == 1:
        return dot(a.astype(BF16), b.astype(BF16))
    ah, al = _split_bf16(a, 2)
    bh, bl = _split_bf16(b, 2)
    return dot(ah, bh) + (dot(al, bh) + dot(ah, bl))


def _rwkv_scan_kernel(*refs, has_init, pairs):
    in_refs = (refs[0:6], refs[6:12])
    s0_ref = refs[12] if has_init else None
    y_refs = refs[-4:-2]
    s_ref, sp_ref = refs[-2:]
    c_idx = pl.program_id(1)
    n_chunks = pl.num_programs(1)
    L = CHUNK
    W2 = 2 * HEAD
    P = SCAN_PASSES

    @pl.when(c_idx == 0)
    def _():
        sp_ref[...] = jnp.zeros_like(sp_ref)
        if has_init:
            for d in range(2):
                for p in range(pairs):
                    sp_ref[d * pairs + p, 0:HEAD, 0:HEAD] = s0_ref[d, 2 * p]
                    sp_ref[d * pairs + p, HEAD:W2, HEAD:W2] = s0_ref[d, 2 * p + 1]

    row = lax.broadcasted_iota(jnp.int32, (L, W2), 0)
    lane = lax.broadcasted_iota(jnp.int32, (L, W2), 1)
    col = lane % HEAD
    first = lane < HEAD
    incl_d = (col <= row, col >= row)
    strict_d = (col < row, col > row)
    eye = (col == row).astype(F32)
    blk8 = (row // 8) == (col // 8)
    off = [((row // (2 * b)) == (col // (2 * b))) & ((row // b) != (col // b)) for b in (8, 16, 32)]
    row_s = lax.broadcasted_iota(jnp.int32, (W2, W2), 0)
    lane_s = lax.broadcasted_iota(jnp.int32, (W2, W2), 1)
    diag_blocks = (row_s < HEAD) == (lane_s < HEAD)
    trow = lax.broadcasted_iota(jnp.int32, (L, L), 0)
    tcol = lax.broadcasted_iota(jnp.int32, (L, L), 1)
    tri_d = ((tcol <= trow).astype(BF16), (tcol >= trow).astype(BF16))

    def bd(x):
        return jnp.concatenate([jnp.where(first, x, 0.0), jnp.where(first, 0.0, x)], axis=0)

    ps = [slice(p * W2, (p + 1) * W2) for p in range(pairs)]
    each = lambda f, *xs: [f(*z) for z in zip(*xs)]
    ar, bt, kt, v, g_last = [], [], [], [], []
    for d in range(2):
        lw_ref, r_ref, k_ref, v_ref, a_ref, b_ref = in_refs[d]
        lw = lw_ref[...]
        c = sum(jnp.dot(tri_d[d], t, preferred_element_type=F32) for t in _split_bf16(lw, 3))
        g_in = jnp.exp(c)
        g_inv = jnp.exp(-c)
        at_all = a_ref[...] * jnp.exp(c - lw)
        rt_all = r_ref[...] * g_in
        bt_all = b_ref[...] * g_inv
        kt_all = k_ref[...] * g_inv
        v_all = v_ref[...]
        g_end = g_in[0:1, :] if d else g_in[L - 1:L, :]
        ar += [jnp.concatenate([at_all[:, s], rt_all[:, s]], axis=0) for s in ps]
        bt += [bt_all[:, s] for s in ps]
        kt += [kt_all[:, s] for s in ps]
        v += [v_all[:, s] for s in ps]
        g_last += [g_end[:, s] for s in ps]
    n_inst = 2 * pairs
    strict = [strict_d[i // pairs] for i in range(n_inst)]
    incl = [incl_d[i // pairs] for i in range(n_inst)]
    s0 = [sp_ref[i] for i in range(n_inst)]
    npq = each(lambda x, y: _mmp(x, bd(y), P["scores"], nt=True), ar, bt)
    mq = each(lambda x, y: _mmp(x, bd(y), P["scores"], nt=True), ar, kt)
    ws = each(lambda x, y: _mmp(x, y, P["ws"], nt=True), ar, s0)
    n_mat = each(lambda m, x: jnp.where(m, x[:L], 0.0), strict, npq)
    m_mat = each(lambda m, x: jnp.where(m, x[:L], 0.0), strict, mq)
    p_mat = each(lambda m, x: jnp.where(m, x[L:], 0.0), incl, npq)
    q_mat = each(lambda m, x: jnp.where(m, x[L:], 0.0), incl, mq)
    w = each(lambda x, m, y: x[:L] + _mmp(m, bd(y), P["mv"]), ws, m_mat, v)
    n8 = [jnp.where(blk8, x, 0.0) for x in n_mat]
    n8_2 = each(lambda x: _mmp(x, bd(x), P["inv"]), n8)
    n8_4 = each(lambda x: _mmp(x, bd(x), P["inv"]), n8_2)
    t = each(lambda x, y: _mmp(eye - x, bd(eye + y), P["inv"]), n8, n8_2)
    t = each(lambda x, y: _mmp(x, bd(eye + y), P["inv"]), t, n8_4)
    for m in off:
        x = each(lambda n, y: _mmp(jnp.where(m, n, 0.0), bd(y), P["inv"]), n_mat, t)
        t = each(lambda y, z: y - _mmp(y, bd(z), P["inv"]), t, x)
    u = each(lambda x, y: -_mmp(x, bd(y), P["u"]), t, w)
    y = each(lambda x, pm, qm, uu, vv: x[L:] + _mmp(jnp.concatenate([pm, qm], axis=1),
                                                     jnp.concatenate([bd(uu), bd(vv)], axis=0), P["y"]),
             ws, p_mat, q_mat, u, v)
    for i in range(n_inst):
        y_refs[i // pairs][:, ps[i % pairs]] = y[i]
    upd = each(lambda uu, vv, b, k: _mmp(jnp.concatenate([uu, vv], axis=0).T,
                                         jnp.concatenate([b, k], axis=0), P["state"]), u, v, bt, kt)
    s_new = each(lambda s, x, g: (s + jnp.where(diag_blocks, x, 0.0)) * g, s0, upd, g_last)
    for i in range(n_inst):
        sp_ref[i] = s_new[i]

    @pl.when(c_idx == n_chunks - 1)
    def _():
        for i in range(n_inst):
            d, p = divmod(i, pairs)
            s_ref[d, 2 * p] = s_new[i][0:HEAD, 0:HEAD]
            s_ref[d, 2 * p + 1] = s_new[i][HEAD:W2, HEAD:W2]


def _rwkv_scan(lw, r, k, v, a, b, s0, y_prev, *, n_seq, seq_len, row0):
    pairs = N_HEADS // 2
    nc = seq_len // CHUNK

    def rows(d):
        return lambda s, c: row0 // CHUNK + s * nc + (nc - 1 - c if d else c)

    in_specs, args = [], []
    for d in range(2):
        in_specs.append(pl.BlockSpec((None, CHUNK, RWKV_W), lambda s, c, d=d: (d, rows(d)(s, c), 0)))
        in_specs += [pl.BlockSpec((CHUNK, RWKV_W), lambda s, c, d=d: (rows(d)(s, c), 0))] * 5
        args += [lw, r, k, v, a, b]
    if s0 is not None:
        in_specs.append(pl.BlockSpec((None, 2, N_HEADS, HEAD, HEAD), lambda s, c: (s, 0, 0, 0, 0)))
        args.append(s0)
    aliases = {}
    if y_prev is not None:
        aliases = {len(args): 0, len(args) + 1: 1}
        in_specs += [pl.BlockSpec(memory_space=pl.ANY)] * 2
        args += list(y_prev)
    y_sds = jax.ShapeDtypeStruct((ROWS, RWKV_W), F32)
    return pl.pallas_call(
        functools.partial(_rwkv_scan_kernel, has_init=s0 is not None, pairs=pairs),
        grid=(n_seq, nc),
        in_specs=in_specs,
        out_specs=[pl.BlockSpec((CHUNK, RWKV_W), lambda s, c: (rows(0)(s, c), 0)),
                   pl.BlockSpec((CHUNK, RWKV_W), lambda s, c: (rows(1)(s, c), 0)),
                   pl.BlockSpec((None, 2, N_HEADS, HEAD, HEAD), lambda s, c: (s, 0, 0, 0, 0))],
        out_shape=[y_sds, y_sds, jax.ShapeDtypeStruct((n_seq, 2, N_HEADS, HEAD, HEAD), F32)],
        scratch_shapes=[pltpu.VMEM((2 * pairs, 2 * HEAD, 2 * HEAD), F32)],
        input_output_aliases=aliases,
        compiler_params=_cparams(("parallel", "arbitrary")),
        name="rwkv_scan",
    )(*args)


def _rwkv_post_kernel(yf_ref, yb_ref, r_ref, k_ref, v_ref, g_ref, gg_ref, gb_ref, rk_ref, e_ref, o_ref):
    y = yf_ref[...] + yb_ref[...]
    e = e_ref[...]
    mu = _head_sum(y, e)
    yc = y - mu
    var = _head_sum(yc * yc, e)
    yn = yc * lax.rsqrt(var + GN_EPS) * gg_ref[...] + gb_ref[...]
    bonus = _head_sum(r_ref[...] * k_ref[...] * rk_ref[...], e) * float(HEAD) * v_ref[...]
    o_ref[...] = ((yn + bonus) * g_ref[...]).astype(o_ref.dtype)


def _rwkv_post(yf, yb, r, k, v, g, gn_g, gn_b, r_k):
    tr, cw = 512, 512
    e = _head_sum_matrix(cw, 1.0 / HEAD)
    blk = pl.BlockSpec((tr, cw), lambda i, c: (i, c))
    vec = pl.BlockSpec((1, cw), lambda i, c: (0, c))
    return pl.pallas_call(
        _rwkv_post_kernel,
        grid=(ROWS // tr, RWKV_W // cw),
        in_specs=[blk, blk, blk, blk, blk, blk, vec, vec, vec, pl.BlockSpec((cw, cw), lambda i, c: (0, 0))],
        out_specs=blk,
        out_shape=jax.ShapeDtypeStruct((ROWS, RWKV_W), BF16),
        compiler_params=_cparams(("parallel", "parallel")),
        name="rwkv_post",
    )(yf, yb, r, k, v, g, gn_g, gn_b, r_k, e)


def _rms(x, gain):
    return x * lax.rsqrt(jnp.mean(x * x, axis=-1, keepdims=True) + 1e-6) * gain


def _rope(x, cos, sin_signed):
    lane = lax.broadcasted_iota(jnp.int32, x.shape, 1)
    quarter = HEAD_DIM // 4
    partner = jnp.where(lane % (2 * quarter) < quarter,
                        pltpu.roll(x, HEAD_DIM - quarter, 1), pltpu.roll(x, quarter, 1))
    return x * cos + partner * sin_signed


def _rope_tables():
    n = SAMPLE_LEN
    quarter = HEAD_DIM // 4
    row = (np.arange(n) // GRID_W).astype(np.float64)
    col = (np.arange(n) % GRID_W).astype(np.float64)
    inv_freq = ROPE_THETA ** (-np.arange(quarter, dtype=np.float64) / quarter)
    ar = row[:, None] * inv_freq[None, :]
    ac = col[:, None] * inv_freq[None, :]
    cos = np.concatenate([np.cos(ar), np.cos(ar), np.cos(ac), np.cos(ac)], axis=1)
    sin = np.concatenate([-np.sin(ar), np.sin(ar), -np.sin(ac), np.sin(ac)], axis=1)
    return cos.astype(np.float32), sin.astype(np.float32)


def _attn_prompt_kernel(q_ref, k_ref, v_ref, qn_ref, kn_ref, o_ref, ko_ref, vo_ref):
    scale = HEAD_DIM ** -0.5
    kn = _rms(k_ref[...], kn_ref[...])
    v = v_ref[...]
    ko_ref[...] = kn
    vo_ref[...] = v
    kb = kn.astype(BF16)
    vb = v.astype(BF16)
    for j in range(ATT_W // KV_W):
        hs = slice(j * HEAD_DIM, (j + 1) * HEAD_DIM)
        q = (_rms(q_ref[:, hs], qn_ref[...]) * scale).astype(BF16)
        s = lax.dot_general(q, kb, (((1,), (1,)), ((), ())), preferred_element_type=F32)
        e = jnp.exp(s - jnp.max(s, axis=-1, keepdims=True))
        inv = 1.0 / jnp.sum(e, axis=-1, keepdims=True)
        o_ref[:, hs] = (jnp.dot(e.astype(BF16), vb, preferred_element_type=F32) * inv).astype(o_ref.dtype)


def _attn_prompt(p, q_norm, k_norm):
    n = PROMPT_LEN
    qw = ATT_W // N_KV
    return pl.pallas_call(
        _attn_prompt_kernel,
        grid=(N_PROMPT_SEQ, N_KV),
        in_specs=[pl.BlockSpec((n, qw), lambda s, g: (s, C_Q // qw + g)),
                  pl.BlockSpec((n, HEAD_DIM), lambda s, g: (s, C_AK // HEAD_DIM + g)),
                  pl.BlockSpec((n, HEAD_DIM), lambda s, g: (s, C_AV // HEAD_DIM + g)),
                  pl.BlockSpec((1, HEAD_DIM), lambda s, g: (0, 0)),
                  pl.BlockSpec((1, HEAD_DIM), lambda s, g: (0, 0))],
        out_specs=[pl.BlockSpec((n, qw), lambda s, g: (s, g)),
                   pl.BlockSpec((n, HEAD_DIM), lambda s, g: (s, g)),
                   pl.BlockSpec((n, HEAD_DIM), lambda s, g: (s, g))],
        out_shape=[jax.ShapeDtypeStruct((ROWS, ATT_W), BF16),
                   jax.ShapeDtypeStruct((PROMPT_ROWS, KV_W), F32),
                   jax.ShapeDtypeStruct((PROMPT_ROWS, KV_W), F32)],
        compiler_params=_cparams(("parallel", "parallel")),
        name="attn_prompt",
    )(p, p, p, q_norm, k_norm)


def _attn_sample_kernel(q_ref, k_ref, v_ref, ck_ref, cv_ref, qn_ref, kn_ref, cos_ref, sin_ref, _o_prev, o_ref):
    scale = HEAD_DIM ** -0.5
    tq = 256
    cos = cos_ref[...]
    sin = sin_ref[...]
    ks = _rope(_rms(k_ref[...], kn_ref[...]), cos, sin).astype(BF16)
    vs = v_ref[...].astype(BF16)
    kc = ck_ref[...].astype(BF16)
    vc = cv_ref[...].astype(BF16)
    nt = (((1,), (1,)), ((), ()))
    for j in range(ATT_W // KV_W):
        hs = slice(j * HEAD_DIM, (j + 1) * HEAD_DIM)
        for qb in range(SAMPLE_LEN // tq):
            rs = slice(qb * tq, (qb + 1) * tq)
            q = (_rope(_rms(q_ref[rs, hs], qn_ref[...]), cos[rs], sin[rs]) * scale).astype(BF16)
            s1 = lax.dot_general(q, kc, nt, preferred_element_type=F32)
            s2 = lax.dot_general(q, ks, nt, preferred_element_type=F32)
            m = jnp.maximum(jnp.max(s1, axis=-1, keepdims=True), jnp.max(s2, axis=-1, keepdims=True))
            e1 = jnp.exp(s1 - m)
            e2 = jnp.exp(s2 - m)
            inv = 1.0 / (jnp.sum(e1, axis=-1, keepdims=True) + jnp.sum(e2, axis=-1, keepdims=True))
            o = (jnp.dot(e1.astype(BF16), vc, preferred_element_type=F32)
                 + jnp.dot(e2.astype(BF16), vs, preferred_element_type=F32))
            o_ref[rs, hs] = (o * inv).astype(o_ref.dtype)


def _attn_sample(p, cache_k, cache_v, q_norm, k_norm, o_prev):
    n = SAMPLE_LEN
    qw = ATT_W // N_KV
    r0 = PROMPT_ROWS // n
    cos, sin = _rope_tables()
    return pl.pallas_call(
        _attn_sample_kernel,
        grid=(N_SAMPLE_SEQ, N_KV),
        in_specs=[pl.BlockSpec((n, qw), lambda s, g: (r0 + s, C_Q // qw + g)),
                  pl.BlockSpec((n, HEAD_DIM), lambda s, g: (r0 + s, C_AK // HEAD_DIM + g)),
                  pl.BlockSpec((n, HEAD_DIM), lambda s, g: (r0 + s, C_AV // HEAD_DIM + g)),
                  pl.BlockSpec((PAST_LEN, HEAD_DIM), lambda s, g: (s, g)),
                  pl.BlockSpec((PAST_LEN, HEAD_DIM), lambda s, g: (s, g)),
                  pl.BlockSpec((1, HEAD_DIM), lambda s, g: (0, 0)),
                  pl.BlockSpec((1, HEAD_DIM), lambda s, g: (0, 0)),
                  pl.BlockSpec((n, HEAD_DIM), lambda s, g: (0, 0)),
                  pl.BlockSpec((n, HEAD_DIM), lambda s, g: (0, 0)),
                  pl.BlockSpec(memory_space=pl.ANY)],
        out_specs=pl.BlockSpec((n, qw), lambda s, g: (r0 + s, g)),
        out_shape=jax.ShapeDtypeStruct((ROWS, ATT_W), BF16),
        input_output_aliases={9: 0},
        compiler_params=_cparams(("parallel", "parallel")),
        name="attn_sample",
    )(p, p, p, cache_k, cache_v, q_norm, k_norm, jnp.asarray(cos), jnp.asarray(sin), o_prev)


_ODD_SEGMENTS = ((0, C_R, 6144), (7008, C_Q, ATT_W), (9056, C_AK, KV_W), (9568, C_AV, KV_W),
                 (6144, C_G, GATE_LORA), (6624, C_W, 2 * DECAY_LORA), (6880, C_A, ICLR_LORA))
_ODD_ZERO = ((C_G + GATE_LORA, GATE_PAD - GATE_LORA), (C_A + ICLR_LORA, ODD_COLS - C_A - ICLR_LORA))


def _pack_odd_kernel(w_ref, o_ref):
    for src, dst, width in _ODD_SEGMENTS:
        o_ref[dst:dst + width, :] = w_ref[src:src + width, :].astype(o_ref.dtype)
    for dst, width in _ODD_ZERO:
        o_ref[dst:dst + width, :] = jnp.zeros((width, o_ref.shape[1]), o_ref.dtype)


def _pack_odd_w_in(wt):
    tc = 256
    n, k = wt.shape
    return pl.pallas_call(
        _pack_odd_kernel,
        grid=(k // tc,),
        in_specs=[pl.BlockSpec((n, tc), lambda j: (0, j))],
        out_specs=pl.BlockSpec((ODD_COLS, tc), lambda j: (0, j)),
        out_shape=jax.ShapeDtypeStruct((ODD_COLS, k), BF16),
        compiler_params=_cparams(("parallel",)),
        name="pack_odd_w_in",
    )(wt)


def _pack_shift_mu(mu):
    z = lambda n: jnp.zeros((n,), mu.dtype)
    out = jnp.concatenate([mu[0:6144], z(C_G - C_Q), mu[6144:6144 + GATE_LORA], z(GATE_PAD - GATE_LORA),
                           mu[6624:6880], mu[6880:7008], z(ODD_COLS - C_A - ICLR_LORA)])
    return out[None, :]


def kernel(x_prompt, x_sample, cache_k, cache_v, state_rwkv, c, c_ctx, w_ada, b_ada, ln_g, ln_b, ffn_w_in,
           ffn_w_out, even_w_in, pool_w, pool_scale, even_w_out, odd_w_in, shift_mu, decay_w0, decay_w2,
           iclr_a0, iclr_w2, gate_w2, k_k, k_a, r_k, gn_g, gn_b, q_norm, k_norm, odd_w_out):
    x = (x_prompt.reshape(PROMPT_ROWS, D), x_sample.reshape(ROWS - PROMPT_ROWS, D))
    cond =jnp.concatenate([c_ctx[None, :], c, jnp.zeros((N_GROUPS - 1 - c.shape[0], D), F32)], axis=0)
    mods = _ada(cond, w_ada, b_ada[:, None, :])
    mods = mods.reshape(DEPTH, N_GROUPS, 3 * N_SUB, D).transpose(0, 2, 1, 3)[:, :, :, None, :]
    ln_g4 = ln_g[:, :, None, :]
    ln_b4 = ln_b[:, :, None, :]

    u = _modulate(x, mods, 0, 0)
    new_k = new_v = new_s = None
    for l in range(DEPTH):
        i = l // 2
        mixer_w = ((even_w_in, (i,)), (even_w_out, (i,))) if l % 2 == 0 else ((odd_w_out, (i,)),)
        h, w_out_b, *mixer_wb = _swiglu_in(u, ffn_w_in, l, 0, ((ffn_w_out, (l, 0)),) + mixer_w)
        z = _mm([h], w_out_b, 512, 512, F32, "ffn_out", resid=(x, mods, l, 0, 0.5))
        x, u = _ln(z, mods, ln_g4, ln_b4, l, 0, (l, 1))
        if l % 2 == 0:
            p = _mm([u], mixer_wb[0], 1024, 512, F32, "even_in")
            mid = _even_mid(p, pool_w[i].astype(BF16), pool_scale[i].reshape(4, 1, GW))
            z = _mm([mid], mixer_wb[1], 1024, 512, F32, "even_out", resid=(x, mods, l, 1, 1.0))
        else:
            p = _mm([u], _pack_odd_w_in(jnp.swapaxes(odd_w_in[i], 0, 1)), 1024, 512, F32, "odd_in",
                    w_rows_out=True)
            gw2 = jnp.concatenate([gate_w2[i], jnp.zeros((GATE_PAD - GATE_LORA, RWKV_W), F32)], axis=0)
            r, k, v, kk, kka, lw, g = _rwkv_prep(
                p, _pack_shift_mu(shift_mu[i]), decay_w0[i][:, None, :], decay_w2[i].astype(BF16),
                iclr_a0[i][None, :], iclr_w2[i].astype(BF16), gw2.astype(BF16), k_k[i][None, :], k_a[i][None, :])
            yf, yb, s_prompt = _rwkv_scan(lw, r, k, v, kk, kka, None, None,
                                          n_seq=N_PROMPT_SEQ, seq_len=PROMPT_LEN, row0=0)
            yf, yb, _ = _rwkv_scan(lw, r, k, v, kk, kka, state_rwkv[:, i], (yf, yb),
                                   n_seq=N_SAMPLE_SEQ, seq_len=SAMPLE_LEN, row0=PROMPT_ROWS)
            y_rwkv = _rwkv_post(yf, yb, r, k, v, g, gn_g[i][None, :], gn_b[i][None, :],
                                r_k[i].reshape(1, RWKV_W))
            o_p, nk, nv = _attn_prompt(p, q_norm[i][None, :], k_norm[i][None, :])
            o = _attn_sample(p, cache_k[:, i].reshape(N_SAMPLE_SEQ * PAST_LEN, KV_W),
                             cache_v[:, i].reshape(N_SAMPLE_SEQ * PAST_LEN, KV_W),
                             q_norm[i][None, :], k_norm[i][None, :], o_p)
            z = _mm([y_rwkv, o], mixer_wb[0], 1024, 512, F32, "odd_out", resid=(x, mods, l, 1, 1.0))
            new_k = nk.reshape(N_PROMPT_SEQ, 1, PROMPT_LEN, N_KV, HEAD_DIM)
            new_v = nv.reshape(N_PROMPT_SEQ, 1, PROMPT_LEN, N_KV, HEAD_DIM)
            new_s = s_prompt[:, None]
        x, u = _ln(z, mods, ln_g4, ln_b4, l, 1, (l, 2))
        h, w_out_b = _swiglu_in(u, ffn_w_in, l, 1, ((ffn_w_out, (l, 1)),))
        z = _mm([h], w_out_b, 512, 512, F32, "ffn_out", resid=(x, mods, l, 2, 0.5))
        if l + 1 < DEPTH:
            x, u = _ln(z, mods, ln_g4, ln_b4, l, 2, (l + 1, 0))
        else:
            y_prompt = _ln(z, mods, ln_g4, ln_b4, l, 2, None, 0, PROMPT_ROWS)
            y_sample = _ln(z, mods, ln_g4, ln_b4, l, 2, None, PROMPT_ROWS, ROWS - PROMPT_ROWS)

    return (y_prompt.reshape(N_PROMPT_SEQ, PROMPT_LEN, D), y_sample.reshape(N_SAMPLE_SEQ, SAMPLE_LEN, D),
            new_k, new_v, new_s)
```

```python
import functools
from typing import Callable, NamedTuple, Optional

import numpy as np
import jax
import jax.numpy as jnp
from jax import lax
from jax.experimental import pallas as pl
from jax.experimental.pallas import tpu as pltpu

F32 = jnp.float32
BF16 = jnp.bfloat16

D = 4096
DEPTH = 2
N_SUB = 3
D_FF = 11008
N_PROMPT_SEQ = 16
PROMPT_LEN = 256
N_SAMPLE_SEQ = 4
SAMPLE_LEN = 1024
PAST_LEN = 512
GRID_W = 64
PROMPT_ROWS = N_PROMPT_SEQ * PROMPT_LEN
ROWS = PROMPT_ROWS + N_SAMPLE_SEQ * SAMPLE_LEN
RB = 1024
N_GROUPS = 8
POOL_WINDOWS = (2, 4, 8, 16)
GW = 512
RWKV_W = 2048
HEAD = 64
N_HEADS = 32
GATE_LORA = 480
GATE_PAD = 512
DECAY_LORA = 128
ICLR_LORA = 128
ATT_W = 2048
HEAD_DIM = 128
N_KV = 4
KV_W = N_KV * HEAD_DIM
ROPE_THETA = 10000.0
ALPHA = (2 * DEPTH) ** 0.25
LN_EPS = 1e-5
GN_EPS = 64e-5
CHUNK = 64
CAST_ROWS = 32

C_R, C_K, C_V, C_Q = 0, 2048, 4096, 6144
C_AK, C_AV, C_G, C_W, C_A = 8192, 8704, 9216, 9728, 9984
ODD_COLS = 10240

VMEM_LIMIT = 56 * 1024 * 1024


def _cparams(sem):
    return pltpu.CompilerParams(dimension_semantics=sem, vmem_limit_bytes=VMEM_LIMIT)


def _group_of_block(i, rows_per_block):
    r0 = i * rows_per_block
    return jnp.where(r0 < PROMPT_ROWS, 0, 1 + (r0 - PROMPT_ROWS) // SAMPLE_LEN)


def _ada_kernel(c_ref, w_ref, b_ref, o_ref):
    c = c_ref[...]
    s = (c * jax.nn.sigmoid(c)).astype(BF16)
    o_ref[...] = jnp.dot(s, w_ref[...].astype(BF16), preferred_element_type=F32) + b_ref[...]


def _ada(cond, w_ada, b_ada):
    tn = 512
    n = w_ada.shape[-1]
    return pl.pallas_call(
        _ada_kernel,
        grid=(DEPTH, n // tn),
        in_specs=[pl.BlockSpec((N_GROUPS, D), lambda l, j: (0, 0)),
                  pl.BlockSpec((None, D, tn), lambda l, j: (l, 0, j)),
                  pl.BlockSpec((None, 1, tn), lambda l, j: (l, 0, j))],
        out_specs=pl.BlockSpec((None, N_GROUPS, tn), lambda l, j: (l, 0, j)),
        out_shape=jax.ShapeDtypeStruct((DEPTH, N_GROUPS, n), F32),
        compiler_params=_cparams(("parallel", "parallel")),
        name="ada",
    )(cond, w_ada, b_ada)


def _mod_spec(tr, l, sub, kind):
    return pl.BlockSpec((None, None, None, 1, D),
                        lambda i: (l, sub * 3 + kind, _group_of_block(i, tr), 0, 0))


def _row_block_specs(x, tr, tn, rank):
    col = (lambda ij: ij[1]) if rank == 2 else (lambda ij: 0)
    if not isinstance(x, tuple):
        return [pl.BlockSpec((tr, tn), lambda *ij: (ij[0], col(ij)))], [x]
    nb = PROMPT_ROWS // tr
    return ([pl.BlockSpec((tr, tn), lambda *ij: (jnp.minimum(ij[0], nb - 1), col(ij))),
             pl.BlockSpec((tr, tn), lambda *ij: (jnp.maximum(ij[0] - nb, 0), col(ij)))], list(x))


def _pick_rows(x_refs, tr):
    if len(x_refs) == 1:
        return x_refs[0][...]
    return jnp.where(pl.program_id(0) * tr < PROMPT_ROWS, x_refs[0][...], x_refs[1][...])


def _modulate_kernel(*refs, n_x, tr):
    shift_ref, scale_ref, u_ref = refs[n_x:]
    x = _pick_rows(refs[:n_x], tr)
    u_ref[...] = (x * (1.0 + scale_ref[...]) + shift_ref[...]).astype(u_ref.dtype)


def _modulate(x, mods, l, sub):
    tr = 256
    x_specs, x_args = _row_block_specs(x, tr, D, 1)
    return pl.pallas_call(
        functools.partial(_modulate_kernel, n_x=len(x_args), tr=tr),
        grid=(ROWS // tr,),
        in_specs=x_specs + [_mod_spec(tr, l, sub, 0), _mod_spec(tr, l, sub, 1)],
        out_specs=pl.BlockSpec((tr, D), lambda i: (i, 0)),
        out_shape=jax.ShapeDtypeStruct((ROWS, D), BF16),
        compiler_params=_cparams(("parallel",)),
        name="modulate",
    )(*x_args, mods, mods)


def _ln_kernel(z_ref, g_ref, b_ref, *rest, emit_u):
    z = z_ref[...]
    mu = jnp.mean(z, axis=-1, keepdims=True)
    zc = z - mu
    var = jnp.mean(zc * zc, axis=-1, keepdims=True)
    xn = zc * lax.rsqrt(var + LN_EPS) * g_ref[...] + b_ref[...]
    if emit_u:
        shift_ref, scale_ref, xo_ref, u_ref = rest
        xo_ref[...] = xn
        u_ref[...] = (xn * (1.0 + scale_ref[...]) + shift_ref[...]).astype(u_ref.dtype)
    else:
        (xo_ref,) = rest
        xo_ref[...] = xn


def _ln(z, mods, ln_g, ln_b, l, sub, nxt, row0=0, n_rows=ROWS):
    tr = 512
    row = pl.BlockSpec((tr, D), lambda i: (i, 0))
    vec = pl.BlockSpec((None, None, 1, D), lambda i: (l, sub, 0, 0))
    in_specs = [pl.BlockSpec((tr, D), lambda i: (row0 // tr + i, 0)), vec, vec]
    args = [z, ln_g, ln_b]
    emit_u = nxt is not None
    if emit_u:
        assert row0 == 0 and n_rows == ROWS
        in_specs += [_mod_spec(tr, nxt[0], nxt[1], 0), _mod_spec(tr, nxt[0], nxt[1], 1)]
        args += [mods, mods]
        out_specs = [row, row]
        out_shape = [jax.ShapeDtypeStruct((ROWS, D), F32), jax.ShapeDtypeStruct((ROWS, D), BF16)]
    else:
        out_specs = row
        out_shape = jax.ShapeDtypeStruct((n_rows, D), F32)
    return pl.pallas_call(
        functools.partial(_ln_kernel, emit_u=emit_u),
        grid=(n_rows // tr,),
        in_specs=in_specs, out_specs=out_specs, out_shape=out_shape,
        compiler_params=_cparams(("parallel",)),
        name="ln",
    )(*args)


def _mm_kernel(*refs, n_a, n_x, tm, y_scale, w_rows_out):
    o_ref = refs[-1]
    dims = (((1,), (1 if w_rows_out else 0,)), ((), ()))
    dot = lambda a_ref, w_ref: lax.dot_general(a_ref[...], w_ref[...], dims, preferred_element_type=F32)
    acc = dot(refs[0], refs[n_a])
    for r in range(1, n_a):
        acc += dot(refs[r], refs[n_a + r])
    if n_x:
        gate_ref = refs[2 * n_a + n_x]
        acc = ALPHA * _pick_rows(refs[2 * n_a:2 * n_a + n_x], tm) + (y_scale * gate_ref[...]) * acc
    o_ref[...] = acc.astype(o_ref.dtype)


def _mm(a_parts, w, tm, tn, out_dtype, name, widx=(), resid=None, w_rows_out=False):
    m, k = a_parts[0].shape
    assert all(a.shape == (m, k) for a in a_parts)
    n = w.shape[-2] if w_rows_out else w.shape[-1]
    lead = (None,) * len(widx)
    in_specs = [pl.BlockSpec((tm, k), lambda i, j: (i, 0)) for _ in a_parts]
    if w_rows_out:
        in_specs += [pl.BlockSpec(lead + (tn, k), lambda i, j, r=r: widx + (j, r)) for r in range(len(a_parts))]
    else:
        in_specs += [pl.BlockSpec(lead + (k, tn), lambda i, j, r=r: widx + (r, j)) for r in range(len(a_parts))]
    args = list(a_parts) + [w] * len(a_parts)
    n_x, y_scale = 0, 1.0
    if resid is not None:
        x, mods, l, sub, y_scale = resid
        x_specs, x_args = _row_block_specs(x, tm, tn, 2)
        n_x = len(x_args)
        in_specs += x_specs + [pl.BlockSpec((None, None, None, 1, tn),
                                            lambda i, j: (l, sub * 3 + 2, _group_of_block(i, tm), 0, j))]
        args += x_args + [mods]
    return pl.pallas_call(
        functools.partial(_mm_kernel, n_a=len(a_parts), n_x=n_x, tm=tm, y_scale=y_scale, w_rows_out=w_rows_out),
        grid=(m // tm, n // tn),
        in_specs=in_specs,
        out_specs=pl.BlockSpec((tm, tn), lambda i, j: (i, j)),
        out_shape=jax.ShapeDtypeStruct((m, n), out_dtype),
        compiler_params=_cparams(("parallel", "parallel")),
        name=name,
    )(*args)


def _swiglu_kernel(*refs, cast_plan, steps_per_col):
    n_cast = len(cast_plan)
    a_ref, wg_ref, wu_ref = refs[:3]
    cast_in = refs[3:3 + n_cast]
    o_ref = refs[3 + n_cast]
    cast_out = refs[4 + n_cast:4 + 2 * n_cast]
    wgb_ref, wub_ref = refs[-2:]

    @pl.when(pl.program_id(1) == 0)
    def _():
        wgb_ref[...] = wg_ref[...].astype(BF16)
        wub_ref[...] = wu_ref[...].astype(BF16)

    step = pl.program_id(0) * steps_per_col + pl.program_id(1)
    for src, dst, (last, zeros) in zip(cast_in, cast_out, cast_plan):
        w = src[...].astype(BF16)
        if zeros:
            blk = jnp.minimum(step, last)
            is_zero = functools.reduce(jnp.logical_or, [blk == z for z in zeros])
            w = jnp.where(is_zero, jnp.zeros_like(w), w)
        dst[...] = w

    a = a_ref[...]
    g = jnp.dot(a, wgb_ref[...], preferred_element_type=F32)
    u = jnp.dot(a, wub_ref[...], preferred_element_type=F32)
    o_ref[...] = (g * jax.nn.sigmoid(g) * u).astype(o_ref.dtype)


class _CastJob(NamedTuple):
    w: jax.Array
    idx: tuple = ()
    dst_rows: int = 0
    src_block: Optional[Callable] = None
    zero_blocks: tuple = ()


def _swiglu_in(u, w_in, l, j, jobs):
    tm, tn = 1024, 256
    nb = D_FF // tn
    mb = ROWS // tm
    wr = CAST_ROWS
    in_specs = [pl.BlockSpec((tm, D), lambda n, m: (m, 0)),
                pl.BlockSpec((None, None, D, tn), lambda n, m: (l, j, 0, n)),
                pl.BlockSpec((None, None, D, tn), lambda n, m: (l, j, 0, nb + n))]
    out_specs = [pl.BlockSpec((tm, tn), lambda n, m: (m, n))]
    out_shape = [jax.ShapeDtypeStruct((ROWS, D_FF), BF16)]
    plan = []
    for job in jobs:
        rows = job.dst_rows or job.w.shape[-2]
        assert job.w.shape[-1] == D and rows % wr == 0 and rows // wr <= nb * mb
        plan.append((rows // wr - 1, tuple(job.zero_blocks)))
        dst = lambda n, m, last=rows // wr - 1: jnp.minimum(n * mb + m, last)
        src = (lambda b: b) if job.src_block is None else job.src_block
        in_specs.append(pl.BlockSpec((None,) * len(job.idx) + (wr, D),
                                     lambda n, m, job=job, dst=dst, src=src: job.idx + (src(dst(n, m)), 0)))
        out_specs.append(pl.BlockSpec((wr, D), lambda n, m, dst=dst: (dst(n, m), 0)))
        out_shape.append(jax.ShapeDtypeStruct((rows, D), BF16))
    return pl.pallas_call(
        functools.partial(_swiglu_kernel, cast_plan=tuple(plan), steps_per_col=mb),
        grid=(nb, mb),
        in_specs=in_specs, out_specs=out_specs, out_shape=out_shape,
        scratch_shapes=[pltpu.VMEM((D, tn), BF16), pltpu.VMEM((D, tn), BF16)],
        compiler_params=_cparams(("arbitrary", "arbitrary")),
        name="swiglu_in",
    )(u, w_in, w_in, *[job.w for job in jobs])


def _dft_tables(n):
    t = np.arange(n)
    ang = 2.0 * np.pi * ((t[:, None] * t[None, :]) % n) / n
    return np.cos(ang).astype(np.float32), np.sin(ang).astype(np.float32)


def _even_tables():
    icnt = np.zeros((2, 4, RB, 1), np.float32)
    for kind, n in enumerate((PROMPT_LEN, SAMPLE_LEN)):
        pos = np.arange(RB) % n
        for gi, w in enumerate(POOL_WINDOWS):
            icnt[kind, gi, :, 0] = 1.0 / (np.clip(pos + w // 2, 0, n) - np.clip(pos - w // 2, 0, n))
    pos_tables = []
    for n in (PROMPT_LEN, SAMPLE_LEN):
        c, s = _dft_tables(n)
        pos_tables += [c * np.float32((n * GW) ** -0.5), s * np.float32((n * GW) ** -0.5)]
    return (icnt,) + tuple(pos_tables) + _dft_tables(GW)


def _window_sum(x, half, seq_len):
    n = x.shape[0]
    pos = lax.broadcasted_iota(jnp.int32, x.shape, 0) % seq_len
    down = lambda y, k: jnp.where(pos >= k, pltpu.roll(y, k, 0), 0.0)
    up = lambda y, k: jnp.where(pos < seq_len - k, pltpu.roll(y, n - k, 0), 0.0)
    trail = lead = x
    k = 1
    while k < half:
        trail = trail + down(trail, k)
        lead = lead + up(lead, k)
        k *= 2
    return down(trail, 1) + lead


def _even_mid_kernel(p_ref, icnt_ref, pw_ref, ps_ref, cnp_ref, snp_ref, cns_ref, sns_ref, cc_ref, sc_ref, o_ref):
    j = pl.program_id(1)
    is_prompt = pl.program_id(0) * RB < PROMPT_ROWS

    def pool(half, seq_len):
        x = p_ref[...]
        pooled = _window_sum(x, half, seq_len) * icnt_ref[...] - x
        y = jnp.dot(pooled.astype(BF16), pw_ref[...], preferred_element_type=F32)
        o_ref[...] = (y * ps_ref[...]).astype(o_ref.dtype)

    def fourier(cn_ref, sn_ref, seq_len):
        x = p_ref[...].astype(BF16)
        xc = jnp.dot(x, cc_ref[...], preferred_element_type=F32).astype(BF16)
        xs = jnp.dot(x, sc_ref[...], preferred_element_type=F32).astype(BF16)
        for s in range(RB // seq_len):
            rs = slice(s * seq_len, (s + 1) * seq_len)
            y = (jnp.dot(cn_ref[...], xc[rs], preferred_element_type=F32)
                 - jnp.dot(sn_ref[...], xs[rs], preferred_element_type=F32))
            o_ref[rs, :] = y.astype(o_ref.dtype)

    for gi, w in enumerate(POOL_WINDOWS):
        pl.when((j == gi) & is_prompt)(functools.partial(pool, w // 2, PROMPT_LEN))
        pl.when((j == gi) & jnp.logical_not(is_prompt))(functools.partial(pool, w // 2, SAMPLE_LEN))
    pl.when((j >= 4) & is_prompt)(functools.partial(fourier, cnp_ref, snp_ref, PROMPT_LEN))
    pl.when((j >= 4) & jnp.logical_not(is_prompt))(functools.partial(fourier, cns_ref, sns_ref, SAMPLE_LEN))


def _even_mid(p, pool_w, pool_scale):
    icnt, *dft = _even_tables()
    cnp, snp, cns, sns, cc, sc = (jnp.asarray(t).astype(BF16) for t in dft)
    kind = lambda i: jnp.where(i * RB < PROMPT_ROWS, 0, 1)
    grp = lambda j: jnp.minimum(j, 3)
    whole = lambda t: pl.BlockSpec(t.shape, lambda i, j: (0, 0))
    return pl.pallas_call(
        _even_mid_kernel,
        grid=(ROWS // RB, 8),
        in_specs=[pl.BlockSpec((RB, GW), lambda i, j: (i, j)),
                  pl.BlockSpec((None, None, RB, 1), lambda i, j: (kind(i), grp(j), 0, 0)),
                  pl.BlockSpec((None, GW, GW), lambda i, j: (grp(j), 0, 0)),
                  pl.BlockSpec((None, 1, GW), lambda i, j: (grp(j), 0, 0)),
                  whole(cnp), whole(snp), whole(cns), whole(sns), whole(cc), whole(sc)],
        out_specs=pl.BlockSpec((RB, GW), lambda i, j: (i, j)),
        out_shape=jax.ShapeDtypeStruct((ROWS, D), BF16),
        compiler_params=_cparams(("parallel", "arbitrary")),
        name="even_mid",
    )(p, jnp.asarray(icnt), pool_w, pool_scale, cnp, snp, cns, sns, cc, sc)


def _head_sum_matrix(width, value):
    h = np.arange(width) // HEAD
    return jnp.asarray((h[:, None] == h[None, :]).astype(np.float32) * value).astype(BF16)


def _split_bf16(x, n):
    parts = []
    for _ in range(n - 1):
        h = x.astype(BF16)
        parts.append(h)
        x = x - h.astype(F32)
    parts.append(x.astype(BF16))
    return parts


def _head_sum(x, e):
    hi, lo = _split_bf16(x, 2)
    return jnp.dot(hi, e, preferred_element_type=F32) + jnp.dot(lo, e, preferred_element_type=F32)


def _token_shift(x, mu, seq_len):
    n = x.shape[0]
    pos = lax.broadcasted_iota(jnp.int32, x.shape, 0) % seq_len
    prev = jnp.where(pos == 0, 0.0, pltpu.roll(x, 1, 0))
    nxt = jnp.where(pos == seq_len - 1, 0.0, pltpu.roll(x, n - 1, 0))
    return x + (0.5 * (prev + nxt) - x) * mu


def _rwkv_prep_kernel(pr_ref, pk_ref, pv_ref, pg_ref, pw_ref, pa_ref,
                      mr_ref, mk_ref, mv_ref, mg_ref, mw_ref, ma_ref,
                      w0_ref, w2_ref, a0_ref, aw2_ref, gw2_ref, kk_ref, ka_ref, e_ref,
                      r_o, k_o, v_o, kk_o, kka_o, lw_o, g_o, sg_ref, tw_ref, al_ref):
    i = pl.program_id(0)

    def body(seq_len):
        @pl.when(pl.program_id(1) == 0)
        def _():
            sg_ref[...] = jax.nn.sigmoid(_token_shift(pg_ref[...], mg_ref[...], seq_len)).astype(BF16)
            tw_ref[...] = jnp.tanh(_token_shift(pw_ref[...], mw_ref[...], seq_len)).astype(BF16)
            al_ref[...] = _token_shift(pa_ref[...], ma_ref[...], seq_len).astype(BF16)

        r = _token_shift(pr_ref[...], mr_ref[...], seq_len)
        k = _token_shift(pk_ref[...], mk_ref[...], seq_len)
        v = _token_shift(pv_ref[...], mv_ref[...], seq_len)
        for d in range(2):
            x = w0_ref[d] + jnp.dot(tw_ref[:, d * DECAY_LORA:(d + 1) * DECAY_LORA], w2_ref[d],
                                    preferred_element_type=F32)
            z = -x
            softplus = jnp.maximum(z, 0.0) + jnp.log(1.0 + jnp.exp(-jnp.abs(z)))
            lw_o[d] = -jnp.exp(-softplus - 0.5)
        a = jax.nn.sigmoid(a0_ref[...] + jnp.dot(al_ref[...], aw2_ref[...], preferred_element_type=F32))
        g_o[...] = jnp.dot(sg_ref[...], gw2_ref[...], preferred_element_type=F32)
        kk = k * kk_ref[...]
        ss = _head_sum(kk * kk, e_ref[...])
        kk = kk * lax.rsqrt(ss + 1e-12)
        r_o[...] = r
        k_o[...] = k * (1.0 + (a - 1.0) * ka_ref[...])
        v_o[...] = v
        kk_o[...] = kk
        kka_o[...] = kk * a

    @pl.when(i * RB < PROMPT_ROWS)
    def _():
        body(PROMPT_LEN)

    @pl.when(i * RB >= PROMPT_ROWS)
    def _():
        body(SAMPLE_LEN)


def _rwkv_prep(p, mu, w0, w2, a0, aw2, gw2, k_k, k_a):
    cw = 256
    ncb = RWKV_W // cw
    e = _head_sum_matrix(cw, 1.0)
    pcol = lambda base, width: pl.BlockSpec((RB, width), lambda i, c: (i, base // width))
    pch = lambda base: pl.BlockSpec((RB, cw), lambda i, c: (i, base // cw + c))
    mcol = lambda base, width: pl.BlockSpec((1, width), lambda i, c: (0, base // width))
    mch = lambda base: pl.BlockSpec((1, cw), lambda i, c: (0, base // cw + c))
    vch = pl.BlockSpec((1, cw), lambda i, c: (0, c))
    out = pl.BlockSpec((RB, cw), lambda i, c: (i, c))
    sds = jax.ShapeDtypeStruct((ROWS, RWKV_W), F32)
    return pl.pallas_call(
        _rwkv_prep_kernel,
        grid=(ROWS // RB, ncb),
        in_specs=[pch(C_R), pch(C_K), pch(C_V), pcol(C_G, GATE_PAD), pcol(C_W, 2 * DECAY_LORA), pcol(C_A, ICLR_LORA),
                  mch(C_R), mch(C_K), mch(C_V), mcol(C_G, GATE_PAD), mcol(C_W, 2 * DECAY_LORA), mcol(C_A, ICLR_LORA),
                  pl.BlockSpec((2, 1, cw), lambda i, c: (0, 0, c)),
                  pl.BlockSpec((2, DECAY_LORA, cw), lambda i, c: (0, 0, c)),
                  vch,
                  pl.BlockSpec((ICLR_LORA, cw), lambda i, c: (0, c)),
                  pl.BlockSpec((GATE_PAD, cw), lambda i, c: (0, c)),
                  vch, vch,
                  pl.BlockSpec((cw, cw), lambda i, c: (0, 0))],
        out_specs=[out, out, out, out, out,
                   pl.BlockSpec((2, RB, cw), lambda i, c: (0, i, c)), out],
        out_shape=[sds, sds, sds, sds, sds, jax.ShapeDtypeStruct((2, ROWS, RWKV_W), F32), sds],
        scratch_shapes=[pltpu.VMEM((RB, GATE_PAD), BF16), pltpu.VMEM((RB, 2 * DECAY_LORA), BF16),
                        pltpu.VMEM((RB, ICLR_LORA), BF16)],
        compiler_params=_cparams(("parallel", "arbitrary")),
        name="rwkv_prep",
    )(p, p, p, p, p, p, mu, mu, mu, mu, mu, mu, w0, w2, a0, aw2, gw2, k_k, k_a, e)


def _mmp(a, b, nt=False):
    dims = (((1,), (1 if nt else 0,)), ((), ()))
    return lax.dot_general(a.astype(BF16), b.astype(BF16), dims, preferred_element_type=F32)


def _rwkv_scan_kernel(*refs, has_init, pairs):
    in_refs = (refs[0:6], refs[6:12])
    s0_ref = refs[12] if has_init else None
    y_refs = refs[-4:-2]
    s_ref, sp_ref = refs[-2:]
    c_idx = pl.program_id(1)
    n_chunks = pl.num_programs(1)
    L = CHUNK
    W2 = 2 * HEAD

    @pl.when(c_idx == 0)
    def _():
        sp_ref[...] = jnp.zeros_like(sp_ref)
        if has_init:
            for d in range(2):
                for p in range(pairs):
                    sp_ref[d * pairs + p, 0:HEAD, 0:HEAD] = s0_ref[d, 2 * p]
                    sp_ref[d * pairs + p, HEAD:W2, HEAD:W2] = s0_ref[d, 2 * p + 1]

    row = lax.broadcasted_iota(jnp.int32, (L, W2), 0)
    lane = lax.broadcasted_iota(jnp.int32, (L, W2), 1)
    col = lane % HEAD
    first = lane < HEAD
    incl_d = (col <= row, col >= row)
    strict_d = (col < row, col > row)
    eye = (col == row).astype(F32)
    blk8 = (row // 8) == (col // 8)
    off = [((row // (2 * b)) == (col // (2 * b))) & ((row // b) != (col // b)) for b in (8, 16, 32)]
    row_s = lax.broadcasted_iota(jnp.int32, (W2, W2), 0)
    lane_s = lax.broadcasted_iota(jnp.int32, (W2, W2), 1)
    diag_blocks = (row_s < HEAD) == (lane_s < HEAD)
    trow = lax.broadcasted_iota(jnp.int32, (L, L), 0)
    tcol = lax.broadcasted_iota(jnp.int32, (L, L), 1)
    tri_d = ((tcol <= trow).astype(BF16), (tcol >= trow).astype(BF16))

    first_b = lane.astype(F32).astype(BF16) < HEAD

    def bd(x):
        xb = x.astype(BF16)
        zero = jnp.zeros_like(xb)
        return jnp.concatenate([jnp.where(first_b, xb, zero), jnp.where(first_b, zero, xb)], axis=0)

    ps = [slice(p * W2, (p + 1) * W2) for p in range(pairs)]
    each = lambda f, *xs: [f(*z) for z in zip(*xs)]
    ar, bt, kt, v, g_last = [], [], [], [], []
    for d in range(2):
        lw_ref, r_ref, k_ref, v_ref, a_ref, b_ref = in_refs[d]
        lw = lw_ref[...]
        c = sum(jnp.dot(tri_d[d], t, preferred_element_type=F32) for t in _split_bf16(lw, 3))
        g_in = jnp.exp(c)
        g_inv = jnp.exp(-c)
        at_all = a_ref[...] * jnp.exp(c - lw)
        rt_all = r_ref[...] * g_in
        bt_all = b_ref[...] * g_inv
        kt_all = k_ref[...] * g_inv
        v_all = v_ref[...]
        g_end = g_in[0:1, :] if d else g_in[L - 1:L, :]
        ar += [jnp.concatenate([at_all[:, s], rt_all[:, s]], axis=0) for s in ps]
        bt += [bt_all[:, s] for s in ps]
        kt += [kt_all[:, s] for s in ps]
        v += [v_all[:, s] for s in ps]
        g_last += [g_end[:, s] for s in ps]
    n_inst = 2 * pairs
    strict = [strict_d[i // pairs] for i in range(n_inst)]
    incl = [incl_d[i // pairs] for i in range(n_inst)]
    s0 = [sp_ref[i] for i in range(n_inst)]
    sc = each(lambda x, b, k, s: _mmp(x, jnp.concatenate([bd(b), bd(k), s.astype(BF16)], axis=0), nt=True),
              ar, bt, kt, s0)
    npq = [x[:, 0:W2] for x in sc]
    mq = [x[:, W2:2 * W2] for x in sc]
    ws = [x[:, 2 * W2:3 * W2] for x in sc]
    n_mat = each(lambda m, x: jnp.where(m, x[:L], 0.0), strict, npq)
    m_mat = each(lambda m, x: jnp.where(m, x[:L], 0.0), strict, mq)
    p_mat = each(lambda m, x: jnp.where(m, x[L:], 0.0), incl, npq)
    q_mat = each(lambda m, x: jnp.where(m, x[L:], 0.0), incl, mq)
    mqv = each(lambda m, q, y: _mmp(jnp.concatenate([m, q], axis=0), bd(y)), m_mat, q_mat, v)
    w = each(lambda x, z: x[:L] + z[:L], ws, mqv)
    n8 = [jnp.where(blk8, x, 0.0) for x in n_mat]
    n8_2 = each(lambda x: _mmp(x, bd(x)), n8)
    n8_4 = each(lambda x: _mmp(x, bd(x)), n8_2)
    t = each(lambda x, y: _mmp(eye - x, bd(eye + y)), n8, n8_2)
    t = each(lambda x, y: _mmp(x, bd(eye + y)), t, n8_4)
    for m in off:
        x = each(lambda n, y: _mmp(jnp.where(m, n, 0.0), bd(y)), n_mat, t)
        t = each(lambda y, z: y - _mmp(y, bd(z)), t, x)
    u = each(lambda x, y: -_mmp(x, bd(y)), t, w)
    y = each(lambda x, z, pm, uu: x[L:] + z[L:] + _mmp(pm, bd(uu)), ws, mqv, p_mat, u)
    for i in range(n_inst):
        y_refs[i // pairs][:, ps[i % pairs]] = y[i]
    upd = each(lambda uu, vv, b, k: _mmp(jnp.concatenate([uu, vv], axis=0).T,
                                         jnp.concatenate([b, k], axis=0)), u, v, bt, kt)
    s_new = each(lambda s, x, g: (s + jnp.where(diag_blocks, x, 0.0)) * g, s0, upd, g_last)
    for i in range(n_inst):
        sp_ref[i] = s_new[i]

    @pl.when(c_idx == n_chunks - 1)
    def _():
        for i in range(n_inst):
            d, p = divmod(i, pairs)
            s_ref[d, 2 * p] = s_new[i][0:HEAD, 0:HEAD]
            s_ref[d, 2 * p + 1] = s_new[i][HEAD:W2, HEAD:W2]


def _rwkv_scan(lw, r, k, v, a, b, s0, y_prev, *, n_seq, seq_len, row0):
    pairs = N_HEADS // 2
    nc = seq_len // CHUNK

    def rows(d):
        return lambda s, c: row0 // CHUNK + s * nc + (nc - 1 - c if d else c)

    in_specs, args = [], []
    for d in range(2):
        in_specs.append(pl.BlockSpec((None, CHUNK, RWKV_W), lambda s, c, d=d: (d, rows(d)(s, c), 0)))
        in_specs += [pl.BlockSpec((CHUNK, RWKV_W), lambda s, c, d=d: (rows(d)(s, c), 0))] * 5
        args += [lw, r, k, v, a, b]
    if s0 is not None:
        in_specs.append(pl.BlockSpec((None, 2, N_HEADS, HEAD, HEAD), lambda s, c: (s, 0, 0, 0, 0)))
        args.append(s0)
    aliases = {}
    if y_prev is not None:
        aliases = {len(args): 0, len(args) + 1: 1}
        in_specs += [pl.BlockSpec(memory_space=pl.ANY)] * 2
        args += list(y_prev)
    y_sds = jax.ShapeDtypeStruct((ROWS, RWKV_W), F32)
    return pl.pallas_call(
        functools.partial(_rwkv_scan_kernel, has_init=s0 is not None, pairs=pairs),
        grid=(n_seq, nc),
        in_specs=in_specs,
        out_specs=[pl.BlockSpec((CHUNK, RWKV_W), lambda s, c: (rows(0)(s, c), 0)),
                   pl.BlockSpec((CHUNK, RWKV_W), lambda s, c: (rows(1)(s, c), 0)),
                   pl.BlockSpec((None, 2, N_HEADS, HEAD, HEAD), lambda s, c: (s, 0, 0, 0, 0))],
        out_shape=[y_sds, y_sds, jax.ShapeDtypeStruct((n_seq, 2, N_HEADS, HEAD, HEAD), F32)],
        scratch_shapes=[pltpu.VMEM((2 * pairs, 2 * HEAD, 2 * HEAD), F32)],
        input_output_aliases=aliases,
        compiler_params=_cparams(("parallel", "arbitrary")),
        name="rwkv_scan",
    )(*args)


def _rwkv_post_kernel(yf_ref, yb_ref, r_ref, k_ref, v_ref, g_ref, gg_ref, gb_ref, rk_ref, e_ref, o_ref):
    y = yf_ref[...] + yb_ref[...]
    e = e_ref[...]
    mu = _head_sum(y, e)
    yc = y - mu
    var = _head_sum(yc * yc, e)
    yn = yc * lax.rsqrt(var + GN_EPS) * gg_ref[...] + gb_ref[...]
    bonus = _head_sum(r_ref[...] * k_ref[...] * rk_ref[...], e) * float(HEAD) * v_ref[...]
    o_ref[...] = ((yn + bonus) * g_ref[...]).astype(o_ref.dtype)


def _rwkv_post(yf, yb, r, k, v, g, gn_g, gn_b, r_k):
    tr, cw = 512, 512
    e = _head_sum_matrix(cw, 1.0 / HEAD)
    blk = pl.BlockSpec((tr, cw), lambda i, c: (i, c))
    vec = pl.BlockSpec((1, cw), lambda i, c: (0, c))
    return pl.pallas_call(
        _rwkv_post_kernel,
        grid=(ROWS // tr, RWKV_W // cw),
        in_specs=[blk, blk, blk, blk, blk, blk, vec, vec, vec, pl.BlockSpec((cw, cw), lambda i, c: (0, 0))],
        out_specs=blk,
        out_shape=jax.ShapeDtypeStruct((ROWS, RWKV_W), BF16),
        compiler_params=_cparams(("parallel", "parallel")),
        name="rwkv_post",
    )(yf, yb, r, k, v, g, gn_g, gn_b, r_k, e)


def _rms(x, gain):
    return x * lax.rsqrt(jnp.mean(x * x, axis=-1, keepdims=True) + 1e-6) * gain


def _rope(x, cos, sin_signed):
    lane = lax.broadcasted_iota(jnp.int32, x.shape, 1)
    quarter = HEAD_DIM // 4
    partner = jnp.where(lane % (2 * quarter) < quarter,
                        pltpu.roll(x, HEAD_DIM - quarter, 1), pltpu.roll(x, quarter, 1))
    return x * cos + partner * sin_signed


def _rope_tables():
    n = SAMPLE_LEN
    quarter = HEAD_DIM // 4
    row = (np.arange(n) // GRID_W).astype(np.float64)
    col = (np.arange(n) % GRID_W).astype(np.float64)
    inv_freq = ROPE_THETA ** (-np.arange(quarter, dtype=np.float64) / quarter)
    ar = row[:, None] * inv_freq[None, :]
    ac = col[:, None] * inv_freq[None, :]
    cos = np.concatenate([np.cos(ar), np.cos(ar), np.cos(ac), np.cos(ac)], axis=1)
    sin = np.concatenate([-np.sin(ar), np.sin(ar), -np.sin(ac), np.sin(ac)], axis=1)
    return cos.astype(np.float32), sin.astype(np.float32)


def _attn_prompt_kernel(q_ref, k_ref, v_ref, qn_ref, kn_ref, o_ref, ko_ref, vo_ref):
    scale = HEAD_DIM ** -0.5
    kn = _rms(k_ref[...], kn_ref[...])
    v = v_ref[...]
    ko_ref[...] = kn
    vo_ref[...] = v
    kb = kn.astype(BF16)
    vb = v.astype(BF16)
    for j in range(ATT_W // KV_W):
        hs = slice(j * HEAD_DIM, (j + 1) * HEAD_DIM)
        q = (_rms(q_ref[:, hs], qn_ref[...]) * scale).astype(BF16)
        s = lax.dot_general(q, kb, (((1,), (1,)), ((), ())), preferred_element_type=F32)
        e = jnp.exp(s - jnp.max(s, axis=-1, keepdims=True))
        inv = 1.0 / jnp.sum(e, axis=-1, keepdims=True)
        o_ref[:, hs] = (jnp.dot(e.astype(BF16), vb, preferred_element_type=F32) * inv).astype(o_ref.dtype)


def _attn_prompt(p, q_norm, k_norm):
    n = PROMPT_LEN
    qw = ATT_W // N_KV
    return pl.pallas_call(
        _attn_prompt_kernel,
        grid=(N_PROMPT_SEQ, N_KV),
        in_specs=[pl.BlockSpec((n, qw), lambda s, g: (s, C_Q // qw + g)),
                  pl.BlockSpec((n, HEAD_DIM), lambda s, g: (s, C_AK // HEAD_DIM + g)),
                  pl.BlockSpec((n, HEAD_DIM), lambda s, g: (s, C_AV // HEAD_DIM + g)),
                  pl.BlockSpec((1, HEAD_DIM), lambda s, g: (0, 0)),
                  pl.BlockSpec((1, HEAD_DIM), lambda s, g: (0, 0))],
        out_specs=[pl.BlockSpec((n, qw), lambda s, g: (s, g)),
                   pl.BlockSpec((n, HEAD_DIM), lambda s, g: (s, g)),
                   pl.BlockSpec((n, HEAD_DIM), lambda s, g: (s, g))],
        out_shape=[jax.ShapeDtypeStruct((ROWS, ATT_W), BF16),
                   jax.ShapeDtypeStruct((PROMPT_ROWS, KV_W), F32),
                   jax.ShapeDtypeStruct((PROMPT_ROWS, KV_W), F32)],
        compiler_params=_cparams(("parallel", "parallel")),
        name="attn_prompt",
    )(p, p, p, q_norm, k_norm)


def _attn_sample_kernel(q_ref, k_ref, v_ref, ck_ref, cv_ref, qn_ref, kn_ref, cos_ref, sin_ref, _o_prev, o_ref):
    scale = HEAD_DIM ** -0.5
    tq = 256
    cos = cos_ref[...]
    sin = sin_ref[...]
    ks = _rope(_rms(k_ref[...], kn_ref[...]), cos, sin).astype(BF16)
    vs = v_ref[...].astype(BF16)
    kc = ck_ref[...].astype(BF16)
    vc = cv_ref[...].astype(BF16)
    nt = (((1,), (1,)), ((), ()))
    for j in range(ATT_W // KV_W):
        hs = slice(j * HEAD_DIM, (j + 1) * HEAD_DIM)
        for qb in range(SAMPLE_LEN // tq):
            rs = slice(qb * tq, (qb + 1) * tq)
            q = (_rope(_rms(q_ref[rs, hs], qn_ref[...]), cos[rs], sin[rs]) * scale).astype(BF16)
            s1 = lax.dot_general(q, kc, nt, preferred_element_type=F32)
            s2 = lax.dot_general(q, ks, nt, preferred_element_type=F32)
            m = jnp.maximum(jnp.max(s1, axis=-1, keepdims=True), jnp.max(s2, axis=-1, keepdims=True))
            e1 = jnp.exp(s1 - m)
            e2 = jnp.exp(s2 - m)
            inv = 1.0 / (jnp.sum(e1, axis=-1, keepdims=True) + jnp.sum(e2, axis=-1, keepdims=True))
            o = (jnp.dot(e1.astype(BF16), vc, preferred_element_type=F32)
                 + jnp.dot(e2.astype(BF16), vs, preferred_element_type=F32))
            o_ref[rs, hs] = (o * inv).astype(o_ref.dtype)


def _attn_sample(p, cache_k, cache_v, q_norm, k_norm, o_prev):
    n = SAMPLE_LEN
    qw = ATT_W // N_KV
    r0 = PROMPT_ROWS // n
    cos, sin = _rope_tables()
    return pl.pallas_call(
        _attn_sample_kernel,
        grid=(N_SAMPLE_SEQ, N_KV),
        in_specs=[pl.BlockSpec((n, qw), lambda s, g: (r0 + s, C_Q // qw + g)),
                  pl.BlockSpec((n, HEAD_DIM), lambda s, g: (r0 + s, C_AK // HEAD_DIM + g)),
                  pl.BlockSpec((n, HEAD_DIM), lambda s, g: (r0 + s, C_AV // HEAD_DIM + g)),
                  pl.BlockSpec((PAST_LEN, HEAD_DIM), lambda s, g: (s, g)),
                  pl.BlockSpec((PAST_LEN, HEAD_DIM), lambda s, g: (s, g)),
                  pl.BlockSpec((1, HEAD_DIM), lambda s, g: (0, 0)),
                  pl.BlockSpec((1, HEAD_DIM), lambda s, g: (0, 0)),
                  pl.BlockSpec((n, HEAD_DIM), lambda s, g: (0, 0)),
                  pl.BlockSpec((n, HEAD_DIM), lambda s, g: (0, 0)),
                  pl.BlockSpec(memory_space=pl.ANY)],
        out_specs=pl.BlockSpec((n, qw), lambda s, g: (r0 + s, g)),
        out_shape=jax.ShapeDtypeStruct((ROWS, ATT_W), BF16),
        input_output_aliases={9: 0},
        compiler_params=_cparams(("parallel", "parallel")),
        name="attn_sample",
    )(p, p, p, cache_k, cache_v, q_norm, k_norm, jnp.asarray(cos), jnp.asarray(sin), o_prev)


_ODD_SEGMENTS = ((0, C_R, 6144), (7008, C_Q, ATT_W), (9056, C_AK, KV_W), (9568, C_AV, KV_W),
                 (6144, C_G, GATE_LORA), (6624, C_W, 2 * DECAY_LORA), (6880, C_A, ICLR_LORA))
_ODD_ZERO = ((C_G + GATE_LORA, GATE_PAD - GATE_LORA), (C_A + ICLR_LORA, ODD_COLS - C_A - ICLR_LORA))


def _pack_odd_job(wt):
    b = CAST_ROWS
    assert all(v % b == 0 for seg in _ODD_SEGMENTS + _ODD_ZERO for v in seg)
    runs = [(dst // b, (dst + width) // b, (src - dst) // b) for src, dst, width in _ODD_SEGMENTS]
    zeros = tuple(z for dst, width in _ODD_ZERO for z in range(dst // b, (dst + width) // b))

    def src_block(blk):
        out = jnp.zeros_like(blk)
        for lo, hi, delta in runs:
            out = jnp.where((blk >= lo) & (blk < hi), blk + delta, out)
        return out

    return _CastJob(wt, (), ODD_COLS, src_block, zeros)


def _pack_shift_mu(mu):
    z = lambda n: jnp.zeros((n,), mu.dtype)
    out = jnp.concatenate([mu[0:6144], z(C_G - C_Q), mu[6144:6144 + GATE_LORA], z(GATE_PAD - GATE_LORA),
                           mu[6624:6880], mu[6880:7008], z(ODD_COLS - C_A - ICLR_LORA)])
    return out[None, :]


def kernel(x_prompt, x_sample, cache_k, cache_v, state_rwkv, c, c_ctx, w_ada, b_ada, ln_g, ln_b, ffn_w_in,
           ffn_w_out, even_w_in, pool_w, pool_scale, even_w_out, odd_w_in, shift_mu, decay_w0, decay_w2,
           iclr_a0, iclr_w2, gate_w2, k_k, k_a, r_k, gn_g, gn_b, q_norm, k_norm, odd_w_out):
    x = (x_prompt.reshape(PROMPT_ROWS, D), x_sample.reshape(ROWS - PROMPT_ROWS, D))
    cond =jnp.concatenate([c_ctx[None, :], c, jnp.zeros((N_GROUPS - 1 - c.shape[0], D), F32)], axis=0)
    mods = _ada(cond, w_ada, b_ada[:, None, :])
    mods = mods.reshape(DEPTH, N_GROUPS, 3 * N_SUB, D).transpose(0, 2, 1, 3)[:, :, :, None, :]
    ln_g4 = ln_g[:, :, None, :]
    ln_b4 = ln_b[:, :, None, :]

    u = _modulate(x, mods, 0, 0)
    new_k = new_v = new_s = None
    for l in range(DEPTH):
        i = l // 2
        if l % 2 == 0:
            mixer_w = [_CastJob(even_w_in, (i,)), _CastJob(even_w_out, (i,))]
        else:
            mixer_w = [_pack_odd_job(jnp.swapaxes(odd_w_in[i], 0, 1)), _CastJob(odd_w_out, (i,))]
        h, w_out_b, *mixer_wb = _swiglu_in(u, ffn_w_in, l, 0, [_CastJob(ffn_w_out, (l, 0))] + mixer_w)
        z = _mm([h], w_out_b, 512, 512, F32, "ffn_out", resid=(x, mods, l, 0, 0.5))
        x, u = _ln(z, mods, ln_g4, ln_b4, l, 0, (l, 1))
        if l % 2 == 0:
            p = _mm([u], mixer_wb[0], 1024, 512, F32, "even_in")
            mid = _even_mid(p, pool_w[i].astype(BF16), pool_scale[i].reshape(4, 1, GW))
            z = _mm([mid], mixer_wb[1], 1024, 512, F32, "even_out", resid=(x, mods, l, 1, 1.0))
        else:
            p = _mm([u], mixer_wb[0], 1024, 512, F32, "odd_in", w_rows_out=True)
            gw2 = jnp.concatenate([gate_w2[i], jnp.zeros((GATE_PAD - GATE_LORA, RWKV_W), F32)], axis=0)
            r, k, v, kk, kka, lw, g = _rwkv_prep(
                p, _pack_shift_mu(shift_mu[i]), decay_w0[i][:, None, :], decay_w2[i].astype(BF16),
                iclr_a0[i][None, :], iclr_w2[i].astype(BF16), gw2.astype(BF16), k_k[i][None, :], k_a[i][None, :])
            yf, yb, s_prompt = _rwkv_scan(lw, r, k, v, kk, kka, None, None,
                                          n_seq=N_PROMPT_SEQ, seq_len=PROMPT_LEN, row0=0)
            yf, yb, _ = _rwkv_scan(lw, r, k, v, kk, kka, state_rwkv[:, i], (yf, yb),
                                   n_seq=N_SAMPLE_SEQ, seq_len=SAMPLE_LEN, row0=PROMPT_ROWS)
            y_rwkv = _rwkv_post(yf, yb, r, k, v, g, gn_g[i][None, :], gn_b[i][None, :],
                                r_k[i].reshape(1, RWKV_W))
            o_p, nk, nv = _attn_prompt(p, q_norm[i][None, :], k_norm[i][None, :])
            o = _attn_sample(p, cache_k[:, i].reshape(N_SAMPLE_SEQ * PAST_LEN, KV_W),
                             cache_v[:, i].reshape(N_SAMPLE_SEQ * PAST_LEN, KV_W),
                             q_norm[i][None, :], k_norm[i][None, :], o_p)
            z = _mm([y_rwkv, o], mixer_wb[1], 1024, 512, F32, "odd_out", resid=(x, mods, l, 1, 1.0))
            new_k = nk.reshape(N_PROMPT_SEQ, 1, PROMPT_LEN, N_KV, HEAD_DIM)
            new_v = nv.reshape(N_PROMPT_SEQ, 1, PROMPT_LEN, N_KV, HEAD_DIM)
            new_s = s_prompt[:, None]
        x, u = _ln(z, mods, ln_g4, ln_b4, l, 1, (l, 2))
        h, w_out_b = _swiglu_in(u, ffn_w_in, l, 1, [_CastJob(ffn_w_out, (l, 1))])
        z = _mm([h], w_out_b, 512, 512, F32, "ffn_out", resid=(x, mods, l, 2, 0.5))
        if l + 1 < DEPTH:
            x, u = _ln(z, mods, ln_g4, ln_b4, l, 2, (l + 1, 0))
        else:
            y_prompt = _ln(z, mods, ln_g4, ln_b4, l, 2, None, 0, PROMPT_ROWS)
            y_sample = _ln(z, mods, ln_g4, ln_b4, l, 2, None, PROMPT_ROWS, ROWS - PROMPT_ROWS)

    return (y_prompt.reshape(N_PROMPT_SEQ, PROMPT_LEN, D), y_sample.reshape(N_SAMPLE_SEQ, SAMPLE_LEN, D),
            new_k, new_v, new_s)
```

```python
import functools
from typing import Callable, NamedTuple, Optional

import numpy as np
import jax
import jax.numpy as jnp
from jax import lax
from jax.experimental import pallas as pl
from jax.experimental.pallas import tpu as pltpu

F32 = jnp.float32
BF16 = jnp.bfloat16

D = 4096
DEPTH = 2
N_SUB = 3
D_FF = 11008
N_PROMPT_SEQ = 16
PROMPT_LEN = 256
N_SAMPLE_SEQ = 4
SAMPLE_LEN = 1024
PAST_LEN = 512
GRID_W = 64
PROMPT_ROWS = N_PROMPT_SEQ * PROMPT_LEN
ROWS = PROMPT_ROWS + N_SAMPLE_SEQ * SAMPLE_LEN
RB = 1024
N_GROUPS = 8
POOL_WINDOWS = (2, 4, 8, 16)
GW = 512
RWKV_W = 2048
HEAD = 64
N_HEADS = 32
GATE_LORA = 480
GATE_PAD = 512
DECAY_LORA = 128
ICLR_LORA = 128
ATT_W = 2048
HEAD_DIM = 128
N_KV = 4
KV_W = N_KV * HEAD_DIM
ROPE_THETA = 10000.0
ALPHA = (2 * DEPTH) ** 0.25
LN_EPS = 1e-5
GN_EPS = 64e-5
CHUNK = 64
CAST_ROWS = 32
PACK_ROWS = 32

C_R, C_K, C_V, C_Q = 0, 2048, 4096, 6144
C_AK, C_AV, C_G, C_W, C_A = 8192, 8704, 9216, 9728, 9984
ODD_COLS = 10240

VMEM_LIMIT = 56 * 1024 * 1024


def _cparams(sem):
    return pltpu.CompilerParams(dimension_semantics=sem, vmem_limit_bytes=VMEM_LIMIT)


def _group_of_block(i, rows_per_block):
    r0 = i * rows_per_block
    return jnp.where(r0 < PROMPT_ROWS, 0, 1 + (r0 - PROMPT_ROWS) // SAMPLE_LEN)


def _ada_kernel(c_ref, w_ref, b_ref, o_ref):
    c = c_ref[...]
    s = (c * jax.nn.sigmoid(c)).astype(BF16)
    o_ref[...] = jnp.dot(s, w_ref[...].astype(BF16), preferred_element_type=F32) + b_ref[...]


def _ada(cond, w_ada, b_ada):
    tn = 512
    n = w_ada.shape[-1]
    return pl.pallas_call(
        _ada_kernel,
        grid=(DEPTH, n // tn),
        in_specs=[pl.BlockSpec((N_GROUPS, D), lambda l, j: (0, 0)),
                  pl.BlockSpec((None, D, tn), lambda l, j: (l, 0, j)),
                  pl.BlockSpec((None, 1, tn), lambda l, j: (l, 0, j))],
        out_specs=pl.BlockSpec((None, N_GROUPS, tn), lambda l, j: (l, 0, j)),
        out_shape=jax.ShapeDtypeStruct((DEPTH, N_GROUPS, n), F32),
        compiler_params=_cparams(("parallel", "parallel")),
        name="ada",
    )(cond, w_ada, b_ada)


def _mod_spec(tr, l, sub, kind):
    return pl.BlockSpec((None, None, None, 1, D),
                        lambda i: (l, sub * 3 + kind, _group_of_block(i, tr), 0, 0))


def _row_block_specs(x, tr, tn, rank):
    col = (lambda ij: ij[1]) if rank == 2 else (lambda ij: 0)
    if not isinstance(x, tuple):
        return [pl.BlockSpec((tr, tn), lambda *ij: (ij[0], col(ij)))], [x]
    nb = PROMPT_ROWS // tr
    return ([pl.BlockSpec((tr, tn), lambda *ij: (jnp.minimum(ij[0], nb - 1), col(ij))),
             pl.BlockSpec((tr, tn), lambda *ij: (jnp.maximum(ij[0] - nb, 0), col(ij)))], list(x))


def _pick_rows(x_refs, tr):
    if len(x_refs) == 1:
        return x_refs[0][...]
    return jnp.where(pl.program_id(0) * tr < PROMPT_ROWS, x_refs[0][...], x_refs[1][...])


def _modulate_kernel(*refs, n_x, tr):
    shift_ref, scale_ref, u_ref = refs[n_x:]
    x = _pick_rows(refs[:n_x], tr)
    u_ref[...] = (x * (1.0 + scale_ref[...]) + shift_ref[...]).astype(u_ref.dtype)


def _modulate(x, mods, l, sub):
    tr = 256
    x_specs, x_args = _row_block_specs(x, tr, D, 1)
    return pl.pallas_call(
        functools.partial(_modulate_kernel, n_x=len(x_args), tr=tr),
        grid=(ROWS // tr,),
        in_specs=x_specs + [_mod_spec(tr, l, sub, 0), _mod_spec(tr, l, sub, 1)],
        out_specs=pl.BlockSpec((tr, D), lambda i: (i, 0)),
        out_shape=jax.ShapeDtypeStruct((ROWS, D), BF16),
        compiler_params=_cparams(("parallel",)),
        name="modulate",
    )(*x_args, mods, mods)


def _ln_kernel(z_ref, g_ref, b_ref, *rest, emit_u):
    z = z_ref[...]
    mu = jnp.mean(z, axis=-1, keepdims=True)
    zc = z - mu
    var = jnp.mean(zc * zc, axis=-1, keepdims=True)
    xn = zc * lax.rsqrt(var + LN_EPS) * g_ref[...] + b_ref[...]
    if emit_u:
        shift_ref, scale_ref, xo_ref, u_ref = rest
        xo_ref[...] = xn
        u_ref[...] = (xn * (1.0 + scale_ref[...]) + shift_ref[...]).astype(u_ref.dtype)
    else:
        (xo_ref,) = rest
        xo_ref[...] = xn


def _ln(z, mods, ln_g, ln_b, l, sub, nxt, row0=0, n_rows=ROWS):
    tr = 512
    row = pl.BlockSpec((tr, D), lambda i: (i, 0))
    vec = pl.BlockSpec((None, None, 1, D), lambda i: (l, sub, 0, 0))
    in_specs = [pl.BlockSpec((tr, D), lambda i: (row0 // tr + i, 0)), vec, vec]
    args = [z, ln_g, ln_b]
    emit_u = nxt is not None
    if emit_u:
        assert row0 == 0 and n_rows == ROWS
        in_specs += [_mod_spec(tr, nxt[0], nxt[1], 0), _mod_spec(tr, nxt[0], nxt[1], 1)]
        args += [mods, mods]
        out_specs = [row, row]
        out_shape = [jax.ShapeDtypeStruct((ROWS, D), F32), jax.ShapeDtypeStruct((ROWS, D), BF16)]
    else:
        out_specs = row
        out_shape = jax.ShapeDtypeStruct((n_rows, D), F32)
    return pl.pallas_call(
        functools.partial(_ln_kernel, emit_u=emit_u),
        grid=(n_rows // tr,),
        in_specs=in_specs, out_specs=out_specs, out_shape=out_shape,
        compiler_params=_cparams(("parallel",)),
        name="ln",
    )(*args)


def _mm_kernel(*refs, n_a, n_x, tm, y_scale, w_rows_out):
    o_ref = refs[-1]
    dims = (((1,), (1 if w_rows_out else 0,)), ((), ()))
    dot = lambda a_ref, w_ref: lax.dot_general(a_ref[...], w_ref[...], dims, preferred_element_type=F32)
    acc = dot(refs[0], refs[n_a])
    for r in range(1, n_a):
        acc += dot(refs[r], refs[n_a + r])
    if n_x:
        gate_ref = refs[2 * n_a + n_x]
        acc = ALPHA * _pick_rows(refs[2 * n_a:2 * n_a + n_x], tm) + (y_scale * gate_ref[...]) * acc
    o_ref[...] = acc.astype(o_ref.dtype)


def _mm(a_parts, w, tm, tn, out_dtype, name, widx=(), resid=None, w_rows_out=False):
    m, k = a_parts[0].shape
    assert all(a.shape == (m, k) for a in a_parts)
    n = w.shape[-2] if w_rows_out else w.shape[-1]
    lead = (None,) * len(widx)
    in_specs = [pl.BlockSpec((tm, k), lambda i, j: (i, 0)) for _ in a_parts]
    if w_rows_out:
        in_specs += [pl.BlockSpec(lead + (tn, k), lambda i, j, r=r: widx + (j, r)) for r in range(len(a_parts))]
    else:
        in_specs += [pl.BlockSpec(lead + (k, tn), lambda i, j, r=r: widx + (r, j)) for r in range(len(a_parts))]
    args = list(a_parts) + [w] * len(a_parts)
    n_x, y_scale = 0, 1.0
    if resid is not None:
        x, mods, l, sub, y_scale = resid
        x_specs, x_args = _row_block_specs(x, tm, tn, 2)
        n_x = len(x_args)
        in_specs += x_specs + [pl.BlockSpec((None, None, None, 1, tn),
                                            lambda i, j: (l, sub * 3 + 2, _group_of_block(i, tm), 0, j))]
        args += x_args + [mods]
    return pl.pallas_call(
        functools.partial(_mm_kernel, n_a=len(a_parts), n_x=n_x, tm=tm, y_scale=y_scale, w_rows_out=w_rows_out),
        grid=(m // tm, n // tn),
        in_specs=in_specs,
        out_specs=pl.BlockSpec((tm, tn), lambda i, j: (i, j)),
        out_shape=jax.ShapeDtypeStruct((m, n), out_dtype),
        compiler_params=_cparams(("parallel", "parallel")),
        name=name,
    )(*args)


def _swiglu_kernel(*refs, cast_plan, steps_per_col, n_steps):
    n_cast = len(cast_plan)
    a_ref, wg_ref, wu_ref = refs[:3]
    cast_in = refs[3:3 + n_cast]
    o_ref = refs[3 + n_cast]
    cast_out = refs[4 + n_cast:4 + 2 * n_cast]
    wgb_ref, wub_ref = refs[-2:]

    @pl.when(pl.program_id(1) == 0)
    def _():
        wgb_ref[...] = wg_ref[...].astype(BF16)
        wub_ref[...] = wu_ref[...].astype(BF16)

    step = pl.program_id(0) * steps_per_col + pl.program_id(1)
    for src, dst, (last, zeros) in zip(cast_in, cast_out, cast_plan):
        def cast(src=src, dst=dst, zeros=zeros):
            w = src[...].astype(BF16)
            if zeros:
                is_zero = functools.reduce(jnp.logical_or, [step == z for z in zeros])
                w = jnp.where(is_zero, jnp.zeros_like(w), w)
            dst[...] = w

        if last == n_steps - 1:
            cast()
        else:
            pl.when(step <= last)(cast)

    a = a_ref[...]
    g = jnp.dot(a, wgb_ref[...], preferred_element_type=F32)
    u = jnp.dot(a, wub_ref[...], preferred_element_type=F32)
    o_ref[...] = (g * jax.nn.sigmoid(g) * u).astype(o_ref.dtype)


class _CastJob(NamedTuple):
    w: jax.Array
    idx: tuple = ()
    dst_rows: int = 0
    src_block: Optional[Callable] = None
    zero_blocks: tuple = ()
    slice_rows: int = CAST_ROWS


def _swiglu_in(u, w_in, l, j, jobs):
    tm, tn = 1024, 256
    nb = D_FF // tn
    mb = ROWS // tm
    in_specs = [pl.BlockSpec((tm, D), lambda n, m: (m, 0)),
                pl.BlockSpec((None, None, D, tn), lambda n, m: (l, j, 0, n)),
                pl.BlockSpec((None, None, D, tn), lambda n, m: (l, j, 0, nb + n))]
    out_specs = [pl.BlockSpec((tm, tn), lambda n, m: (m, n))]
    out_shape = [jax.ShapeDtypeStruct((ROWS, D_FF), BF16)]
    plan = []
    for job in jobs:
        rows = job.dst_rows or job.w.shape[-2]
        wr = job.slice_rows
        assert job.w.shape[-1] == D and rows % wr == 0 and rows // wr <= nb * mb
        plan.append((rows // wr - 1, tuple(job.zero_blocks)))
        dst = lambda n, m, last=rows // wr - 1: jnp.minimum(n * mb + m, last)
        src = (lambda b: b) if job.src_block is None else job.src_block
        in_specs.append(pl.BlockSpec((None,) * len(job.idx) + (wr, D),
                                     lambda n, m, job=job, dst=dst, src=src: job.idx + (src(dst(n, m)), 0)))
        out_specs.append(pl.BlockSpec((wr, D), lambda n, m, dst=dst: (dst(n, m), 0)))
        out_shape.append(jax.ShapeDtypeStruct((rows, D), BF16))
    return pl.pallas_call(
        functools.partial(_swiglu_kernel, cast_plan=tuple(plan), steps_per_col=mb, n_steps=nb * mb),
        grid=(nb, mb),
        in_specs=in_specs, out_specs=out_specs, out_shape=out_shape,
        scratch_shapes=[pltpu.VMEM((D, tn), BF16), pltpu.VMEM((D, tn), BF16)],
        compiler_params=_cparams(("arbitrary", "arbitrary")),
        name="swiglu_in",
    )(u, w_in, w_in, *[job.w for job in jobs])


def _dft_tables(n):
    t = np.arange(n)
    ang = 2.0 * np.pi * ((t[:, None] * t[None, :]) % n) / n
    return np.cos(ang).astype(np.float32), np.sin(ang).astype(np.float32)


def _even_tables():
    icnt = np.zeros((2, 4, RB, 1), np.float32)
    for kind, n in enumerate((PROMPT_LEN, SAMPLE_LEN)):
        pos = np.arange(RB) % n
        for gi, w in enumerate(POOL_WINDOWS):
            icnt[kind, gi, :, 0] = 1.0 / (np.clip(pos + w // 2, 0, n) - np.clip(pos - w // 2, 0, n))
    pos_tables = []
    for n in (PROMPT_LEN, SAMPLE_LEN):
        c, s = _dft_tables(n)
        pos_tables += [c * np.float32((n * GW) ** -0.5), s * np.float32((n * GW) ** -0.5)]
    return (icnt,) + tuple(pos_tables) + _dft_tables(GW)


def _window_sum(x, half, seq_len):
    n = x.shape[0]
    pos = lax.broadcasted_iota(jnp.int32, x.shape, 0) % seq_len
    down = lambda y, k: jnp.where(pos >= k, pltpu.roll(y, k, 0), 0.0)
    up = lambda y, k: jnp.where(pos < seq_len - k, pltpu.roll(y, n - k, 0), 0.0)
    trail = lead = x
    k = 1
    while k < half:
        trail = trail + down(trail, k)
        lead = lead + up(lead, k)
        k *= 2
    return down(trail, 1) + lead


def _even_mid_kernel(p_ref, icnt_ref, pw_ref, ps_ref, cnp_ref, snp_ref, cns_ref, sns_ref, cc_ref, sc_ref, o_ref):
    j = pl.program_id(1)
    is_prompt = pl.program_id(0) * RB < PROMPT_ROWS

    def pool(half, seq_len):
        x = p_ref[...]
        pooled = _window_sum(x, half, seq_len) * icnt_ref[...] - x
        y = jnp.dot(pooled.astype(BF16), pw_ref[...], preferred_element_type=F32)
        o_ref[...] = (y * ps_ref[...]).astype(o_ref.dtype)

    def fourier(cn_ref, sn_ref, seq_len):
        x = p_ref[...].astype(BF16)
        xc = jnp.dot(x, cc_ref[...], preferred_element_type=F32).astype(BF16)
        xs = jnp.dot(x, sc_ref[...], preferred_element_type=F32).astype(BF16)
        for s in range(RB // seq_len):
            rs = slice(s * seq_len, (s + 1) * seq_len)
            y = (jnp.dot(cn_ref[...], xc[rs], preferred_element_type=F32)
                 - jnp.dot(sn_ref[...], xs[rs], preferred_element_type=F32))
            o_ref[rs, :] = y.astype(o_ref.dtype)

    for gi, w in enumerate(POOL_WINDOWS):
        pl.when((j == gi) & is_prompt)(functools.partial(pool, w // 2, PROMPT_LEN))
        pl.when((j == gi) & jnp.logical_not(is_prompt))(functools.partial(pool, w // 2, SAMPLE_LEN))
    pl.when((j >= 4) & is_prompt)(functools.partial(fourier, cnp_ref, snp_ref, PROMPT_LEN))
    pl.when((j >= 4) & jnp.logical_not(is_prompt))(functools.partial(fourier, cns_ref, sns_ref, SAMPLE_LEN))


def _even_mid(p, pool_w, pool_scale):
    icnt, *dft = _even_tables()
    cnp, snp, cns, sns, cc, sc = (jnp.asarray(t).astype(BF16) for t in dft)
    kind = lambda i: jnp.where(i * RB < PROMPT_ROWS, 0, 1)
    grp = lambda j: jnp.minimum(j, 3)
    whole = lambda t: pl.BlockSpec(t.shape, lambda i, j: (0, 0))
    return pl.pallas_call(
        _even_mid_kernel,
        grid=(ROWS // RB, 8),
        in_specs=[pl.BlockSpec((RB, GW), lambda i, j: (i, j)),
                  pl.BlockSpec((None, None, RB, 1), lambda i, j: (kind(i), grp(j), 0, 0)),
                  pl.BlockSpec((None, GW, GW), lambda i, j: (grp(j), 0, 0)),
                  pl.BlockSpec((None, 1, GW), lambda i, j: (grp(j), 0, 0)),
                  whole(cnp), whole(snp), whole(cns), whole(sns), whole(cc), whole(sc)],
        out_specs=pl.BlockSpec((RB, GW), lambda i, j: (i, j)),
        out_shape=jax.ShapeDtypeStruct((ROWS, D), BF16),
        compiler_params=_cparams(("parallel", "arbitrary")),
        name="even_mid",
    )(p, jnp.asarray(icnt), pool_w, pool_scale, cnp, snp, cns, sns, cc, sc)


def _head_sum_matrix(width, value):
    h = np.arange(width) // HEAD
    return jnp.asarray((h[:, None] == h[None, :]).astype(np.float32) * value).astype(BF16)


def _split_bf16(x, n):
    parts = []
    for _ in range(n - 1):
        h = x.astype(BF16)
        parts.append(h)
        x = x - h.astype(F32)
    parts.append(x.astype(BF16))
    return parts


def _head_sum(x, e):
    hi, lo = _split_bf16(x, 2)
    return jnp.dot(hi, e, preferred_element_type=F32) + jnp.dot(lo, e, preferred_element_type=F32)


def _token_shift(x, mu, seq_len):
    n = x.shape[0]
    pos = lax.broadcasted_iota(jnp.int32, x.shape, 0) % seq_len
    prev = jnp.where(pos == 0, 0.0, pltpu.roll(x, 1, 0))
    nxt = jnp.where(pos == seq_len - 1, 0.0, pltpu.roll(x, n - 1, 0))
    return x + (0.5 * (prev + nxt) - x) * mu


def _rwkv_prep_kernel(pr_ref, pk_ref, pv_ref, pg_ref, pw_ref, pa_ref,
                      mr_ref, mk_ref, mv_ref, mg_ref, mw_ref, ma_ref,
                      w0_ref, w2_ref, a0_ref, aw2_ref, gw2_ref, kk_ref, ka_ref, e_ref,
                      r_o, k_o, v_o, kk_o, kka_o, lw_o, g_o, sg_ref, tw_ref, al_ref):
    i = pl.program_id(0)

    def body(seq_len):
        @pl.when(pl.program_id(1) == 0)
        def _():
            sg_ref[...] = jax.nn.sigmoid(_token_shift(pg_ref[...], mg_ref[...], seq_len)).astype(BF16)
            tw_ref[...] = jnp.tanh(_token_shift(pw_ref[...], mw_ref[...], seq_len)).astype(BF16)
            al_ref[...] = _token_shift(pa_ref[...], ma_ref[...], seq_len).astype(BF16)

        r = _token_shift(pr_ref[...], mr_ref[...], seq_len)
        k = _token_shift(pk_ref[...], mk_ref[...], seq_len)
        v = _token_shift(pv_ref[...], mv_ref[...], seq_len)
        for d in range(2):
            x = w0_ref[d] + jnp.dot(tw_ref[:, d * DECAY_LORA:(d + 1) * DECAY_LORA], w2_ref[d],
                                    preferred_element_type=F32)
            z = -x
            softplus = jnp.maximum(z, 0.0) + jnp.log(1.0 + jnp.exp(-jnp.abs(z)))
            lw_o[d] = -jnp.exp(-softplus - 0.5)
        a = jax.nn.sigmoid(a0_ref[...] + jnp.dot(al_ref[...], aw2_ref[...], preferred_element_type=F32))
        g_o[...] = jnp.dot(sg_ref[...], gw2_ref[...], preferred_element_type=F32)
        kk = k * kk_ref[...]
        ss = _head_sum(kk * kk, e_ref[...])
        kk = kk * lax.rsqrt(ss + 1e-12)
        r_o[...] = r
        k_o[...] = k * (1.0 + (a - 1.0) * ka_ref[...])
        v_o[...] = v
        kk_o[...] = kk
        kka_o[...] = kk * a

    @pl.when(i * RB < PROMPT_ROWS)
    def _():
        body(PROMPT_LEN)

    @pl.when(i * RB >= PROMPT_ROWS)
    def _():
        body(SAMPLE_LEN)


def _rwkv_prep(p, mu, w0, w2, a0, aw2, gw2, k_k, k_a):
    cw = 256
    ncb = RWKV_W // cw
    e = _head_sum_matrix(cw, 1.0)
    pcol = lambda base, width: pl.BlockSpec((RB, width), lambda i, c: (i, base // width))
    pch = lambda base: pl.BlockSpec((RB, cw), lambda i, c: (i, base // cw + c))
    mcol = lambda base, width: pl.BlockSpec((1, width), lambda i, c: (0, base // width))
    mch = lambda base: pl.BlockSpec((1, cw), lambda i, c: (0, base // cw + c))
    vch = pl.BlockSpec((1, cw), lambda i, c: (0, c))
    out = pl.BlockSpec((RB, cw), lambda i, c: (i, c))
    sds = jax.ShapeDtypeStruct((ROWS, RWKV_W), F32)
    return pl.pallas_call(
        _rwkv_prep_kernel,
        grid=(ROWS // RB, ncb),
        in_specs=[pch(C_R), pch(C_K), pch(C_V), pcol(C_G, GATE_PAD), pcol(C_W, 2 * DECAY_LORA), pcol(C_A, ICLR_LORA),
                  mch(C_R), mch(C_K), mch(C_V), mcol(C_G, GATE_PAD), mcol(C_W, 2 * DECAY_LORA), mcol(C_A, ICLR_LORA),
                  pl.BlockSpec((2, 1, cw), lambda i, c: (0, 0, c)),
                  pl.BlockSpec((2, DECAY_LORA, cw), lambda i, c: (0, 0, c)),
                  vch,
                  pl.BlockSpec((ICLR_LORA, cw), lambda i, c: (0, c)),
                  pl.BlockSpec((GATE_PAD, cw), lambda i, c: (0, c)),
                  vch, vch,
                  pl.BlockSpec((cw, cw), lambda i, c: (0, 0))],
        out_specs=[out, out, out, out, out,
                   pl.BlockSpec((2, RB, cw), lambda i, c: (0, i, c)), out],
        out_shape=[sds, sds, sds, sds, sds, jax.ShapeDtypeStruct((2, ROWS, RWKV_W), F32), sds],
        scratch_shapes=[pltpu.VMEM((RB, GATE_PAD), BF16), pltpu.VMEM((RB, 2 * DECAY_LORA), BF16),
                        pltpu.VMEM((RB, ICLR_LORA), BF16)],
        compiler_params=_cparams(("parallel", "arbitrary")),
        name="rwkv_prep",
    )(p, p, p, p, p, p, mu, mu, mu, mu, mu, mu, w0, w2, a0, aw2, gw2, k_k, k_a, e)


def _mmp(a, b, nt=False):
    dims = (((1,), (1 if nt else 0,)), ((), ()))
    return lax.dot_general(a.astype(BF16), b.astype(BF16), dims, preferred_element_type=F32)


def _rwkv_scan_kernel(*refs, has_init, pairs):
    in_refs = (refs[0:6], refs[6:12])
    s0_ref = refs[12] if has_init else None
    y_refs = refs[-4:-2]
    s_ref, sp_ref = refs[-2:]
    c_idx = pl.program_id(1)
    n_chunks = pl.num_programs(1)
    L = CHUNK
    W2 = 2 * HEAD

    @pl.when(c_idx == 0)
    def _():
        sp_ref[...] = jnp.zeros_like(sp_ref)
        if has_init:
            for d in range(2):
                for p in range(pairs):
                    sp_ref[d * pairs + p, 0:HEAD, 0:HEAD] = s0_ref[d, 2 * p]
                    sp_ref[d * pairs + p, HEAD:W2, HEAD:W2] = s0_ref[d, 2 * p + 1]

    row = lax.broadcasted_iota(jnp.int32, (L, W2), 0)
    lane = lax.broadcasted_iota(jnp.int32, (L, W2), 1)
    col = lane % HEAD
    first = lane < HEAD
    incl_d = (col <= row, col >= row)
    strict_d = (col < row, col > row)
    eye = (col == row).astype(F32)
    blk8 = (row // 8) == (col // 8)
    off = [((row // (2 * b)) == (col // (2 * b))) & ((row // b) != (col // b)) for b in (8, 16, 32)]
    row_s = lax.broadcasted_iota(jnp.int32, (W2, W2), 0)
    lane_s = lax.broadcasted_iota(jnp.int32, (W2, W2), 1)
    diag_blocks = (row_s < HEAD) == (lane_s < HEAD)
    trow = lax.broadcasted_iota(jnp.int32, (L, L), 0)
    tcol = lax.broadcasted_iota(jnp.int32, (L, L), 1)
    tri_d = ((tcol <= trow).astype(BF16), (tcol >= trow).astype(BF16))

    first_b = lane.astype(F32).astype(BF16) < HEAD

    def bd(x):
        xb = x.astype(BF16)
        zero = jnp.zeros_like(xb)
        return jnp.concatenate([jnp.where(first_b, xb, zero), jnp.where(first_b, zero, xb)], axis=0)

    ps = [slice(p * W2, (p + 1) * W2) for p in range(pairs)]
    each = lambda f, *xs: [f(*z) for z in zip(*xs)]
    ar, bt, kt, v, g_last = [], [], [], [], []
    for d in range(2):
        lw_ref, r_ref, k_ref, v_ref, a_ref, b_ref = in_refs[d]
        lw = lw_ref[...]
        c = sum(jnp.dot(tri_d[d], t, preferred_element_type=F32) for t in _split_bf16(lw, 3))
        g_in = jnp.exp(c)
        g_inv = jnp.exp(-c)
        at_all = a_ref[...] * jnp.exp(c - lw)
        rt_all = r_ref[...] * g_in
        bt_all = b_ref[...] * g_inv
        kt_all = k_ref[...] * g_inv
        v_all = v_ref[...]
        g_end = g_in[0:1, :] if d else g_in[L - 1:L, :]
        ar += [jnp.concatenate([at_all[:, s], rt_all[:, s]], axis=0) for s in ps]
        bt += [bt_all[:, s] for s in ps]
        kt += [kt_all[:, s] for s in ps]
        v += [v_all[:, s] for s in ps]
        g_last += [g_end[:, s] for s in ps]
    n_inst = 2 * pairs
    strict = [strict_d[i // pairs] for i in range(n_inst)]
    incl = [incl_d[i // pairs] for i in range(n_inst)]
    s0 = [sp_ref[i] for i in range(n_inst)]
    sc = each(lambda x, b, k, s: _mmp(x, jnp.concatenate([bd(b), bd(k), s.astype(BF16)], axis=0), nt=True),
              ar, bt, kt, s0)
    npq = [x[:, 0:W2] for x in sc]
    mq = [x[:, W2:2 * W2] for x in sc]
    ws = [x[:, 2 * W2:3 * W2] for x in sc]
    n_mat = each(lambda m, x: jnp.where(m, x[:L], 0.0), strict, npq)
    m_mat = each(lambda m, x: jnp.where(m, x[:L], 0.0), strict, mq)
    p_mat = each(lambda m, x: jnp.where(m, x[L:], 0.0), incl, npq)
    q_mat = each(lambda m, x: jnp.where(m, x[L:], 0.0), incl, mq)
    mqv = each(lambda m, q, y: _mmp(jnp.concatenate([m, q], axis=0), bd(y)), m_mat, q_mat, v)
    w = each(lambda x, z: x[:L] + z[:L], ws, mqv)
    n8 = [jnp.where(blk8, x, 0.0) for x in n_mat]
    n8_2 = each(lambda x: _mmp(x, bd(x)), n8)
    n8_4 = each(lambda x: _mmp(x, bd(x)), n8_2)
    t = each(lambda x, y: _mmp(eye - x, bd(eye + y)), n8, n8_2)
    t = each(lambda x, y: _mmp(x, bd(eye + y)), t, n8_4)
    for m in off:
        x = each(lambda n, y: _mmp(jnp.where(m, n, 0.0), bd(y)), n_mat, t)
        t = each(lambda y, z: y - _mmp(y, bd(z)), t, x)
    u = each(lambda x, y: -_mmp(x, bd(y)), t, w)
    y = each(lambda x, z, pm, uu: x[L:] + z[L:] + _mmp(pm, bd(uu)), ws, mqv, p_mat, u)
    for i in range(n_inst):
        y_refs[i // pairs][:, ps[i % pairs]] = y[i]
    upd = each(lambda uu, vv, b, k: _mmp(jnp.concatenate([uu, vv], axis=0).T,
                                         jnp.concatenate([b, k], axis=0)), u, v, bt, kt)
    s_new = each(lambda s, x, g: (s + jnp.where(diag_blocks, x, 0.0)) * g, s0, upd, g_last)
    for i in range(n_inst):
        sp_ref[i] = s_new[i]

    @pl.when(c_idx == n_chunks - 1)
    def _():
        for i in range(n_inst):
            d, p = divmod(i, pairs)
            s_ref[d, 2 * p] = s_new[i][0:HEAD, 0:HEAD]
            s_ref[d, 2 * p + 1] = s_new[i][HEAD:W2, HEAD:W2]


def _rwkv_scan(lw, r, k, v, a, b, s0, y_prev, *, n_seq, seq_len, row0):
    pairs = N_HEADS // 2
    nc = seq_len // CHUNK

    def rows(d):
        return lambda s, c: row0 // CHUNK + s * nc + (nc - 1 - c if d else c)

    in_specs, args = [], []
    for d in range(2):
        in_specs.append(pl.BlockSpec((None, CHUNK, RWKV_W), lambda s, c, d=d: (d, rows(d)(s, c), 0)))
        in_specs += [pl.BlockSpec((CHUNK, RWKV_W), lambda s, c, d=d: (rows(d)(s, c), 0))] * 5
        args += [lw, r, k, v, a, b]
    if s0 is not None:
        in_specs.append(pl.BlockSpec((None, 2, N_HEADS, HEAD, HEAD), lambda s, c: (s, 0, 0, 0, 0)))
        args.append(s0)
    aliases = {}
    if y_prev is not None:
        aliases = {len(args): 0, len(args) + 1: 1}
        in_specs += [pl.BlockSpec(memory_space=pl.ANY)] * 2
        args += list(y_prev)
    y_sds = jax.ShapeDtypeStruct((ROWS, RWKV_W), F32)
    return pl.pallas_call(
        functools.partial(_rwkv_scan_kernel, has_init=s0 is not None, pairs=pairs),
        grid=(n_seq, nc),
        in_specs=in_specs,
        out_specs=[pl.BlockSpec((CHUNK, RWKV_W), lambda s, c: (rows(0)(s, c), 0)),
                   pl.BlockSpec((CHUNK, RWKV_W), lambda s, c: (rows(1)(s, c), 0)),
                   pl.BlockSpec((None, 2, N_HEADS, HEAD, HEAD), lambda s, c: (s, 0, 0, 0, 0))],
        out_shape=[y_sds, y_sds, jax.ShapeDtypeStruct((n_seq, 2, N_HEADS, HEAD, HEAD), F32)],
        scratch_shapes=[pltpu.VMEM((2 * pairs, 2 * HEAD, 2 * HEAD), F32)],
        input_output_aliases=aliases,
        compiler_params=_cparams(("parallel", "arbitrary")),
        name="rwkv_scan",
    )(*args)


def _rwkv_post_kernel(yf_ref, yb_ref, r_ref, k_ref, v_ref, g_ref, gg_ref, gb_ref, rk_ref, e_ref, o_ref):
    y = yf_ref[...] + yb_ref[...]
    e = e_ref[...]
    mu = _head_sum(y, e)
    yc = y - mu
    var = _head_sum(yc * yc, e)
    yn = yc * lax.rsqrt(var + GN_EPS) * gg_ref[...] + gb_ref[...]
    bonus = _head_sum(r_ref[...] * k_ref[...] * rk_ref[...], e) * float(HEAD) * v_ref[...]
    o_ref[...] = ((yn + bonus) * g_ref[...]).astype(o_ref.dtype)


def _rwkv_post(yf, yb, r, k, v, g, gn_g, gn_b, r_k):
    tr, cw = 512, 512
    e = _head_sum_matrix(cw, 1.0 / HEAD)
    blk = pl.BlockSpec((tr, cw), lambda i, c: (i, c))
    vec = pl.BlockSpec((1, cw), lambda i, c: (0, c))
    return pl.pallas_call(
        _rwkv_post_kernel,
        grid=(ROWS // tr, RWKV_W // cw),
        in_specs=[blk, blk, blk, blk, blk, blk, vec, vec, vec, pl.BlockSpec((cw, cw), lambda i, c: (0, 0))],
        out_specs=blk,
        out_shape=jax.ShapeDtypeStruct((ROWS, RWKV_W), BF16),
        compiler_params=_cparams(("parallel", "parallel")),
        name="rwkv_post",
    )(yf, yb, r, k, v, g, gn_g, gn_b, r_k, e)


def _rms(x, gain):
    return x * lax.rsqrt(jnp.mean(x * x, axis=-1, keepdims=True) + 1e-6) * gain


def _rope(x, cos, sin_signed):
    lane = lax.broadcasted_iota(jnp.int32, x.shape, 1)
    quarter = HEAD_DIM // 4
    partner = jnp.where(lane % (2 * quarter) < quarter,
                        pltpu.roll(x, HEAD_DIM - quarter, 1), pltpu.roll(x, quarter, 1))
    return x * cos + partner * sin_signed


def _rope_tables():
    n = SAMPLE_LEN
    quarter = HEAD_DIM // 4
    row = (np.arange(n) // GRID_W).astype(np.float64)
    col = (np.arange(n) % GRID_W).astype(np.float64)
    inv_freq = ROPE_THETA ** (-np.arange(quarter, dtype=np.float64) / quarter)
    ar = row[:, None] * inv_freq[None, :]
    ac = col[:, None] * inv_freq[None, :]
    cos = np.concatenate([np.cos(ar), np.cos(ar), np.cos(ac), np.cos(ac)], axis=1)
    sin = np.concatenate([-np.sin(ar), np.sin(ar), -np.sin(ac), np.sin(ac)], axis=1)
    return cos.astype(np.float32), sin.astype(np.float32)


def _attn_prompt_kernel(q_ref, k_ref, v_ref, qn_ref, kn_ref, o_ref, ko_ref, vo_ref):
    scale = HEAD_DIM ** -0.5
    kn = _rms(k_ref[...], kn_ref[...])
    v = v_ref[...]
    ko_ref[...] = kn
    vo_ref[...] = v
    kb = kn.astype(BF16)
    vb = v.astype(BF16)
    n = q_ref.shape[0]
    heads = [slice(j * HEAD_DIM, (j + 1) * HEAD_DIM) for j in range(ATT_W // KV_W)]
    q = jnp.concatenate([(_rms(q_ref[:, hs], qn_ref[...]) * scale).astype(BF16) for hs in heads], axis=0)
    s = lax.dot_general(q, kb, (((1,), (1,)), ((), ())), preferred_element_type=F32)
    e = jnp.exp(s - jnp.max(s, axis=-1, keepdims=True))
    inv = 1.0 / jnp.sum(e, axis=-1, keepdims=True)
    o = (jnp.dot(e.astype(BF16), vb, preferred_element_type=F32) * inv).astype(o_ref.dtype)
    for j, hs in enumerate(heads):
        o_ref[:, hs] = o[j * n:(j + 1) * n]


def _attn_prompt(p, q_norm, k_norm):
    n = PROMPT_LEN
    qw = ATT_W // N_KV
    return pl.pallas_call(
        _attn_prompt_kernel,
        grid=(N_PROMPT_SEQ, N_KV),
        in_specs=[pl.BlockSpec((n, qw), lambda s, g: (s, C_Q // qw + g)),
                  pl.BlockSpec((n, HEAD_DIM), lambda s, g: (s, C_AK // HEAD_DIM + g)),
                  pl.BlockSpec((n, HEAD_DIM), lambda s, g: (s, C_AV // HEAD_DIM + g)),
                  pl.BlockSpec((1, HEAD_DIM), lambda s, g: (0, 0)),
                  pl.BlockSpec((1, HEAD_DIM), lambda s, g: (0, 0))],
        out_specs=[pl.BlockSpec((n, qw), lambda s, g: (s, g)),
                   pl.BlockSpec((n, HEAD_DIM), lambda s, g: (s, g)),
                   pl.BlockSpec((n, HEAD_DIM), lambda s, g: (s, g))],
        out_shape=[jax.ShapeDtypeStruct((ROWS, ATT_W), BF16),
                   jax.ShapeDtypeStruct((PROMPT_ROWS, KV_W), F32),
                   jax.ShapeDtypeStruct((PROMPT_ROWS, KV_W), F32)],
        compiler_params=_cparams(("parallel", "parallel")),
        name="attn_prompt",
    )(p, p, p, q_norm, k_norm)


def _attn_sample_kernel(q_ref, k_ref, v_ref, ck_ref, cv_ref, qn_ref, kn_ref, cos_ref, sin_ref, _o_prev, o_ref):
    scale = HEAD_DIM ** -0.5
    tq = 256
    cos = cos_ref[...]
    sin = sin_ref[...]
    ks = _rope(_rms(k_ref[...], kn_ref[...]), cos, sin).astype(BF16)
    vs = v_ref[...].astype(BF16)
    kc = ck_ref[...].astype(BF16)
    vc = cv_ref[...].astype(BF16)
    nt = (((1,), (1,)), ((), ()))
    for j in range(ATT_W // KV_W):
        hs = slice(j * HEAD_DIM, (j + 1) * HEAD_DIM)
        for qb in range(SAMPLE_LEN // tq):
            rs = slice(qb * tq, (qb + 1) * tq)
            q = (_rope(_rms(q_ref[rs, hs], qn_ref[...]), cos[rs], sin[rs]) * scale).astype(BF16)
            s1 = lax.dot_general(q, kc, nt, preferred_element_type=F32)
            s2 = lax.dot_general(q, ks, nt, preferred_element_type=F32)
            m = jnp.maximum(jnp.max(s1, axis=-1, keepdims=True), jnp.max(s2, axis=-1, keepdims=True))
            e1 = jnp.exp(s1 - m)
            e2 = jnp.exp(s2 - m)
            inv = 1.0 / (jnp.sum(e1, axis=-1, keepdims=True) + jnp.sum(e2, axis=-1, keepdims=True))
            o = (jnp.dot(e1.astype(BF16), vc, preferred_element_type=F32)
                 + jnp.dot(e2.astype(BF16), vs, preferred_element_type=F32))
            o_ref[rs, hs] = (o * inv).astype(o_ref.dtype)


def _attn_sample(p, cache_k, cache_v, q_norm, k_norm, o_prev):
    n = SAMPLE_LEN
    qw = ATT_W // N_KV
    r0 = PROMPT_ROWS // n
    cos, sin = _rope_tables()
    return pl.pallas_call(
        _attn_sample_kernel,
        grid=(N_SAMPLE_SEQ, N_KV),
        in_specs=[pl.BlockSpec((n, qw), lambda s, g: (r0 + s, C_Q // qw + g)),
                  pl.BlockSpec((n, HEAD_DIM), lambda s, g: (r0 + s, C_AK // HEAD_DIM + g)),
                  pl.BlockSpec((n, HEAD_DIM), lambda s, g: (r0 + s, C_AV // HEAD_DIM + g)),
                  pl.BlockSpec((PAST_LEN, HEAD_DIM), lambda s, g: (s, g)),
                  pl.BlockSpec((PAST_LEN, HEAD_DIM), lambda s, g: (s, g)),
                  pl.BlockSpec((1, HEAD_DIM), lambda s, g: (0, 0)),
                  pl.BlockSpec((1, HEAD_DIM), lambda s, g: (0, 0)),
                  pl.BlockSpec((n, HEAD_DIM), lambda s, g: (0, 0)),
                  pl.BlockSpec((n, HEAD_DIM), lambda s, g: (0, 0)),
                  pl.BlockSpec(memory_space=pl.ANY)],
        out_specs=pl.BlockSpec((n, qw), lambda s, g: (r0 + s, g)),
        out_shape=jax.ShapeDtypeStruct((ROWS, ATT_W), BF16),
        input_output_aliases={9: 0},
        compiler_params=_cparams(("parallel", "parallel")),
        name="attn_sample",
    )(p, p, p, cache_k, cache_v, q_norm, k_norm, jnp.asarray(cos), jnp.asarray(sin), o_prev)


_ODD_SEGMENTS = ((0, C_R, 6144), (7008, C_Q, ATT_W), (9056, C_AK, KV_W), (9568, C_AV, KV_W),
                 (6144, C_G, GATE_LORA), (6624, C_W, 2 * DECAY_LORA), (6880, C_A, ICLR_LORA))
_ODD_ZERO = ((C_G + GATE_LORA, GATE_PAD - GATE_LORA), (C_A + ICLR_LORA, ODD_COLS - C_A - ICLR_LORA))


def _pack_odd_job(wt):
    b = PACK_ROWS
    assert all(v % b == 0 for seg in _ODD_SEGMENTS + _ODD_ZERO for v in seg)
    runs = [(dst // b, (dst + width) // b, (src - dst) // b) for src, dst, width in _ODD_SEGMENTS]
    zeros = tuple(z for dst, width in _ODD_ZERO for z in range(dst // b, (dst + width) // b))

    def src_block(blk):
        out = jnp.zeros_like(blk)
        for lo, hi, delta in runs:
            out = jnp.where((blk >= lo) & (blk < hi), blk + delta, out)
        return out

    return _CastJob(wt, (), ODD_COLS, src_block, zeros, b)


def _pack_shift_mu(mu):
    z = lambda n: jnp.zeros((n,), mu.dtype)
    out = jnp.concatenate([mu[0:6144], z(C_G - C_Q), mu[6144:6144 + GATE_LORA], z(GATE_PAD - GATE_LORA),
                           mu[6624:6880], mu[6880:7008], z(ODD_COLS - C_A - ICLR_LORA)])
    return out[None, :]


def kernel(x_prompt, x_sample, cache_k, cache_v, state_rwkv, c, c_ctx, w_ada, b_ada, ln_g, ln_b, ffn_w_in,
           ffn_w_out, even_w_in, pool_w, pool_scale, even_w_out, odd_w_in, shift_mu, decay_w0, decay_w2,
           iclr_a0, iclr_w2, gate_w2, k_k, k_a, r_k, gn_g, gn_b, q_norm, k_norm, odd_w_out):
    x = (x_prompt.reshape(PROMPT_ROWS, D), x_sample.reshape(ROWS - PROMPT_ROWS, D))
    cond =jnp.concatenate([c_ctx[None, :], c, jnp.zeros((N_GROUPS - 1 - c.shape[0], D), F32)], axis=0)
    mods = _ada(cond, w_ada, b_ada[:, None, :])
    mods = mods.reshape(DEPTH, N_GROUPS, 3 * N_SUB, D).transpose(0, 2, 1, 3)[:, :, :, None, :]
    ln_g4 = ln_g[:, :, None, :]
    ln_b4 = ln_b[:, :, None, :]

    u = _modulate(x, mods, 0, 0)
    new_k = new_v = new_s = None
    for l in range(DEPTH):
        i = l // 2
        if l % 2 == 0:
            mixer_w = [_CastJob(even_w_in, (i,)), _CastJob(even_w_out, (i,))]
        else:
            mixer_w = [_pack_odd_job(jnp.swapaxes(odd_w_in[i], 0, 1)), _CastJob(odd_w_out, (i,))]
        h, w_out_b, *mixer_wb = _swiglu_in(u, ffn_w_in, l, 0, [_CastJob(ffn_w_out, (l, 0))] + mixer_w)
        z = _mm([h], w_out_b, 512, 512, F32, "ffn_out", resid=(x, mods, l, 0, 0.5))
        x, u = _ln(z, mods, ln_g4, ln_b4, l, 0, (l, 1))
        if l % 2 == 0:
            p = _mm([u], mixer_wb[0], 1024, 512, F32, "even_in")
            mid = _even_mid(p, pool_w[i].astype(BF16), pool_scale[i].reshape(4, 1, GW))
            z = _mm([mid], mixer_wb[1], 1024, 512, F32, "even_out", resid=(x, mods, l, 1, 1.0))
        else:
            p = _mm([u], mixer_wb[0], 1024, 512, F32, "odd_in", w_rows_out=True)
            gw2 = jnp.concatenate([gate_w2[i], jnp.zeros((GATE_PAD - GATE_LORA, RWKV_W), F32)], axis=0)
            r, k, v, kk, kka, lw, g = _rwkv_prep(
                p, _pack_shift_mu(shift_mu[i]), decay_w0[i][:, None, :], decay_w2[i].astype(BF16),
                iclr_a0[i][None, :], iclr_w2[i].astype(BF16), gw2.astype(BF16), k_k[i][None, :], k_a[i][None, :])
            yf, yb, s_prompt = _rwkv_scan(lw, r, k, v, kk, kka, None, None,
                                          n_seq=N_PROMPT_SEQ, seq_len=PROMPT_LEN, row0=0)
            yf, yb, _ = _rwkv_scan(lw, r, k, v, kk, kka, state_rwkv[:, i], (yf, yb),
                                   n_seq=N_SAMPLE_SEQ, seq_len=SAMPLE_LEN, row0=PROMPT_ROWS)
            y_rwkv = _rwkv_post(yf, yb, r, k, v, g, gn_g[i][None, :], gn_b[i][None, :],
                                r_k[i].reshape(1, RWKV_W))
            o_p, nk, nv = _attn_prompt(p, q_norm[i][None, :], k_norm[i][None, :])
            o = _attn_sample(p, cache_k[:, i].reshape(N_SAMPLE_SEQ * PAST_LEN, KV_W),
                             cache_v[:, i].reshape(N_SAMPLE_SEQ * PAST_LEN, KV_W),
                             q_norm[i][None, :], k_norm[i][None, :], o_p)
            z = _mm([y_rwkv, o], mixer_wb[1], 1024, 512, F32, "odd_out", resid=(x, mods, l, 1, 1.0))
            new_k = nk.reshape(N_PROMPT_SEQ, 1, PROMPT_LEN, N_KV, HEAD_DIM)
            new_v = nv.reshape(N_PROMPT_SEQ, 1, PROMPT_LEN, N_KV, HEAD_DIM)
            new_s = s_prompt[:, None]
        x, u = _ln(z, mods, ln_g4, ln_b4, l, 1, (l, 2))
        h, w_out_b = _swiglu_in(u, ffn_w_in, l, 1, [_CastJob(ffn_w_out, (l, 1))])
        z = _mm([h], w_out_b, 512, 512, F32, "ffn_out", resid=(x, mods, l, 2, 0.5))
        if l + 1 < DEPTH:
            x, u = _ln(z, mods, ln_g4, ln_b4, l, 2, (l + 1, 0))
        else:
            y_prompt = _ln(z, mods, ln_g4, ln_b4, l, 2, None, 0, PROMPT_ROWS)
            y_sample = _ln(z, mods, ln_g4, ln_b4, l, 2, None, PROMPT_ROWS, ROWS - PROMPT_ROWS)

    return (y_prompt.reshape(N_PROMPT_SEQ, PROMPT_LEN, D), y_sample.reshape(N_SAMPLE_SEQ, SAMPLE_LEN, D),
            new_k, new_v, new_s)
```

```python
import functools
from typing import Callable, NamedTuple, Optional

import numpy as np
import jax
import jax.numpy as jnp
from jax import lax
from jax.experimental import pallas as pl
from jax.experimental.pallas import tpu as pltpu

F32 = jnp.float32
BF16 = jnp.bfloat16

D = 4096
DEPTH = 2
N_SUB = 3
D_FF = 11008
N_PROMPT_SEQ = 16
PROMPT_LEN = 256
N_SAMPLE_SEQ = 4
SAMPLE_LEN = 1024
PAST_LEN = 512
GRID_W = 64
PROMPT_ROWS = N_PROMPT_SEQ * PROMPT_LEN
ROWS = PROMPT_ROWS + N_SAMPLE_SEQ * SAMPLE_LEN
RB = 1024
N_GROUPS = 8
POOL_WINDOWS = (2, 4, 8, 16)
GW = 512
RWKV_W = 2048
HEAD = 64
N_HEADS = 32
GATE_LORA = 480
GATE_PAD = 512
DECAY_LORA = 128
ICLR_LORA = 128
ATT_W = 2048
HEAD_DIM = 128
N_KV = 4
KV_W = N_KV * HEAD_DIM
ROPE_THETA = 10000.0
ALPHA = (2 * DEPTH) ** 0.25
LN_EPS = 1e-5
GN_EPS = 64e-5
CHUNK = 64
CAST_ROWS = 32
PACK_ROWS = 32

C_R, C_K, C_V, C_Q = 0, 2048, 4096, 6144
C_AK, C_AV, C_G, C_W, C_A = 8192, 8704, 9216, 9728, 9984
ODD_COLS = 10240

VMEM_LIMIT = 56 * 1024 * 1024


def _cparams(sem):
    return pltpu.CompilerParams(dimension_semantics=sem, vmem_limit_bytes=VMEM_LIMIT)


def _group_of_block(i, rows_per_block):
    r0 = i * rows_per_block
    return jnp.where(r0 < PROMPT_ROWS, 0, 1 + (r0 - PROMPT_ROWS) // SAMPLE_LEN)


def _ada_kernel(c_ref, w_ref, b_ref, o_ref):
    c = c_ref[...]
    s = (c * jax.nn.sigmoid(c)).astype(BF16)
    o_ref[...] = jnp.dot(s, w_ref[...].astype(BF16), preferred_element_type=F32) + b_ref[...]


def _ada(cond, w_ada, b_ada):
    tn = 512
    n = w_ada.shape[-1]
    return pl.pallas_call(
        _ada_kernel,
        grid=(DEPTH, n // tn),
        in_specs=[pl.BlockSpec((N_GROUPS, D), lambda l, j: (0, 0)),
                  pl.BlockSpec((None, D, tn), lambda l, j: (l, 0, j)),
                  pl.BlockSpec((None, 1, tn), lambda l, j: (l, 0, j))],
        out_specs=pl.BlockSpec((None, N_GROUPS, tn), lambda l, j: (l, 0, j)),
        out_shape=jax.ShapeDtypeStruct((DEPTH, N_GROUPS, n), F32),
        compiler_params=_cparams(("parallel", "parallel")),
        name="ada",
    )(cond, w_ada, b_ada)


def _mod_spec(tr, l, sub, kind):
    return pl.BlockSpec((None, None, None, 1, D),
                        lambda i: (l, sub * 3 + kind, _group_of_block(i, tr), 0, 0))


def _row_block_specs(x, tr, tn, rank):
    col = (lambda ij: ij[1]) if rank == 2 else (lambda ij: 0)
    if not isinstance(x, tuple):
        return [pl.BlockSpec((tr, tn), lambda *ij: (ij[0], col(ij)))], [x]
    nb = PROMPT_ROWS // tr
    return ([pl.BlockSpec((tr, tn), lambda *ij: (jnp.minimum(ij[0], nb - 1), col(ij))),
             pl.BlockSpec((tr, tn), lambda *ij: (jnp.maximum(ij[0] - nb, 0), col(ij)))], list(x))


def _pick_rows(x_refs, tr):
    if len(x_refs) == 1:
        return x_refs[0][...]
    return jnp.where(pl.program_id(0) * tr < PROMPT_ROWS, x_refs[0][...], x_refs[1][...])


def _modulate_kernel(*refs, n_x, tr):
    shift_ref, scale_ref, u_ref = refs[n_x:]
    x = _pick_rows(refs[:n_x], tr)
    u_ref[...] = (x * (1.0 + scale_ref[...]) + shift_ref[...]).astype(u_ref.dtype)


def _modulate(x, mods, l, sub):
    tr = 256
    x_specs, x_args = _row_block_specs(x, tr, D, 1)
    return pl.pallas_call(
        functools.partial(_modulate_kernel, n_x=len(x_args), tr=tr),
        grid=(ROWS // tr,),
        in_specs=x_specs + [_mod_spec(tr, l, sub, 0), _mod_spec(tr, l, sub, 1)],
        out_specs=pl.BlockSpec((tr, D), lambda i: (i, 0)),
        out_shape=jax.ShapeDtypeStruct((ROWS, D), BF16),
        compiler_params=_cparams(("parallel",)),
        name="modulate",
    )(*x_args, mods, mods)


def _ln_kernel(z_ref, g_ref, b_ref, *rest, emit_u):
    z = z_ref[...]
    mu = jnp.mean(z, axis=-1, keepdims=True)
    zc = z - mu
    var = jnp.mean(zc * zc, axis=-1, keepdims=True)
    xn = zc * lax.rsqrt(var + LN_EPS) * g_ref[...] + b_ref[...]
    if emit_u:
        shift_ref, scale_ref, xo_ref, u_ref = rest
        xo_ref[...] = xn
        u_ref[...] = (xn * (1.0 + scale_ref[...]) + shift_ref[...]).astype(u_ref.dtype)
    else:
        (xo_ref,) = rest
        xo_ref[...] = xn


def _ln(z, mods, ln_g, ln_b, l, sub, nxt, row0=0, n_rows=ROWS):
    tr = 512
    row = pl.BlockSpec((tr, D), lambda i: (i, 0))
    vec = pl.BlockSpec((None, None, 1, D), lambda i: (l, sub, 0, 0))
    in_specs = [pl.BlockSpec((tr, D), lambda i: (row0 // tr + i, 0)), vec, vec]
    args = [z, ln_g, ln_b]
    emit_u = nxt is not None
    if emit_u:
        assert row0 == 0 and n_rows == ROWS
        in_specs += [_mod_spec(tr, nxt[0], nxt[1], 0), _mod_spec(tr, nxt[0], nxt[1], 1)]
        args += [mods, mods]
        out_specs = [row, row]
        out_shape = [jax.ShapeDtypeStruct((ROWS, D), F32), jax.ShapeDtypeStruct((ROWS, D), BF16)]
    else:
        out_specs = row
        out_shape = jax.ShapeDtypeStruct((n_rows, D), F32)
    return pl.pallas_call(
        functools.partial(_ln_kernel, emit_u=emit_u),
        grid=(n_rows // tr,),
        in_specs=in_specs, out_specs=out_specs, out_shape=out_shape,
        compiler_params=_cparams(("parallel",)),
        name="ln",
    )(*args)


def _mm_kernel(*refs, n_a, n_x, tm, y_scale, w_rows_out):
    o_ref = refs[-1]
    dims = (((1,), (1 if w_rows_out else 0,)), ((), ()))
    dot = lambda a_ref, w_ref: lax.dot_general(a_ref[...], w_ref[...], dims, preferred_element_type=F32)
    acc = dot(refs[0], refs[n_a])
    for r in range(1, n_a):
        acc += dot(refs[r], refs[n_a + r])
    if n_x:
        gate_ref = refs[2 * n_a + n_x]
        acc = ALPHA * _pick_rows(refs[2 * n_a:2 * n_a + n_x], tm) + (y_scale * gate_ref[...]) * acc
    o_ref[...] = acc.astype(o_ref.dtype)


def _mm(a_parts, w, tm, tn, out_dtype, name, widx=(), resid=None, w_rows_out=False):
    m, k = a_parts[0].shape
    assert all(a.shape == (m, k) for a in a_parts)
    n = w.shape[-2] if w_rows_out else w.shape[-1]
    lead = (None,) * len(widx)
    in_specs = [pl.BlockSpec((tm, k), lambda i, j: (i, 0)) for _ in a_parts]
    if w_rows_out:
        in_specs += [pl.BlockSpec(lead + (tn, k), lambda i, j, r=r: widx + (j, r)) for r in range(len(a_parts))]
    else:
        in_specs += [pl.BlockSpec(lead + (k, tn), lambda i, j, r=r: widx + (r, j)) for r in range(len(a_parts))]
    args = list(a_parts) + [w] * len(a_parts)
    n_x, y_scale = 0, 1.0
    if resid is not None:
        x, mods, l, sub, y_scale = resid
        x_specs, x_args = _row_block_specs(x, tm, tn, 2)
        n_x = len(x_args)
        in_specs += x_specs + [pl.BlockSpec((None, None, None, 1, tn),
                                            lambda i, j: (l, sub * 3 + 2, _group_of_block(i, tm), 0, j))]
        args += x_args + [mods]
    return pl.pallas_call(
        functools.partial(_mm_kernel, n_a=len(a_parts), n_x=n_x, tm=tm, y_scale=y_scale, w_rows_out=w_rows_out),
        grid=(m // tm, n // tn),
        in_specs=in_specs,
        out_specs=pl.BlockSpec((tm, tn), lambda i, j: (i, j)),
        out_shape=jax.ShapeDtypeStruct((m, n), out_dtype),
        compiler_params=_cparams(("parallel", "parallel")),
        name=name,
    )(*args)


def _swiglu_kernel(*refs, cast_plan, steps_per_col, n_steps):
    n_cast = len(cast_plan)
    a_ref, wg_ref, wu_ref = refs[:3]
    cast_in = refs[3:3 + n_cast]
    o_ref = refs[3 + n_cast]
    cast_out = refs[4 + n_cast:4 + 2 * n_cast]
    wgb_ref, wub_ref = refs[-2:]

    @pl.when(pl.program_id(1) == 0)
    def _():
        wgb_ref[...] = wg_ref[...].astype(BF16)
        wub_ref[...] = wu_ref[...].astype(BF16)

    step = pl.program_id(0) * steps_per_col + pl.program_id(1)
    for src, dst, (last, zeros) in zip(cast_in, cast_out, cast_plan):
        def cast(src=src, dst=dst, zeros=zeros):
            w = src[...].astype(BF16)
            if zeros:
                is_zero = functools.reduce(jnp.logical_or, [step == z for z in zeros])
                w = jnp.where(is_zero, jnp.zeros_like(w), w)
            dst[...] = w

        if last == n_steps - 1:
            cast()
        else:
            pl.when(step <= last)(cast)

    a = a_ref[...]
    g = jnp.dot(a, wgb_ref[...], preferred_element_type=F32)
    u = jnp.dot(a, wub_ref[...], preferred_element_type=F32)
    o_ref[...] = (g * jax.nn.sigmoid(g) * u).astype(o_ref.dtype)


class _CastJob(NamedTuple):
    w: jax.Array
    idx: tuple = ()
    dst_rows: int = 0
    src_block: Optional[Callable] = None
    zero_blocks: tuple = ()
    slice_rows: int = CAST_ROWS


def _swiglu_in(u, w_in, l, j, jobs):
    tm, tn = 1024, 256
    nb = D_FF // tn
    mb = ROWS // tm
    in_specs = [pl.BlockSpec((tm, D), lambda n, m: (m, 0)),
                pl.BlockSpec((None, None, D, tn), lambda n, m: (l, j, 0, n)),
                pl.BlockSpec((None, None, D, tn), lambda n, m: (l, j, 0, nb + n))]
    out_specs = [pl.BlockSpec((tm, tn), lambda n, m: (m, n))]
    out_shape = [jax.ShapeDtypeStruct((ROWS, D_FF), BF16)]
    plan = []
    for job in jobs:
        rows = job.dst_rows or job.w.shape[-2]
        wr = job.slice_rows
        assert job.w.shape[-1] == D and rows % wr == 0 and rows // wr <= nb * mb
        plan.append((rows // wr - 1, tuple(job.zero_blocks)))
        dst = lambda n, m, last=rows // wr - 1: jnp.minimum(n * mb + m, last)
        src = (lambda b: b) if job.src_block is None else job.src_block
        in_specs.append(pl.BlockSpec((None,) * len(job.idx) + (wr, D),
                                     lambda n, m, job=job, dst=dst, src=src: job.idx + (src(dst(n, m)), 0)))
        out_specs.append(pl.BlockSpec((wr, D), lambda n, m, dst=dst: (dst(n, m), 0)))
        out_shape.append(jax.ShapeDtypeStruct((rows, D), BF16))
    return pl.pallas_call(
        functools.partial(_swiglu_kernel, cast_plan=tuple(plan), steps_per_col=mb, n_steps=nb * mb),
        grid=(nb, mb),
        in_specs=in_specs, out_specs=out_specs, out_shape=out_shape,
        scratch_shapes=[pltpu.VMEM((D, tn), BF16), pltpu.VMEM((D, tn), BF16)],
        compiler_params=_cparams(("arbitrary", "arbitrary")),
        name="swiglu_in",
    )(u, w_in, w_in, *[job.w for job in jobs])


def _dft_tables(n):
    t = np.arange(n)
    ang = 2.0 * np.pi * ((t[:, None] * t[None, :]) % n) / n
    return np.cos(ang).astype(np.float32), np.sin(ang).astype(np.float32)


def _even_tables():
    icnt = np.zeros((2, 4, RB, 1), np.float32)
    for kind, n in enumerate((PROMPT_LEN, SAMPLE_LEN)):
        pos = np.arange(RB) % n
        for gi, w in enumerate(POOL_WINDOWS):
            icnt[kind, gi, :, 0] = 1.0 / (np.clip(pos + w // 2, 0, n) - np.clip(pos - w // 2, 0, n))
    pos_tables = []
    for n in (PROMPT_LEN, SAMPLE_LEN):
        c, s = _dft_tables(n)
        pos_tables += [c * np.float32((n * GW) ** -0.5), s * np.float32((n * GW) ** -0.5)]
    return (icnt,) + tuple(pos_tables) + _dft_tables(GW)


def _window_sum(x, half, seq_len):
    n = x.shape[0]
    pos = lax.broadcasted_iota(jnp.int32, x.shape, 0) % seq_len
    down = lambda y, k: jnp.where(pos >= k, pltpu.roll(y, k, 0), 0.0)
    up = lambda y, k: jnp.where(pos < seq_len - k, pltpu.roll(y, n - k, 0), 0.0)
    trail = lead = x
    k = 1
    while k < half:
        trail = trail + down(trail, k)
        lead = lead + up(lead, k)
        k *= 2
    return down(trail, 1) + lead


def _even_mid_kernel(p_ref, icnt_ref, pw_ref, ps_ref, cnp_ref, snp_ref, cns_ref, sns_ref, cc_ref, sc_ref, o_ref):
    j = pl.program_id(1)
    is_prompt = pl.program_id(0) * RB < PROMPT_ROWS

    def pool(half, seq_len):
        x = p_ref[...]
        pooled = _window_sum(x, half, seq_len) * icnt_ref[...] - x
        y = jnp.dot(pooled.astype(BF16), pw_ref[...], preferred_element_type=F32)
        o_ref[...] = (y * ps_ref[...]).astype(o_ref.dtype)

    def fourier(cn_ref, sn_ref, seq_len):
        x = p_ref[...].astype(BF16)
        xc = jnp.dot(x, cc_ref[...], preferred_element_type=F32).astype(BF16)
        xs = jnp.dot(x, sc_ref[...], preferred_element_type=F32).astype(BF16)
        for s in range(RB // seq_len):
            rs = slice(s * seq_len, (s + 1) * seq_len)
            y = (jnp.dot(cn_ref[...], xc[rs], preferred_element_type=F32)
                 - jnp.dot(sn_ref[...], xs[rs], preferred_element_type=F32))
            o_ref[rs, :] = y.astype(o_ref.dtype)

    for gi, w in enumerate(POOL_WINDOWS):
        pl.when((j == gi) & is_prompt)(functools.partial(pool, w // 2, PROMPT_LEN))
        pl.when((j == gi) & jnp.logical_not(is_prompt))(functools.partial(pool, w // 2, SAMPLE_LEN))
    pl.when((j >= 4) & is_prompt)(functools.partial(fourier, cnp_ref, snp_ref, PROMPT_LEN))
    pl.when((j >= 4) & jnp.logical_not(is_prompt))(functools.partial(fourier, cns_ref, sns_ref, SAMPLE_LEN))


def _even_mid(p, pool_w, pool_scale):
    icnt, *dft = _even_tables()
    cnp, snp, cns, sns, cc, sc = (jnp.asarray(t).astype(BF16) for t in dft)
    kind = lambda i: jnp.where(i * RB < PROMPT_ROWS, 0, 1)
    grp = lambda j: jnp.minimum(j, 3)
    whole = lambda t: pl.BlockSpec(t.shape, lambda i, j: (0, 0))
    return pl.pallas_call(
        _even_mid_kernel,
        grid=(ROWS // RB, 8),
        in_specs=[pl.BlockSpec((RB, GW), lambda i, j: (i, j)),
                  pl.BlockSpec((None, None, RB, 1), lambda i, j: (kind(i), grp(j), 0, 0)),
                  pl.BlockSpec((None, GW, GW), lambda i, j: (grp(j), 0, 0)),
                  pl.BlockSpec((None, 1, GW), lambda i, j: (grp(j), 0, 0)),
                  whole(cnp), whole(snp), whole(cns), whole(sns), whole(cc), whole(sc)],
        out_specs=pl.BlockSpec((RB, GW), lambda i, j: (i, j)),
        out_shape=jax.ShapeDtypeStruct((ROWS, D), BF16),
        compiler_params=_cparams(("parallel", "arbitrary")),
        name="even_mid",
    )(p, jnp.asarray(icnt), pool_w, pool_scale, cnp, snp, cns, sns, cc, sc)


def _even_in_mid_kernel(a_ref, w_ref, icnt_ref, pw_ref, ps_ref, cnp_ref, snp_ref, cns_ref, sns_ref, cc_ref, sc_ref, o_ref):
    j = pl.program_id(1)
    is_prompt = pl.program_id(0) * RB < PROMPT_ROWS
    p = jnp.dot(a_ref[...], w_ref[...], preferred_element_type=F32)

    def pool(half, seq_len):
        pooled = _window_sum(p, half, seq_len) * icnt_ref[...] - p
        y = jnp.dot(pooled.astype(BF16), pw_ref[...], preferred_element_type=F32)
        o_ref[...] = (y * ps_ref[...]).astype(o_ref.dtype)

    def fourier(cn_ref, sn_ref, seq_len):
        x = p.astype(BF16)
        xc = jnp.dot(x, cc_ref[...], preferred_element_type=F32).astype(BF16)
        xs = jnp.dot(x, sc_ref[...], preferred_element_type=F32).astype(BF16)
        for s in range(RB // seq_len):
            rs = slice(s * seq_len, (s + 1) * seq_len)
            y = (jnp.dot(cn_ref[...], xc[rs], preferred_element_type=F32)
                 - jnp.dot(sn_ref[...], xs[rs], preferred_element_type=F32))
            o_ref[rs, :] = y.astype(o_ref.dtype)

    for gi, w in enumerate(POOL_WINDOWS):
        pl.when((j == gi) & is_prompt)(functools.partial(pool, w // 2, PROMPT_LEN))
        pl.when((j == gi) & jnp.logical_not(is_prompt))(functools.partial(pool, w // 2, SAMPLE_LEN))
    pl.when((j >= 4) & is_prompt)(functools.partial(fourier, cnp_ref, snp_ref, PROMPT_LEN))
    pl.when((j >= 4) & jnp.logical_not(is_prompt))(functools.partial(fourier, cns_ref, sns_ref, SAMPLE_LEN))


def _even_in_mid(u, w, pool_w, pool_scale):
    icnt, *dft = _even_tables()
    cnp, snp, cns, sns, cc, sc = (jnp.asarray(t).astype(BF16) for t in dft)
    kind = lambda i: jnp.where(i * RB < PROMPT_ROWS, 0, 1)
    grp = lambda j: jnp.minimum(j, 3)
    whole = lambda t: pl.BlockSpec(t.shape, lambda i, j: (0, 0))
    return pl.pallas_call(
        _even_in_mid_kernel,
        grid=(ROWS // RB, 8),
        in_specs=[pl.BlockSpec((RB, D), lambda i, j: (i, 0)),
                  pl.BlockSpec((D, GW), lambda i, j: (0, j)),
                  pl.BlockSpec((None, None, RB, 1), lambda i, j: (kind(i), grp(j), 0, 0)),
                  pl.BlockSpec((None, GW, GW), lambda i, j: (grp(j), 0, 0)),
                  pl.BlockSpec((None, 1, GW), lambda i, j: (grp(j), 0, 0)),
                  whole(cnp), whole(snp), whole(cns), whole(sns), whole(cc), whole(sc)],
        out_specs=pl.BlockSpec((RB, GW), lambda i, j: (i, j)),
        out_shape=jax.ShapeDtypeStruct((ROWS, D), BF16),
        compiler_params=_cparams(("parallel", "arbitrary")),
        name="even_in_mid",
    )(u, w, jnp.asarray(icnt), pool_w, pool_scale, cnp, snp, cns, sns, cc, sc)


def _head_sum_matrix(width, value):
    h = np.arange(width) // HEAD
    return jnp.asarray((h[:, None] == h[None, :]).astype(np.float32) * value).astype(BF16)


def _split_bf16(x, n):
    parts = []
    for _ in range(n - 1):
        h = x.astype(BF16)
        parts.append(h)
        x = x - h.astype(F32)
    parts.append(x.astype(BF16))
    return parts


def _head_sum(x, e):
    hi, lo = _split_bf16(x, 2)
    return jnp.dot(hi, e, preferred_element_type=F32) + jnp.dot(lo, e, preferred_element_type=F32)


def _token_shift(x, mu, seq_len):
    n = x.shape[0]
    pos = lax.broadcasted_iota(jnp.int32, x.shape, 0) % seq_len
    prev = jnp.where(pos == 0, 0.0, pltpu.roll(x, 1, 0))
    nxt = jnp.where(pos == seq_len - 1, 0.0, pltpu.roll(x, n - 1, 0))
    return x + (0.5 * (prev + nxt) - x) * mu


def _rwkv_prep_kernel(pr_ref, pk_ref, pv_ref, pg_ref, pw_ref, pa_ref,
                      mr_ref, mk_ref, mv_ref, mg_ref, mw_ref, ma_ref,
                      w0_ref, w2_ref, a0_ref, aw2_ref, gw2_ref, kk_ref, ka_ref, e_ref,
                      r_o, k_o, v_o, kk_o, kka_o, lw_o, g_o, sg_ref, tw_ref, al_ref):
    i = pl.program_id(0)

    def body(seq_len):
        @pl.when(pl.program_id(1) == 0)
        def _():
            sg_ref[...] = jax.nn.sigmoid(_token_shift(pg_ref[...], mg_ref[...], seq_len)).astype(BF16)
            tw_ref[...] = jnp.tanh(_token_shift(pw_ref[...], mw_ref[...], seq_len)).astype(BF16)
            al_ref[...] = _token_shift(pa_ref[...], ma_ref[...], seq_len).astype(BF16)

        r = _token_shift(pr_ref[...], mr_ref[...], seq_len)
        k = _token_shift(pk_ref[...], mk_ref[...], seq_len)
        v = _token_shift(pv_ref[...], mv_ref[...], seq_len)
        for d in range(2):
            x = w0_ref[d] + jnp.dot(tw_ref[:, d * DECAY_LORA:(d + 1) * DECAY_LORA], w2_ref[d],
                                    preferred_element_type=F32)
            z = -x
            softplus = jnp.maximum(z, 0.0) + jnp.log(1.0 + jnp.exp(-jnp.abs(z)))
            lw_o[d] = -jnp.exp(-softplus - 0.5)
        a = jax.nn.sigmoid(a0_ref[...] + jnp.dot(al_ref[...], aw2_ref[...], preferred_element_type=F32))
        g_o[...] = jnp.dot(sg_ref[...], gw2_ref[...], preferred_element_type=F32)
        kk = k * kk_ref[...]
        ss = _head_sum(kk * kk, e_ref[...])
        kk = kk * lax.rsqrt(ss + 1e-12)
        r_o[...] = r
        k_o[...] = k * (1.0 + (a - 1.0) * ka_ref[...])
        v_o[...] = v
        kk_o[...] = kk
        kka_o[...] = kk * a

    @pl.when(i * RB < PROMPT_ROWS)
    def _():
        body(PROMPT_LEN)

    @pl.when(i * RB >= PROMPT_ROWS)
    def _():
        body(SAMPLE_LEN)


def _rwkv_prep(p, mu, w0, w2, a0, aw2, gw2, k_k, k_a):
    cw = 256
    ncb = RWKV_W // cw
    e = _head_sum_matrix(cw, 1.0)
    pcol = lambda base, width: pl.BlockSpec((RB, width), lambda i, c: (i, base // width))
    pch = lambda base: pl.BlockSpec((RB, cw), lambda i, c: (i, base // cw + c))
    mcol = lambda base, width: pl.BlockSpec((1, width), lambda i, c: (0, base // width))
    mch = lambda base: pl.BlockSpec((1, cw), lambda i, c: (0, base // cw + c))
    vch = pl.BlockSpec((1, cw), lambda i, c: (0, c))
    out = pl.BlockSpec((RB, cw), lambda i, c: (i, c))
    sds = jax.ShapeDtypeStruct((ROWS, RWKV_W), F32)
    return pl.pallas_call(
        _rwkv_prep_kernel,
        grid=(ROWS // RB, ncb),
        in_specs=[pch(C_R), pch(C_K), pch(C_V), pcol(C_G, GATE_PAD), pcol(C_W, 2 * DECAY_LORA), pcol(C_A, ICLR_LORA),
                  mch(C_R), mch(C_K), mch(C_V), mcol(C_G, GATE_PAD), mcol(C_W, 2 * DECAY_LORA), mcol(C_A, ICLR_LORA),
                  pl.BlockSpec((2, 1, cw), lambda i, c: (0, 0, c)),
                  pl.BlockSpec((2, DECAY_LORA, cw), lambda i, c: (0, 0, c)),
                  vch,
                  pl.BlockSpec((ICLR_LORA, cw), lambda i, c: (0, c)),
                  pl.BlockSpec((GATE_PAD, cw), lambda i, c: (0, c)),
                  vch, vch,
                  pl.BlockSpec((cw, cw), lambda i, c: (0, 0))],
        out_specs=[out, out, out, out, out,
                   pl.BlockSpec((2, RB, cw), lambda i, c: (0, i, c)), out],
        out_shape=[sds, sds, sds, sds, sds, jax.ShapeDtypeStruct((2, ROWS, RWKV_W), F32), sds],
        scratch_shapes=[pltpu.VMEM((RB, GATE_PAD), BF16), pltpu.VMEM((RB, 2 * DECAY_LORA), BF16),
                        pltpu.VMEM((RB, ICLR_LORA), BF16)],
        compiler_params=_cparams(("parallel", "arbitrary")),
        name="rwkv_prep",
    )(p, p, p, p, p, p, mu, mu, mu, mu, mu, mu, w0, w2, a0, aw2, gw2, k_k, k_a, e)


def _mmp(a, b, nt=False):
    dims = (((1,), (1 if nt else 0,)), ((), ()))
    return lax.dot_general(a.astype(BF16), b.astype(BF16), dims, preferred_element_type=F32)


def _rwkv_scan_kernel(*refs, has_init, pairs):
    in_refs = (refs[0:6], refs[6:12])
    s0_ref = refs[12] if has_init else None
    y_refs = refs[-4:-2]
    s_ref, sp_ref = refs[-2:]
    c_idx = pl.program_id(1)
    n_chunks = pl.num_programs(1)
    L = CHUNK
    W2 = 2 * HEAD

    @pl.when(c_idx == 0)
    def _():
        sp_ref[...] = jnp.zeros_like(sp_ref)
        if has_init:
            for d in range(2):
                for p in range(pairs):
                    sp_ref[d * pairs + p, 0:HEAD, 0:HEAD] = s0_ref[d, 2 * p]
                    sp_ref[d * pairs + p, HEAD:W2, HEAD:W2] = s0_ref[d, 2 * p + 1]

    row = lax.broadcasted_iota(jnp.int32, (L, W2), 0)
    lane = lax.broadcasted_iota(jnp.int32, (L, W2), 1)
    col = lane % HEAD
    first = lane < HEAD
    incl_d = (col <= row, col >= row)
    strict_d = (col < row, col > row)
    eye = (col == row).astype(F32)
    blk8 = (row // 8) == (col // 8)
    off = [((row // (2 * b)) == (col // (2 * b))) & ((row // b) != (col // b)) for b in (8, 16, 32)]
    row_s = lax.broadcasted_iota(jnp.int32, (W2, W2), 0)
    lane_s = lax.broadcasted_iota(jnp.int32, (W2, W2), 1)
    diag_blocks = (row_s < HEAD) == (lane_s < HEAD)
    trow = lax.broadcasted_iota(jnp.int32, (L, L), 0)
    tcol = lax.broadcasted_iota(jnp.int32, (L, L), 1)
    tri_d = ((tcol <= trow).astype(BF16), (tcol >= trow).astype(BF16))

    first_b = lane.astype(F32).astype(BF16) < HEAD

    def bd(x):
        xb = x.astype(BF16)
        zero = jnp.zeros_like(xb)
        return jnp.concatenate([jnp.where(first_b, xb, zero), jnp.where(first_b, zero, xb)], axis=0)

    ps = [slice(p * W2, (p + 1) * W2) for p in range(pairs)]
    each = lambda f, *xs: [f(*z) for z in zip(*xs)]
    ar, bt, kt, v, g_last = [], [], [], [], []
    for d in range(2):
        lw_ref, r_ref, k_ref, v_ref, a_ref, b_ref = in_refs[d]
        lw = lw_ref[...]
        c = sum(jnp.dot(tri_d[d], t, preferred_element_type=F32) for t in _split_bf16(lw, 3))
        g_in = jnp.exp(c)
        g_inv = jnp.exp(-c)
        at_all = a_ref[...] * jnp.exp(c - lw)
        rt_all = r_ref[...] * g_in
        bt_all = b_ref[...] * g_inv
        kt_all = k_ref[...] * g_inv
        v_all = v_ref[...]
        g_end = g_in[0:1, :] if d else g_in[L - 1:L, :]
        ar += [jnp.concatenate([at_all[:, s], rt_all[:, s]], axis=0) for s in ps]
        bt += [bt_all[:, s] for s in ps]
        kt += [kt_all[:, s] for s in ps]
        v += [v_all[:, s] for s in ps]
        g_last += [g_end[:, s] for s in ps]
    n_inst = 2 * pairs
    strict = [strict_d[i // pairs] for i in range(n_inst)]
    incl = [incl_d[i // pairs] for i in range(n_inst)]
    s0 = [sp_ref[i] for i in range(n_inst)]
    sc = each(lambda x, b, k, s: _mmp(x, jnp.concatenate([bd(b), bd(k), s.astype(BF16)], axis=0), nt=True),
              ar, bt, kt, s0)
    npq = [x[:, 0:W2] for x in sc]
    mq = [x[:, W2:2 * W2] for x in sc]
    ws = [x[:, 2 * W2:3 * W2] for x in sc]
    n_mat = each(lambda m, x: jnp.where(m, x[:L], 0.0), strict, npq)
    m_mat = each(lambda m, x: jnp.where(m, x[:L], 0.0), strict, mq)
    p_mat = each(lambda m, x: jnp.where(m, x[L:], 0.0), incl, npq)
    q_mat = each(lambda m, x: jnp.where(m, x[L:], 0.0), incl, mq)
    mqv = each(lambda m, q, y: _mmp(jnp.concatenate([m, q], axis=0), bd(y)), m_mat, q_mat, v)
    w = each(lambda x, z: x[:L] + z[:L], ws, mqv)
    n8 = [jnp.where(blk8, x, 0.0) for x in n_mat]
    n8_2 = each(lambda x: _mmp(x, bd(x)), n8)
    n8_4 = each(lambda x: _mmp(x, bd(x)), n8_2)
    t = each(lambda x, y: _mmp(eye - x, bd(eye + y)), n8, n8_2)
    t = each(lambda x, y: _mmp(x, bd(eye + y)), t, n8_4)
    for m in off:
        x = each(lambda n, y: _mmp(jnp.where(m, n, 0.0), bd(y)), n_mat, t)
        t = each(lambda y, z: y - _mmp(y, bd(z)), t, x)
    u = each(lambda x, y: -_mmp(x, bd(y)), t, w)
    y = each(lambda x, z, pm, uu: x[L:] + z[L:] + _mmp(pm, bd(uu)), ws, mqv, p_mat, u)
    for i in range(n_inst):
        y_refs[i // pairs][:, ps[i % pairs]] = y[i]
    upd = each(lambda uu, vv, b, k: _mmp(jnp.concatenate([uu, vv], axis=0).T,
                                         jnp.concatenate([b, k], axis=0)), u, v, bt, kt)
    s_new = each(lambda s, x, g: (s + jnp.where(diag_blocks, x, 0.0)) * g, s0, upd, g_last)
    for i in range(n_inst):
        sp_ref[i] = s_new[i]

    @pl.when(c_idx == n_chunks - 1)
    def _():
        for i in range(n_inst):
            d, p = divmod(i, pairs)
            s_ref[d, 2 * p] = s_new[i][0:HEAD, 0:HEAD]
            s_ref[d, 2 * p + 1] = s_new[i][HEAD:W2, HEAD:W2]


def _rwkv_scan(lw, r, k, v, a, b, s0, y_prev, *, n_seq, seq_len, row0):
    pairs = N_HEADS // 2
    nc = seq_len // CHUNK

    def rows(d):
        return lambda s, c: row0 // CHUNK + s * nc + (nc - 1 - c if d else c)

    in_specs, args = [], []
    for d in range(2):
        in_specs.append(pl.BlockSpec((None, CHUNK, RWKV_W), lambda s, c, d=d: (d, rows(d)(s, c), 0)))
        in_specs += [pl.BlockSpec((CHUNK, RWKV_W), lambda s, c, d=d: (rows(d)(s, c), 0))] * 5
        args += [lw, r, k, v, a, b]
    if s0 is not None:
        in_specs.append(pl.BlockSpec((None, 2, N_HEADS, HEAD, HEAD), lambda s, c: (s, 0, 0, 0, 0)))
        args.append(s0)
    aliases = {}
    if y_prev is not None:
        aliases = {len(args): 0, len(args) + 1: 1}
        in_specs += [pl.BlockSpec(memory_space=pl.ANY)] * 2
        args += list(y_prev)
    y_sds = jax.ShapeDtypeStruct((ROWS, RWKV_W), F32)
    return pl.pallas_call(
        functools.partial(_rwkv_scan_kernel, has_init=s0 is not None, pairs=pairs),
        grid=(n_seq, nc),
        in_specs=in_specs,
        out_specs=[pl.BlockSpec((CHUNK, RWKV_W), lambda s, c: (rows(0)(s, c), 0)),
                   pl.BlockSpec((CHUNK, RWKV_W), lambda s, c: (rows(1)(s, c), 0)),
                   pl.BlockSpec((None, 2, N_HEADS, HEAD, HEAD), lambda s, c: (s, 0, 0, 0, 0))],
        out_shape=[y_sds, y_sds, jax.ShapeDtypeStruct((n_seq, 2, N_HEADS, HEAD, HEAD), F32)],
        scratch_shapes=[pltpu.VMEM((2 * pairs, 2 * HEAD, 2 * HEAD), F32)],
        input_output_aliases=aliases,
        compiler_params=_cparams(("parallel", "arbitrary")),
        name="rwkv_scan",
    )(*args)


def _rwkv_post_kernel(yf_ref, yb_ref, r_ref, k_ref, v_ref, g_ref, gg_ref, gb_ref, rk_ref, e_ref, o_ref):
    y = yf_ref[...] + yb_ref[...]
    e = e_ref[...]
    mu = _head_sum(y, e)
    yc = y - mu
    var = _head_sum(yc * yc, e)
    yn = yc * lax.rsqrt(var + GN_EPS) * gg_ref[...] + gb_ref[...]
    bonus = _head_sum(r_ref[...] * k_ref[...] * rk_ref[...], e) * float(HEAD) * v_ref[...]
    o_ref[...] = ((yn + bonus) * g_ref[...]).astype(o_ref.dtype)


def _rwkv_post(yf, yb, r, k, v, g, gn_g, gn_b, r_k):
    tr, cw = 512, 512
    e = _head_sum_matrix(cw, 1.0 / HEAD)
    blk = pl.BlockSpec((tr, cw), lambda i, c: (i, c))
    vec = pl.BlockSpec((1, cw), lambda i, c: (0, c))
    return pl.pallas_call(
        _rwkv_post_kernel,
        grid=(ROWS // tr, RWKV_W // cw),
        in_specs=[blk, blk, blk, blk, blk, blk, vec, vec, vec, pl.BlockSpec((cw, cw), lambda i, c: (0, 0))],
        out_specs=blk,
        out_shape=jax.ShapeDtypeStruct((ROWS, RWKV_W), BF16),
        compiler_params=_cparams(("parallel", "parallel")),
        name="rwkv_post",
    )(yf, yb, r, k, v, g, gn_g, gn_b, r_k, e)


def _rms(x, gain):
    return x * lax.rsqrt(jnp.mean(x * x, axis=-1, keepdims=True) + 1e-6) * gain


def _rope(x, cos, sin_signed):
    lane = lax.broadcasted_iota(jnp.int32, x.shape, 1)
    quarter = HEAD_DIM // 4
    partner = jnp.where(lane % (2 * quarter) < quarter,
                        pltpu.roll(x, HEAD_DIM - quarter, 1), pltpu.roll(x, quarter, 1))
    return x * cos + partner * sin_signed


def _rope_tables():
    n = SAMPLE_LEN
    quarter = HEAD_DIM // 4
    row = (np.arange(n) // GRID_W).astype(np.float64)
    col = (np.arange(n) % GRID_W).astype(np.float64)
    inv_freq = ROPE_THETA ** (-np.arange(quarter, dtype=np.float64) / quarter)
    ar = row[:, None] * inv_freq[None, :]
    ac = col[:, None] * inv_freq[None, :]
    cos = np.concatenate([np.cos(ar), np.cos(ar), np.cos(ac), np.cos(ac)], axis=1)
    sin = np.concatenate([-np.sin(ar), np.sin(ar), -np.sin(ac), np.sin(ac)], axis=1)
    return cos.astype(np.float32), sin.astype(np.float32)


def _attn_prompt_kernel(q_ref, k_ref, v_ref, qn_ref, kn_ref, o_ref, ko_ref, vo_ref):
    scale = HEAD_DIM ** -0.5
    kn = _rms(k_ref[...], kn_ref[...])
    v = v_ref[...]
    ko_ref[...] = kn
    vo_ref[...] = v
    kb = kn.astype(BF16)
    vb = v.astype(BF16)
    n = q_ref.shape[0]
    heads = [slice(j * HEAD_DIM, (j + 1) * HEAD_DIM) for j in range(ATT_W // KV_W)]
    q = jnp.concatenate([(_rms(q_ref[:, hs], qn_ref[...]) * scale).astype(BF16) for hs in heads], axis=0)
    s = lax.dot_general(q, kb, (((1,), (1,)), ((), ())), preferred_element_type=F32)
    e = jnp.exp(s - jnp.max(s, axis=-1, keepdims=True))
    inv = 1.0 / jnp.sum(e, axis=-1, keepdims=True)
    o = (jnp.dot(e.astype(BF16), vb, preferred_element_type=F32) * inv).astype(o_ref.dtype)
    for j, hs in enumerate(heads):
        o_ref[:, hs] = o[j * n:(j + 1) * n]


def _attn_prompt(p, q_norm, k_norm):
    n = PROMPT_LEN
    qw = ATT_W // N_KV
    return pl.pallas_call(
        _attn_prompt_kernel,
        grid=(N_PROMPT_SEQ, N_KV),
        in_specs=[pl.BlockSpec((n, qw), lambda s, g: (s, C_Q // qw + g)),
                  pl.BlockSpec((n, HEAD_DIM), lambda s, g: (s, C_AK // HEAD_DIM + g)),
                  pl.BlockSpec((n, HEAD_DIM), lambda s, g: (s, C_AV // HEAD_DIM + g)),
                  pl.BlockSpec((1, HEAD_DIM), lambda s, g: (0, 0)),
                  pl.BlockSpec((1, HEAD_DIM), lambda s, g: (0, 0))],
        out_specs=[pl.BlockSpec((n, qw), lambda s, g: (s, g)),
                   pl.BlockSpec((n, HEAD_DIM), lambda s, g: (s, g)),
                   pl.BlockSpec((n, HEAD_DIM), lambda s, g: (s, g))],
        out_shape=[jax.ShapeDtypeStruct((ROWS, ATT_W), BF16),
                   jax.ShapeDtypeStruct((PROMPT_ROWS, KV_W), F32),
                   jax.ShapeDtypeStruct((PROMPT_ROWS, KV_W), F32)],
        compiler_params=_cparams(("parallel", "parallel")),
        name="attn_prompt",
    )(p, p, p, q_norm, k_norm)


def _attn_sample_kernel(q_ref, k_ref, v_ref, ck_ref, cv_ref, qn_ref, kn_ref, cos_ref, sin_ref, _o_prev, o_ref):
    scale = HEAD_DIM ** -0.5
    tq = 256
    cos = cos_ref[...]
    sin = sin_ref[...]
    ks = _rope(_rms(k_ref[...], kn_ref[...]), cos, sin).astype(BF16)
    vs = v_ref[...].astype(BF16)
    kc = ck_ref[...].astype(BF16)
    vc = cv_ref[...].astype(BF16)
    nt = (((1,), (1,)), ((), ()))
    for j in range(ATT_W // KV_W):
        hs = slice(j * HEAD_DIM, (j + 1) * HEAD_DIM)
        for qb in range(SAMPLE_LEN // tq):
            rs = slice(qb * tq, (qb + 1) * tq)
            q = (_rope(_rms(q_ref[rs, hs], qn_ref[...]), cos[rs], sin[rs]) * scale).astype(BF16)
            s1 = lax.dot_general(q, kc, nt, preferred_element_type=F32)
            s2 = lax.dot_general(q, ks, nt, preferred_element_type=F32)
            m = jnp.maximum(jnp.max(s1, axis=-1, keepdims=True), jnp.max(s2, axis=-1, keepdims=True))
            e1 = jnp.exp(s1 - m)
            e2 = jnp.exp(s2 - m)
            inv = 1.0 / (jnp.sum(e1, axis=-1, keepdims=True) + jnp.sum(e2, axis=-1, keepdims=True))
            o = (jnp.dot(e1.astype(BF16), vc, preferred_element_type=F32)
                 + jnp.dot(e2.astype(BF16), vs, preferred_element_type=F32))
            o_ref[rs, hs] = (o * inv).astype(o_ref.dtype)


def _attn_sample(p, cache_k, cache_v, q_norm, k_norm, o_prev):
    n = SAMPLE_LEN
    qw = ATT_W // N_KV
    r0 = PROMPT_ROWS // n
    cos, sin = _rope_tables()
    return pl.pallas_call(
        _attn_sample_kernel,
        grid=(N_SAMPLE_SEQ, N_KV),
        in_specs=[pl.BlockSpec((n, qw), lambda s, g: (r0 + s, C_Q // qw + g)),
                  pl.BlockSpec((n, HEAD_DIM), lambda s, g: (r0 + s, C_AK // HEAD_DIM + g)),
                  pl.BlockSpec((n, HEAD_DIM), lambda s, g: (r0 + s, C_AV // HEAD_DIM + g)),
                  pl.BlockSpec((PAST_LEN, HEAD_DIM), lambda s, g: (s, g)),
                  pl.BlockSpec((PAST_LEN, HEAD_DIM), lambda s, g: (s, g)),
                  pl.BlockSpec((1, HEAD_DIM), lambda s, g: (0, 0)),
                  pl.BlockSpec((1, HEAD_DIM), lambda s, g: (0, 0)),
                  pl.BlockSpec((n, HEAD_DIM), lambda s, g: (0, 0)),
                  pl.BlockSpec((n, HEAD_DIM), lambda s, g: (0, 0)),
                  pl.BlockSpec(memory_space=pl.ANY)],
        out_specs=pl.BlockSpec((n, qw), lambda s, g: (r0 + s, g)),
        out_shape=jax.ShapeDtypeStruct((ROWS, ATT_W), BF16),
        input_output_aliases={9: 0},
        compiler_params=_cparams(("parallel", "parallel")),
        name="attn_sample",
    )(p, p, p, cache_k, cache_v, q_norm, k_norm, jnp.asarray(cos), jnp.asarray(sin), o_prev)


_ODD_SEGMENTS = ((0, C_R, 6144), (7008, C_Q, ATT_W), (9056, C_AK, KV_W), (9568, C_AV, KV_W),
                 (6144, C_G, GATE_LORA), (6624, C_W, 2 * DECAY_LORA), (6880, C_A, ICLR_LORA))
_ODD_ZERO = ((C_G + GATE_LORA, GATE_PAD - GATE_LORA), (C_A + ICLR_LORA, ODD_COLS - C_A - ICLR_LORA))


def _pack_odd_job(wt):
    b = PACK_ROWS
    assert all(v % b == 0 for seg in _ODD_SEGMENTS + _ODD_ZERO for v in seg)
    runs = [(dst // b, (dst + width) // b, (src - dst) // b) for src, dst, width in _ODD_SEGMENTS]
    zeros = tuple(z for dst, width in _ODD_ZERO for z in range(dst // b, (dst + width) // b))

    def src_block(blk):
        out = jnp.zeros_like(blk)
        for lo, hi, delta in runs:
            out = jnp.where((blk >= lo) & (blk < hi), blk + delta, out)
        return out

    return _CastJob(wt, (), ODD_COLS, src_block, zeros, b)


def _pack_shift_mu(mu):
    z = lambda n: jnp.zeros((n,), mu.dtype)
    out = jnp.concatenate([mu[0:6144], z(C_G - C_Q), mu[6144:6144 + GATE_LORA], z(GATE_PAD - GATE_LORA),
                           mu[6624:6880], mu[6880:7008], z(ODD_COLS - C_A - ICLR_LORA)])
    return out[None, :]


def kernel(x_prompt, x_sample, cache_k, cache_v, state_rwkv, c, c_ctx, w_ada, b_ada, ln_g, ln_b, ffn_w_in,
           ffn_w_out, even_w_in, pool_w, pool_scale, even_w_out, odd_w_in, shift_mu, decay_w0, decay_w2,
           iclr_a0, iclr_w2, gate_w2, k_k, k_a, r_k, gn_g, gn_b, q_norm, k_norm, odd_w_out):
    x = (x_prompt.reshape(PROMPT_ROWS, D), x_sample.reshape(ROWS - PROMPT_ROWS, D))
    cond =jnp.concatenate([c_ctx[None, :], c, jnp.zeros((N_GROUPS - 1 - c.shape[0], D), F32)], axis=0)
    mods = _ada(cond, w_ada, b_ada[:, None, :])
    mods = mods.reshape(DEPTH, N_GROUPS, 3 * N_SUB, D).transpose(0, 2, 1, 3)[:, :, :, None, :]
    ln_g4 = ln_g[:, :, None, :]
    ln_b4 = ln_b[:, :, None, :]

    u = _modulate(x, mods, 0, 0)
    new_k = new_v = new_s = None
    for l in range(DEPTH):
        i = l // 2
        if l % 2 == 0:
            mixer_w = [_CastJob(even_w_in, (i,)), _CastJob(even_w_out, (i,))]
        else:
            mixer_w = [_pack_odd_job(jnp.swapaxes(odd_w_in[i], 0, 1)), _CastJob(odd_w_out, (i,))]
        h, w_out_b, *mixer_wb = _swiglu_in(u, ffn_w_in, l, 0, [_CastJob(ffn_w_out, (l, 0))] + mixer_w)
        z = _mm([h], w_out_b, 512, 512, F32, "ffn_out", resid=(x, mods, l, 0, 0.5))
        x, u = _ln(z, mods, ln_g4, ln_b4, l, 0, (l, 1))
        if l % 2 == 0:
            mid = _even_in_mid(u, mixer_wb[0], pool_w[i].astype(BF16), pool_scale[i].reshape(4, 1, GW))
            z = _mm([mid], mixer_wb[1], 1024, 512, F32, "even_out", resid=(x, mods, l, 1, 1.0))
        else:
            p = _mm([u], mixer_wb[0], 1024, 512, F32, "odd_in", w_rows_out=True)
            gw2 = jnp.concatenate([gate_w2[i], jnp.zeros((GATE_PAD - GATE_LORA, RWKV_W), F32)], axis=0)
            r, k, v, kk, kka, lw, g = _rwkv_prep(
                p, _pack_shift_mu(shift_mu[i]), decay_w0[i][:, None, :], decay_w2[i].astype(BF16),
                iclr_a0[i][None, :], iclr_w2[i].astype(BF16), gw2.astype(BF16), k_k[i][None, :], k_a[i][None, :])
            yf, yb, s_prompt = _rwkv_scan(lw, r, k, v, kk, kka, None, None,
                                          n_seq=N_PROMPT_SEQ, seq_len=PROMPT_LEN, row0=0)
            yf, yb, _ = _rwkv_scan(lw, r, k, v, kk, kka, state_rwkv[:, i], (yf, yb),
                                   n_seq=N_SAMPLE_SEQ, seq_len=SAMPLE_LEN, row0=PROMPT_ROWS)
            y_rwkv = _rwkv_post(yf, yb, r, k, v, g, gn_g[i][None, :], gn_b[i][None, :],
                                r_k[i].reshape(1, RWKV_W))
            o_p, nk, nv = _attn_prompt(p, q_norm[i][None, :], k_norm[i][None, :])
            o = _attn_sample(p, cache_k[:, i].reshape(N_SAMPLE_SEQ * PAST_LEN, KV_W),
                             cache_v[:, i].reshape(N_SAMPLE_SEQ * PAST_LEN, KV_W),
                             q_norm[i][None, :], k_norm[i][None, :], o_p)
            z = _mm([y_rwkv, o], mixer_wb[1], 1024, 512, F32, "odd_out", resid=(x, mods, l, 1, 1.0))
            new_k = nk.reshape(N_PROMPT_SEQ, 1, PROMPT_LEN, N_KV, HEAD_DIM)
            new_v = nv.reshape(N_PROMPT_SEQ, 1, PROMPT_LEN, N_KV, HEAD_DIM)
            new_s = s_prompt[:, None]
        x, u = _ln(z, mods, ln_g4, ln_b4, l, 1, (l, 2))
        h, w_out_b = _swiglu_in(u, ffn_w_in, l, 1, [_CastJob(ffn_w_out, (l, 1))])
        z = _mm([h], w_out_b, 512, 512, F32, "ffn_out", resid=(x, mods, l, 2, 0.5))
        if l + 1 < DEPTH:
            x, u = _ln(z, mods, ln_g4, ln_b4, l, 2, (l + 1, 0))
        else:
            y_prompt = _ln(z, mods, ln_g4, ln_b4, l, 2, None, 0, PROMPT_ROWS)
            y_sample = _ln(z, mods, ln_g4, ln_b4, l, 2, None, PROMPT_ROWS, ROWS - PROMPT_ROWS)

    return (y_prompt.reshape(N_PROMPT_SEQ, PROMPT_LEN, D), y_sample.reshape(N_SAMPLE_SEQ, SAMPLE_LEN, D),
            new_k, new_v, new_s)
```
